```python
import jax, jax.numpy as jnp
from jax import lax
import numpy as np


D_MODEL = 1024
BATCH = 4
SEQ = 4096
DEPTH = 2

GRID_W = 64
CTX_LEN = 256
EPS = 1e-6

N_HEADS = 8
N_KV_HEADS = 2
HEAD_DIM = 64
GQA_GROUP = N_HEADS // N_KV_HEADS
WINDOW = 128
ATTN_BLOCK = 128
ROPE_BASE = 10000.0
ROPE_PAIRS = HEAD_DIM // 4

CONV_CH = 512
CONV_K = 31
CONV_PAD = CONV_K // 2

POOL_WINDOWS = (2, 4, 8, 16)
POOL_GROUP = 128
POOL_CH = POOL_GROUP * len(POOL_WINDOWS)

N_EXPERTS = 16
CAPACITY_FACTOR = 2
D_EXPERT = 1024

Q_W = N_HEADS * HEAD_DIM
KV_W = N_KV_HEADS * HEAD_DIM
N_BRANCHES = 3
IN_OFFSETS = (Q_W, Q_W + KV_W, Q_W + 2 * KV_W, Q_W + 2 * KV_W + 2 * CONV_CH,
              Q_W + 2 * KV_W + 2 * CONV_CH + POOL_CH)
IN_W = IN_OFFSETS[-1] + N_BRANCHES * D_MODEL

kernel_name = 'hybrid_gated_branch_dit_block'

F32 = jnp.float32


def rmsnorm(x, g):
    xf = x.astype(F32)
    y = xf * lax.rsqrt(jnp.mean(xf * xf, axis=-1, keepdims=True) + EPS)
    return (y * g.astype(F32)).astype(x.dtype)


def layernorm(x, g, b):
    xf = x.astype(F32)
    mu = jnp.mean(xf, axis=-1, keepdims=True)
    var = jnp.mean(jnp.square(xf - mu), axis=-1, keepdims=True)
    return ((xf - mu) * lax.rsqrt(var + EPS) * g.astype(F32) + b.astype(F32)).astype(x.dtype)


def modulate(h, shift, scale):
    return h * (1 + scale) + shift


def rope_tables(n):
    rows = n // GRID_W
    row = jnp.broadcast_to(jnp.arange(rows)[:, None], (rows, GRID_W)).reshape(-1).astype(F32)
    col = jnp.broadcast_to(jnp.arange(GRID_W)[None, :], (rows, GRID_W)).reshape(-1).astype(F32)
    freqs = ROPE_BASE ** (-jnp.arange(ROPE_PAIRS, dtype=F32) / ROPE_PAIRS)
    ang_r = row[:, None] * freqs
    ang_c = col[:, None] * freqs
    return (jnp.cos(ang_r), jnp.sin(ang_r), jnp.cos(ang_c), jnp.sin(ang_c))


def _rotate(x1, x2, cos, sin):
    return x1 * cos - x2 * sin, x2 * cos + x1 * sin


def apply_axial_rope(x, tabs):
    cos_r, sin_r, cos_c, sin_c = [t[:, None, :].astype(x.dtype) for t in tabs]
    xr1, xr2, xc1, xc2 = jnp.split(x, 4, axis=-1)
    r1, r2 = _rotate(xr1, xr2, cos_r, sin_r)
    c1, c2 = _rotate(xc1, xc2, cos_c, sin_c)
    return jnp.concatenate([r1, r2, c1, c2], axis=-1)


def heads(z, n):
    return z.reshape(z.shape[0], z.shape[1], n, HEAD_DIM)


def window_attention(q, k, v, k_ctx, v_ctx, sink):
    B, S = q.shape[:2]
    L = k_ctx.shape[1]
    nb = S // ATTN_BLOCK
    qb = q.reshape(B, nb, ATTN_BLOCK, N_KV_HEADS, GQA_GROUP, HEAD_DIM) * (HEAD_DIM ** -0.5)
    pad = ((0, 0), (1, 1), (0, 0), (0, 0), (0, 0))
    kp = jnp.pad(k.reshape(B, nb, ATTN_BLOCK, N_KV_HEADS, HEAD_DIM), pad)
    vp = jnp.pad(v.reshape(B, nb, ATTN_BLOCK, N_KV_HEADS, HEAD_DIM), pad)
    kw = jnp.concatenate([kp[:, :-2], kp[:, 1:-1], kp[:, 2:]], axis=2)
    vw = jnp.concatenate([vp[:, :-2], vp[:, 1:-1], vp[:, 2:]], axis=2)
    s_win = jnp.einsum('bnqkgd,bnskd->bnkgqs', qb, kw).astype(F32)
    qi = jnp.arange(ATTN_BLOCK)[:, None]
    kj = jnp.arange(3 * ATTN_BLOCK)[None, :]
    band = jnp.abs(kj - ATTN_BLOCK - qi) <= WINDOW
    kpos = (jnp.arange(nb)[:, None, None] - 1) * ATTN_BLOCK + kj[None]
    mask = band[None] & (kpos >= 0) & (kpos < S)
    s_win = jnp.where(mask[None, :, None, None], s_win, -1e30)
    s_ctx = jnp.einsum('bnqkgd,blkd->bnkgql', qb, k_ctx).astype(F32)
    sink_col = jnp.broadcast_to(sink.reshape(N_KV_HEADS, GQA_GROUP, 1, 1).astype(F32),
                                (B, nb, N_KV_HEADS, GQA_GROUP, ATTN_BLOCK, 1))
    p = jax.nn.softmax(jnp.concatenate([s_win, s_ctx, sink_col], axis=-1), axis=-1)
    p_win = p[..., :3 * ATTN_BLOCK].astype(v.dtype)
    p_ctx = p[..., 3 * ATTN_BLOCK:3 * ATTN_BLOCK + L].astype(v.dtype)
    o = jnp.einsum('bnkgqs,bnskd->bnqkgd', p_win, vw) + jnp.einsum('bnkgql,blkd->bnqkgd', p_ctx, v_ctx)
    return o.reshape(B, S, Q_W)


def context_attention(q, k, v, sink):
    B, L = q.shape[:2]
    qg = q.reshape(B, L, N_KV_HEADS, GQA_GROUP, HEAD_DIM) * (HEAD_DIM ** -0.5)
    s = jnp.einsum('blkgd,bmkd->bkglm', qg, k).astype(F32)
    sink_col = jnp.broadcast_to(sink.reshape(N_KV_HEADS, GQA_GROUP, 1, 1).astype(F32),
                                (B, N_KV_HEADS, GQA_GROUP, L, 1))
    p = jax.nn.softmax(jnp.concatenate([s, sink_col], axis=-1), axis=-1)[..., :L].astype(v.dtype)
    o = jnp.einsum('bkglm,bmkd->blkgd', p, v)
    return o.reshape(B, L, Q_W)


def conformer_conv(z, w_dw, b_dw, ln_g, ln_b, w_o):
    a, gl = jnp.split(z, 2, axis=-1)
    u = a * jax.nn.sigmoid(gl)
    u = lax.conv_general_dilated(u, w_dw[:, None, :], window_strides=(1,),
                                 padding=((CONV_PAD, CONV_PAD),),
                                 dimension_numbers=('NWC', 'WIO', 'NWC'),
                                 feature_group_count=CONV_CH) + b_dw
    u = jax.nn.silu(layernorm(u, ln_g, ln_b))
    return u @ w_o


def multiscale_pool(z, w_pool, pool_scale):
    B, N, C = z.shape
    cs = jnp.concatenate([jnp.zeros((B, 1, C), F32), jnp.cumsum(z.astype(F32), axis=1)], axis=1)
    t = jnp.arange(N)
    outs = []
    for gi, w in enumerate(POOL_WINDOWS):
        sl = slice(gi * POOL_GROUP, (gi + 1) * POOL_GROUP)
        lo = jnp.clip(t - w // 2, 0, N)
        hi = jnp.clip(t + w - w // 2, 0, N)
        mean = (cs[:, hi, sl] - cs[:, lo, sl]) / (hi - lo).astype(F32)[None, :, None]
        outs.append((mean.astype(z.dtype) - z[..., sl]) @ w_pool[gi])
    return jnp.concatenate(outs, axis=-1) * pool_scale


def merge_branches(attn, conv_in, pool_in, gate_in, lp):
    y_a = attn @ lp['w_attn_o']
    y_b = conformer_conv(conv_in, lp['conv_dw'], lp['conv_dw_b'], lp['conv_ln_g'], lp['conv_ln_b'], lp['w_conv_o'])
    y_c = multiscale_pool(pool_in, lp['w_pool'], lp['pool_scale']) @ lp['w_pool_o']
    g_a, g_b, g_c = jnp.split(jax.nn.sigmoid(gate_in), N_BRANCHES, axis=-1)
    return (g_a * y_a + g_b * y_b + g_c * y_c) @ lp['w_out']


def expert_choice_ffn(h, w_router, w_gate, w_up, w_down):
    B, N, _ = h.shape
    cap = (CAPACITY_FACTOR * N) // N_EXPERTS
    aff = jax.nn.softmax(jnp.einsum('bnd,de->bne', h, w_router).astype(F32), axis=-1)
    g, idx = lax.top_k(jnp.transpose(aff, (0, 2, 1)), cap)
    bidx = jnp.arange(B)[:, None, None]
    xe = h[bidx, idx]
    a = jnp.einsum('becd,edf->becf', xe, w_gate)
    u = jnp.einsum('becd,edf->becf', xe, w_up)
    ye = jnp.einsum('becf,efd->becd', jax.nn.silu(a) * u, w_down) * g[..., None].astype(h.dtype)
    return jnp.zeros_like(h).at[bidx, idx].add(ye)


def split_in(p):
    return jnp.split(p, IN_OFFSETS, axis=-1)


def setup_inputs(seed: int = 0) -> dict:
    key = jax.random.key(seed)
    ks = jax.random.split(key, 25)

    def nrm(k, shape, scale):
        return jax.random.normal(k, shape, jnp.float32) * scale

    return {
        'x': nrm(ks[0], (BATCH, SEQ, D_MODEL), 1.0),
        'c': nrm(ks[1], (BATCH, D_MODEL), 1.0),
        'ctx': nrm(ks[2], (BATCH, CTX_LEN, D_MODEL), 1.0),
        'c_ctx': nrm(ks[3], (D_MODEL,), 1.0),
        'norm1_g': 1.0 + nrm(ks[4], (DEPTH, D_MODEL), 0.02),
        'norm2_g': 1.0 + nrm(ks[5], (DEPTH, D_MODEL), 0.02),
        'w_mod': nrm(ks[6], (DEPTH, D_MODEL, 6 * D_MODEL), 0.02),
        'b_mod': nrm(ks[7], (DEPTH, 6 * D_MODEL), 0.02),
        'w_in': nrm(ks[8], (DEPTH, D_MODEL, IN_W), D_MODEL ** -0.5),
        'attn_sink': nrm(ks[9], (DEPTH, N_HEADS), 1.0),
        'w_attn_o': nrm(ks[10], (DEPTH, Q_W, D_MODEL), Q_W ** -0.5),
        'conv_dw': nrm(ks[11], (DEPTH, CONV_K, CONV_CH), CONV_K ** -0.5),
        'conv_dw_b': nrm(ks[12], (DEPTH, CONV_CH), 0.02),
        'conv_ln_g': 1.0 + nrm(ks[13], (DEPTH, CONV_CH), 0.02),
        'conv_ln_b': nrm(ks[14], (DEPTH, CONV_CH), 0.02),
        'w_conv_o': nrm(ks[15], (DEPTH, CONV_CH, D_MODEL), CONV_CH ** -0.5),
        'w_pool': nrm(ks[16], (DEPTH, len(POOL_WINDOWS), POOL_GROUP, POOL_GROUP), POOL_GROUP ** -0.5),
        'pool_scale': 1.0 + nrm(ks[17], (DEPTH, POOL_CH), 0.1),
        'w_pool_o': nrm(ks[18], (DEPTH, POOL_CH, D_MODEL), POOL_CH ** -0.5),
        'w_out': nrm(ks[19], (DEPTH, D_MODEL, D_MODEL), D_MODEL ** -0.5),
        'w_router': nrm(ks[20], (DEPTH, D_MODEL, N_EXPERTS), D_MODEL ** -0.5),
        'w_e_gate': nrm(ks[21], (DEPTH, N_EXPERTS, D_MODEL, D_EXPERT), D_MODEL ** -0.5),
        'w_e_up': nrm(ks[22], (DEPTH, N_EXPERTS, D_MODEL, D_EXPERT), D_MODEL ** -0.5),
        'w_e_down': nrm(ks[23], (DEPTH, N_EXPERTS, D_EXPERT, D_MODEL), D_EXPERT ** -0.5),
        'final_norm_g': 1.0 + nrm(ks[24], (D_MODEL,), 0.02),
    }


def reference(x, c, ctx, c_ctx, norm1_g, norm2_g, w_mod, b_mod, w_in, attn_sink, w_attn_o,
              conv_dw, conv_dw_b, conv_ln_g, conv_ln_b, w_conv_o, w_pool, pool_scale, w_pool_o,
              w_out, w_router, w_e_gate, w_e_up, w_e_down, final_norm_g):
    S = x.shape[1]
    tabs = rope_tables(S)
    for l in range(DEPTH):
        last = l == DEPTH - 1
        lp = {'w_attn_o': w_attn_o[l], 'conv_dw': conv_dw[l], 'conv_dw_b': conv_dw_b[l],
              'conv_ln_g': conv_ln_g[l], 'conv_ln_b': conv_ln_b[l], 'w_conv_o': w_conv_o[l],
              'w_pool': w_pool[l], 'pool_scale': pool_scale[l], 'w_pool_o': w_pool_o[l], 'w_out': w_out[l]}
        mod_x = (jax.nn.silu(c) @ w_mod[l] + b_mod[l])[:, None, :]
        sh1, sc1, g1, sh2, sc2, g2 = jnp.split(mod_x, 6, axis=-1)
        mod_c = jax.nn.silu(c_ctx) @ w_mod[l] + b_mod[l]
        csh1, csc1, cg1, csh2, csc2, cg2 = jnp.split(mod_c, 6, axis=-1)

        p_x = modulate(rmsnorm(x, norm1_g[l]), sh1, sc1) @ w_in[l]
        p_c = modulate(rmsnorm(ctx, norm1_g[l]), csh1, csc1) @ w_in[l]
        qx, kx, vx, conv_x, pool_x, gate_x = split_in(p_x)
        qc, kc, vc, conv_c, pool_c, gate_c = split_in(p_c)
        kc = heads(kc, N_KV_HEADS)
        vc = heads(vc, N_KV_HEADS)
        attn_x = window_attention(apply_axial_rope(heads(qx, N_HEADS), tabs),
                                  apply_axial_rope(heads(kx, N_KV_HEADS), tabs),
                                  heads(vx, N_KV_HEADS), kc, vc, attn_sink[l])
        x = x + g1 * merge_branches(attn_x, conv_x, pool_x, gate_x, lp)
        if not last:
            attn_c = context_attention(heads(qc, N_HEADS), kc, vc, attn_sink[l])
            ctx = ctx + cg1 * merge_branches(attn_c, conv_c, pool_c, gate_c, lp)

        x = x + g2 * expert_choice_ffn(modulate(rmsnorm(x, norm2_g[l]), sh2, sc2),
                                       w_router[l], w_e_gate[l], w_e_up[l], w_e_down[l])
        if not last:
            ctx = ctx + cg2 * expert_choice_ffn(modulate(rmsnorm(ctx, norm2_g[l]), csh2, csc2),
                                                w_router[l], w_e_gate[l], w_e_up[l], w_e_down[l])
    return rmsnorm(x, final_norm_g)
```

```python
import functools

import jax
import jax.numpy as jnp
import numpy as np
from jax import lax
from jax.experimental import pallas as pl
from jax.experimental.pallas import tpu as pltpu

F32 = jnp.float32
BF16 = jnp.bfloat16
I32 = jnp.int32

EPS = 1e-6
GRID_W = 64
N_HEADS = 8
N_KV_HEADS = 2
HEAD_DIM = 64
GQA_GROUP = N_HEADS // N_KV_HEADS
WINDOW = 128
ATTN_BLOCK = 128
ROPE_BASE = 10000.0
ROPE_PAIRS = HEAD_DIM // 4
CONV_CH = 512
CONV_K = 31
CONV_PAD = CONV_K // 2
POOL_WINDOWS = (2, 4, 8, 16)
POOL_GROUP = 128
POOL_CH = POOL_GROUP * len(POOL_WINDOWS)
N_EXPERTS = 16
CAPACITY_FACTOR = 2
Q_W = N_HEADS * HEAD_DIM
KV_W = N_KV_HEADS * HEAD_DIM

LANES = 128
HALO = 16
HIGHEST = lax.Precision.HIGHEST
F32_TINY = 2.0 ** -126
F32_MANTISSA_BITS = 23

_D_MODEL = 1024
COL_GATES = 0
COL_Q = 3 * _D_MODEL
COL_CONV_A = COL_Q + Q_W
COL_CONV_G = COL_CONV_A + CONV_CH
COL_POOL = COL_CONV_G + CONV_CH
COL_K = COL_POOL + POOL_CH
COL_V = COL_K + KV_W
IN_W = COL_V + KV_W


def _dot(a, b):
    return jnp.dot(a, b, preferred_element_type=F32)


def _dot_nt(a, b, precision=None):
    return lax.dot_general(a, b, (((1,), (1,)), ((), ())), preferred_element_type=F32, precision=precision)


def _sigmoid(v):
    return 1.0 / (1.0 + jnp.exp(-v))


def _rms_mod(x, g, sh, sc):
    y = x * lax.rsqrt(jnp.mean(x * x, axis=-1, keepdims=True) + EPS) * g
    return y * (1.0 + sc) + sh


def _mod_kernel(c_ref, w_ref, b_ref, o_ref):
    c = c_ref[...]
    a = c * _sigmoid(c)
    o_ref[0] = jnp.dot(a, w_ref[0], preferred_element_type=F32, precision=HIGHEST) + b_ref[0]


def _modulation(cc, w_mod, b_mod, tn=1536):
    depth, d, n = w_mod.shape
    rows = cc.shape[0]
    return pl.pallas_call(
        _mod_kernel,
        grid=(depth, n // tn),
        in_specs=[pl.BlockSpec((rows, d), lambda l, j: (0, 0)),
                  pl.BlockSpec((1, d, tn), lambda l, j: (l, 0, j)),
                  pl.BlockSpec((1, 1, tn), lambda l, j: (l, 0, j))],
        out_specs=pl.BlockSpec((1, rows, tn), lambda l, j: (l, 0, j)),
        out_shape=jax.ShapeDtypeStruct((depth, rows, n), F32),
        name="modulation",
    )(cc, w_mod, b_mod.reshape(depth, 1, n))


def _inproj_kernel(x_ref, g_ref, sh_ref, sc_ref, w_ref, o_ref, *, cn):
    h = _rms_mod(x_ref[0], g_ref[...], sh_ref[0], sc_ref[0]).astype(BF16)
    n = w_ref.shape[1]
    for j in range(n // cn):
        o_ref[0, :, j * cn:(j + 1) * cn] = _dot(h, w_ref[:, j * cn:(j + 1) * cn]).astype(o_ref.dtype)


def _inproj(x, g, sh, sc, w, tm):
    b, s, d = x.shape
    n = w.shape[1]
    cn = 256 if n % 768 else 768
    return pl.pallas_call(
        functools.partial(_inproj_kernel, cn=cn),
        grid=(b, s // tm),
        in_specs=[pl.BlockSpec((1, tm, d), lambda i, j: (i, j, 0)),
                  pl.BlockSpec((1, d), lambda i, j: (0, 0)),
                  pl.BlockSpec((1, 1, d), lambda i, j: (i, 0, 0)),
                  pl.BlockSpec((1, 1, d), lambda i, j: (i, 0, 0)),
                  pl.BlockSpec((d, n), lambda i, j: (0, 0))],
        out_specs=pl.BlockSpec((1, tm, n), lambda i, j: (i, j, 0)),
        out_shape=jax.ShapeDtypeStruct((b, s, n), BF16),
        name="inproj",
    )(x, g, sh, sc, w)


def _rope_tables(s):
    t = np.arange(s)
    row = (t // GRID_W).astype(np.float32)
    col = (t % GRID_W).astype(np.float32)
    freqs = jnp.asarray(ROPE_BASE, F32) ** (-jnp.arange(ROPE_PAIRS, dtype=F32) / ROPE_PAIRS)
    ang_r = jnp.asarray(row)[:, None] * freqs
    ang_c = jnp.asarray(col)[:, None] * freqs
    cos_h = jnp.concatenate([jnp.cos(ang_r), jnp.cos(ang_r), jnp.cos(ang_c), jnp.cos(ang_c)], axis=-1)
    sin_h = jnp.concatenate([-jnp.sin(ang_r), jnp.sin(ang_r), -jnp.sin(ang_c), jnp.sin(ang_c)], axis=-1)
    return jnp.tile(cos_h, (1, LANES // HEAD_DIM)), jnp.tile(sin_h, (1, LANES // HEAD_DIM))


def _rope(x, cos, sin_signed):
    lane = lax.broadcasted_iota(I32, x.shape, 1)
    low = (lane & (2 * ROPE_PAIRS - 1)) < ROPE_PAIRS
    partner = jnp.where(low, pltpu.roll(x, LANES - ROPE_PAIRS, 1), pltpu.roll(x, ROPE_PAIRS, 1))
    return x * cos + partner * sin_signed


def _softmax_pv(s_list, v_list, sink):
    m = sink
    for s in s_list:
        m = jnp.maximum(m, jnp.max(s, axis=-1, keepdims=True))
    denom = jnp.exp(sink - m)
    o = None
    for s, v in zip(s_list, v_list):
        e = jnp.exp(s - m)
        denom = denom + jnp.sum(e, axis=-1, keepdims=True)
        pv = _dot(e.astype(BF16), v)
        o = pv if o is None else o + pv
    return o / denom


def _lane_lo(shape):
    return lax.broadcasted_iota(I32, shape, 1) < HEAD_DIM


def _dup_heads(t):
    swapped = pltpu.roll(t, HEAD_DIM, 1)
    lo = _lane_lo(t.shape)
    return jnp.where(lo, t, swapped), jnp.where(lo, swapped, t)


def _pair_attention(qp, keys, vals, masks, sinks):
    lo = _lane_lo(qp.shape)
    outs = []
    for sub in range(2):
        qh = jnp.where(lo if sub == 0 else jnp.logical_not(lo), qp, jnp.zeros_like(qp))
        scores = []
        for k, mask in zip(keys, masks):
            sc = _dot_nt(qh, k)
            scores.append(sc if mask is None else jnp.where(mask, sc, -1e30))
        outs.append(_softmax_pv(scores, vals, sinks[sub]))
    return jnp.where(lo, outs[0], outs[1])


def _store_dup(dst_ref, t):
    d0, d1 = _dup_heads(t)
    dst_ref[0] = d0.astype(dst_ref.dtype)
    dst_ref[1] = d1.astype(dst_ref.dtype)


def _win_attn_kernel(sink_ref, q_ref, k_ref, v_ref, kc_ref, vc_ref, cosq_ref, sinq_ref, cosk_ref, sink_tab_ref,
                     o_ref, kd_ref, vd_ref, kcd_ref, vcd_ref, *, seq):
    i = pl.program_id(1)
    blk = ATTN_BLOCK
    win = 3 * blk

    @pl.when(i == 0)
    def _():
        _store_dup(kd_ref, _rope(k_ref[0].astype(F32), cosk_ref[...], sink_tab_ref[...]))
        _store_dup(vd_ref, v_ref[0].astype(F32))
        _store_dup(kcd_ref, kc_ref[0].astype(F32))
        _store_dup(vcd_ref, vc_ref[0].astype(F32))

    start = pl.multiple_of(jnp.clip((i - 1) * blk, 0, seq - win), blk)
    qpos = i * blk + lax.broadcasted_iota(I32, (blk, win), 0)
    kpos = start + lax.broadcasted_iota(I32, (blk, win), 1)
    mask = jnp.abs(kpos - qpos) <= WINDOW
    cos = cosq_ref[...]
    sin = sinq_ref[...]
    scale = HEAD_DIM ** -0.5
    for pair in range(N_HEADS // 2):
        kh = (2 * pair) // GQA_GROUP
        cols = slice(pair * LANES, (pair + 1) * LANES)
        qp = (_rope(q_ref[0, :, cols].astype(F32), cos, sin) * scale).astype(BF16)
        keys = [kd_ref[kh, pl.ds(start, win), :], kcd_ref[kh]]
        vals = [vd_ref[kh, pl.ds(start, win), :], vcd_ref[kh]]
        o = _pair_attention(qp, keys, vals, [mask, None], [sink_ref[2 * pair], sink_ref[2 * pair + 1]])
        o_ref[0, :, cols] = o.astype(o_ref.dtype)


def _window_attention(p_x, p_c, sink, tabs):
    b, s, _ = p_x.shape
    l = p_c.shape[1]
    cos, sin = tabs
    blk = ATTN_BLOCK
    kcol, vcol = COL_K // KV_W, COL_V // KV_W
    return pl.pallas_call(
        functools.partial(_win_attn_kernel, seq=s),
        grid=(b, s // blk),
        in_specs=[pl.BlockSpec(memory_space=pltpu.SMEM),
                  pl.BlockSpec((1, blk, Q_W), lambda i, j: (i, j, COL_Q // Q_W)),
                  pl.BlockSpec((1, s, KV_W), lambda i, j: (i, 0, kcol)),
                  pl.BlockSpec((1, s, KV_W), lambda i, j: (i, 0, vcol)),
                  pl.BlockSpec((1, l, KV_W), lambda i, j: (i, 0, kcol)),
                  pl.BlockSpec((1, l, KV_W), lambda i, j: (i, 0, vcol)),
                  pl.BlockSpec((blk, LANES), lambda i, j: (j, 0)),
                  pl.BlockSpec((blk, LANES), lambda i, j: (j, 0)),
                  pl.BlockSpec((s, LANES), lambda i, j: (0, 0)),
                  pl.BlockSpec((s, LANES), lambda i, j: (0, 0))],
        out_specs=pl.BlockSpec((1, blk, Q_W), lambda i, j: (i, j, 0)),
        out_shape=jax.ShapeDtypeStruct((b, s, Q_W), BF16),
        scratch_shapes=[pltpu.VMEM((N_KV_HEADS, s, LANES), BF16), pltpu.VMEM((N_KV_HEADS, s, LANES), BF16),
                        pltpu.VMEM((N_KV_HEADS, l, LANES), BF16), pltpu.VMEM((N_KV_HEADS, l, LANES), BF16)],
        name="window_attention",
    )(sink, p_x, p_x, p_x, p_c, p_c, cos, sin, cos, sin)


def _ctx_attn_kernel(sink_ref, q_ref, k_ref, v_ref, o_ref):
    kd = [t.astype(BF16) for t in _dup_heads(k_ref[0].astype(F32))]
    vd = [t.astype(BF16) for t in _dup_heads(v_ref[0].astype(F32))]
    scale = HEAD_DIM ** -0.5
    for pair in range(N_HEADS // 2):
        kh = (2 * pair) // GQA_GROUP
        cols = slice(pair * LANES, (pair + 1) * LANES)
        qp = (q_ref[0, :, cols].astype(F32) * scale).astype(BF16)
        o = _pair_attention(qp, [kd[kh]], [vd[kh]], [None], [sink_ref[2 * pair], sink_ref[2 * pair + 1]])
        o_ref[0, :, cols] = o.astype(o_ref.dtype)


def _context_attention(p_c, sink):
    b, l, _ = p_c.shape
    return pl.pallas_call(
        _ctx_attn_kernel,
        grid=(b,),
        in_specs=[pl.BlockSpec(memory_space=pltpu.SMEM),
                  pl.BlockSpec((1, l, Q_W), lambda i: (i, 0, COL_Q // Q_W)),
                  pl.BlockSpec((1, l, KV_W), lambda i: (i, 0, COL_K // KV_W)),
                  pl.BlockSpec((1, l, KV_W), lambda i: (i, 0, COL_V // KV_W))],
        out_specs=pl.BlockSpec((1, l, Q_W), lambda i: (i, 0, 0)),
        out_shape=jax.ShapeDtypeStruct((b, l, Q_W), BF16),
        name="context_attention",
    )(sink, p_c, p_c, p_c)


def _merge_kernel(ga_ref, gb_ref, gc_ref,
                  ca_ref, ca_p_ref, ca_n_ref, cg_ref, cg_p_ref, cg_n_ref, pz_ref, pz_p_ref, pz_n_ref,
                  attn_ref, x_ref, g1_ref, sh2_ref, sc2_ref, n2g_ref,
                  dw_ref, dwb_ref, lng_ref, lnb_ref, wpool_ref, pscale_ref,
                  wa_ref, wb_ref, wc_ref, wo_ref, wr_ref, wrt_ref,
                  xo_ref, h2_ref, aff_ref, afft_ref,
                  uwin_ref, zwin_ref, *, seq):
    t = pl.program_id(1)
    tt = x_ref.shape[1]
    has_prev = (t > 0).astype(F32)
    has_next = (t < pl.num_programs(1) - 1).astype(F32)

    def glu(a_ref, g_ref):
        return a_ref[0].astype(F32) * _sigmoid(g_ref[0].astype(F32))

    uwin_ref[0:HALO, :] = glu(ca_p_ref, cg_p_ref) * has_prev
    uwin_ref[HALO:HALO + tt, :] = glu(ca_ref, cg_ref)
    uwin_ref[HALO + tt:, :] = glu(ca_n_ref, cg_n_ref) * has_next
    acc = jnp.zeros((tt, CONV_CH), F32) + dwb_ref[...]
    for k in range(CONV_K):
        acc = acc + uwin_ref[pl.ds(HALO - CONV_PAD + k, tt), :] * dw_ref[k:k + 1, :]
    mu = jnp.mean(acc, axis=-1, keepdims=True)
    cen = acc - mu
    var = jnp.mean(cen * cen, axis=-1, keepdims=True)
    ln = cen * lax.rsqrt(var + EPS) * lng_ref[...] + lnb_ref[...]
    feat_b = (ln * _sigmoid(ln)).astype(BF16)

    zwin_ref[0:HALO, :] = pz_p_ref[0].astype(F32) * has_prev
    zwin_ref[HALO:HALO + tt, :] = pz_ref[0].astype(F32)
    zwin_ref[HALO + tt:, :] = pz_n_ref[0].astype(F32) * has_next
    tpos = t * tt + lax.broadcasted_iota(I32, (tt, 1), 0)
    pooled = []
    for gi, w in enumerate(POOL_WINDOWS):
        cols = slice(gi * POOL_GROUP, (gi + 1) * POOL_GROUP)
        tot = zwin_ref[pl.ds(HALO - w // 2, tt), cols]
        for d in range(1 - w // 2, w - w // 2):
            tot = tot + zwin_ref[pl.ds(HALO + d, tt), cols]
        cnt = (jnp.minimum(tpos + (w - w // 2), seq) - jnp.maximum(tpos - w // 2, 0)).astype(F32)
        diff = tot / cnt - zwin_ref[pl.ds(HALO, tt), cols]
        pooled.append(_dot(diff.astype(BF16), wpool_ref[gi]))
    feat_c = (jnp.concatenate(pooled, axis=-1) * pscale_ref[...]).astype(BF16)

    y_a = _dot(attn_ref[0], wa_ref[...])
    y_b = _dot(feat_b, wb_ref[...])
    y_c = _dot(feat_c, wc_ref[...])
    merged = (_sigmoid(ga_ref[0].astype(F32)) * y_a + _sigmoid(gb_ref[0].astype(F32)) * y_b
              + _sigmoid(gc_ref[0].astype(F32)) * y_c)
    xn = x_ref[0] + g1_ref[0] * _dot(merged.astype(BF16), wo_ref[...])
    xo_ref[0] = xn

    h2 = _rms_mod(xn, n2g_ref[...], sh2_ref[0], sc2_ref[0])
    h2_ref[0] = h2.astype(BF16)
    logits = jnp.dot(h2, wr_ref[...], preferred_element_type=F32, precision=HIGHEST)
    e = jnp.exp(logits - jnp.max(logits, axis=-1, keepdims=True))
    aff_ref[0] = e / jnp.sum(e, axis=-1, keepdims=True)
    logits_t = _dot_nt(wrt_ref[...], h2, precision=HIGHEST)
    et = jnp.exp(logits_t - jnp.max(logits_t, axis=0, keepdims=True))
    afft_ref[0] = et / jnp.sum(et, axis=0, keepdims=True)


def _merge(p, attn, x, g1, sh2, sc2, n2g, lw, tt):
    b, s, d = x.shape
    nh = tt // HALO
    last_h = s // HALO - 1
    e = N_EXPERTS

    def main(width, col):
        return pl.BlockSpec((1, tt, width), lambda i, j: (i, j, col))

    def prev(col):
        return pl.BlockSpec((1, HALO, CONV_CH), lambda i, j: (i, jnp.maximum(j * nh - 1, 0), col))

    def nxt(col):
        return pl.BlockSpec((1, HALO, CONV_CH), lambda i, j: (i, jnp.minimum((j + 1) * nh, last_h), col))

    def per_batch():
        return pl.BlockSpec((1, 1, d), lambda i, j: (i, 0, 0))

    def const(shape):
        return pl.BlockSpec(shape, lambda i, j: (0,) * len(shape))

    ca, cg, pz = COL_CONV_A // CONV_CH, COL_CONV_G // CONV_CH, COL_POOL // CONV_CH
    in_specs = [main(d, 0), main(d, 1), main(d, 2),
                main(CONV_CH, ca), prev(ca), nxt(ca), main(CONV_CH, cg), prev(cg), nxt(cg),
                main(POOL_CH, pz), prev(pz), nxt(pz),
                pl.BlockSpec((1, tt, Q_W), lambda i, j: (i, j, 0)),
                pl.BlockSpec((1, tt, d), lambda i, j: (i, j, 0)),
                per_batch(), per_batch(), per_batch(), const((1, d)),
                const((CONV_K, CONV_CH)), const((1, CONV_CH)), const((1, CONV_CH)), const((1, CONV_CH)),
                const((len(POOL_WINDOWS), POOL_GROUP, POOL_GROUP)), const((1, POOL_CH)),
                const((Q_W, d)), const((CONV_CH, d)), const((POOL_CH, d)), const((d, d)),
                const((d, e)), const((e, d))]
    out_specs = [pl.BlockSpec((1, tt, d), lambda i, j: (i, j, 0)),
                 pl.BlockSpec((1, tt, d), lambda i, j: (i, j, 0)),
                 pl.BlockSpec((1, tt, e), lambda i, j: (i, j, 0)),
                 pl.BlockSpec((1, e, tt), lambda i, j: (i, 0, j))]
    out_shape = [jax.ShapeDtypeStruct((b, s, d), F32), jax.ShapeDtypeStruct((b, s, d), BF16),
                 jax.ShapeDtypeStruct((b, s, e), F32), jax.ShapeDtypeStruct((b, e, s), F32)]
    return pl.pallas_call(
        functools.partial(_merge_kernel, seq=s),
        grid=(b, s // tt),
        in_specs=in_specs, out_specs=out_specs, out_shape=out_shape,
        scratch_shapes=[pltpu.VMEM((tt + 2 * HALO, CONV_CH), F32), pltpu.VMEM((tt + 2 * HALO, POOL_CH), F32)],
        name="mix_merge",
    )(p, p, p, p, p, p, p, p, p, p, p, p, attn, x, g1, sh2, sc2, n2g,
      lw['conv_dw'], lw['conv_dw_b'], lw['conv_ln_g'], lw['conv_ln_b'], lw['w_pool'], lw['pool_scale'],
      lw['w_attn_o'], lw['w_conv_o'], lw['w_pool_o'], lw['w_out'], lw['w_router'], lw['w_router_t'])


def _topk_kernel(afft_ref, slot_ref, slott_ref, *, cap, blk):
    a = afft_ref[0]
    e, s = a.shape

    def keeps_cap(cand):
        return jnp.sum((a >= cand).astype(F32), axis=-1, keepdims=True) >= cap

    tiny = jnp.full((e, 1), F32_TINY, F32)
    thr = jnp.where(keeps_cap(tiny), tiny, 0.0)
    for step in (64, 32, 16, 8, 4, 2, 1):
        cand = thr * float(2 ** step)
        thr = jnp.where(keeps_cap(cand), cand, thr)
    delta = thr
    for _ in range(F32_MANTISSA_BITS):
        delta = delta * 0.5
        cand = thr + delta
        thr = jnp.where(keeps_cap(cand), cand, thr)
    gt = a > thr
    eq = a == thr
    need = cap - jnp.sum(gt.astype(F32), axis=-1, keepdims=True)

    r = lax.broadcasted_iota(I32, (blk, blk), 0)
    c = lax.broadcasted_iota(I32, (blk, blk), 1)
    upper = (r < c).astype(BF16)
    eye = (r == c).astype(F32)

    def prefix(mask_f32):
        carry = jnp.zeros((e, 1), F32)
        parts = []
        for j in range(s // blk):
            m = mask_f32[:, j * blk:(j + 1) * blk]
            parts.append(_dot(m.astype(BF16), upper) + carry)
            carry = carry + jnp.sum(m, axis=-1, keepdims=True)
        return jnp.concatenate(parts, axis=-1)

    sel = gt | (eq & (prefix(eq.astype(F32)) < need))
    pos = prefix(sel.astype(F32))
    slot = jnp.where(sel, pos, -1.0)
    slot_ref[0] = slot.astype(I32)
    for j in range(s // blk):
        slott_ref[0, j * blk:(j + 1) * blk, :] = _dot_nt(eye, slot[:, j * blk:(j + 1) * blk],
                                                        precision=HIGHEST).astype(I32)


def _topk(afft, cap):
    b, e, s = afft.shape
    blk = min(s, 256)
    return pl.pallas_call(
        functools.partial(_topk_kernel, cap=cap, blk=blk),
        grid=(b,),
        in_specs=[pl.BlockSpec((1, e, s), lambda i: (i, 0, 0))],
        out_specs=[pl.BlockSpec((1, e, s), lambda i: (i, 0, 0)),
                   pl.BlockSpec((1, s, e), lambda i: (i, 0, 0))],
        out_shape=[jax.ShapeDtypeStruct((b, e, s), I32), jax.ShapeDtypeStruct((b, s, e), I32)],
        name="expert_choice_topk",
    )(afft)


def _ffn_kernel(slot_ref, h_ref, wg_ref, wu_ref, wd_ref, ye_ref, wg_s, wu_s, wd_s, xe_s, *, cap, chunk, rows):
    @pl.when(pl.program_id(1) == 0)
    def _():
        def cast(i, carry):
            sl = pl.ds(pl.multiple_of(i * rows, rows), rows)
            wg_s[sl, :] = wg_ref[0, sl, :].astype(BF16)
            wu_s[sl, :] = wu_ref[0, sl, :].astype(BF16)
            wd_s[sl, :] = wd_ref[0, sl, :].astype(BF16)
            return carry
        lax.fori_loop(0, wg_s.shape[0] // rows, cast, 0)

    s = h_ref.shape[1]
    xe_s[...] = jnp.zeros_like(xe_s)

    def gather(j, carry):
        sl = pl.ds(pl.multiple_of(j * chunk, chunk), chunk)
        slot = slot_ref[0, :, sl]
        onehot = (slot == lax.broadcasted_iota(I32, (cap, chunk), 0)).astype(BF16)
        xe_s[...] += _dot(onehot, h_ref[0, sl, :])
        return carry
    lax.fori_loop(0, s // chunk, gather, 0)

    xe = xe_s[...].astype(BF16)
    a = _dot(xe, wg_s[...])
    u = _dot(xe, wu_s[...])
    hid = (a * _sigmoid(a) * u).astype(BF16)
    ye_ref[0] = _dot(hid, wd_s[...]).astype(ye_ref.dtype)


def _expert_ffn(slot, h2, wg, wu, wd, cap):
    b, e, s = slot.shape
    d = h2.shape[2]
    f = wg.shape[2]
    assert f == d
    chunk = min(s, 512)
    ye = pl.pallas_call(
        functools.partial(_ffn_kernel, cap=cap, chunk=chunk, rows=128),
        grid=(e, b),
        in_specs=[pl.BlockSpec((1, 1, s), lambda j, i: (i * e + j, 0, 0)),
                  pl.BlockSpec((1, s, d), lambda j, i: (i, 0, 0)),
                  pl.BlockSpec((1, d, f), lambda j, i: (j, 0, 0)),
                  pl.BlockSpec((1, d, f), lambda j, i: (j, 0, 0)),
                  pl.BlockSpec((1, f, d), lambda j, i: (j, 0, 0))],
        out_specs=pl.BlockSpec((1, cap, d), lambda j, i: (i * e + j, 0, 0)),
        out_shape=jax.ShapeDtypeStruct((b * e, cap, d), BF16),
        scratch_shapes=[pltpu.VMEM((d, f), BF16), pltpu.VMEM((d, f), BF16), pltpu.VMEM((f, d), BF16),
                        pltpu.VMEM((cap, d), F32)],
        name="expert_ffn",
    )(slot.reshape(b * e, 1, s), h2, wg, wu, wd)
    return ye.reshape(b, e * cap, d)


def _combine_kernel(slott_ref, aff_ref, ye_ref, x_ref, g2_ref, fg_ref, o_ref, *, cap, final_norm):
    tt = x_ref.shape[1]
    slott = slott_ref[0]
    aff = aff_ref[0]
    lane = lax.broadcasted_iota(I32, (tt, cap), 1)
    acc = jnp.zeros(x_ref.shape[1:], F32)
    for ex in range(N_EXPERTS):
        onehot = (slott[:, ex:ex + 1] == lane).astype(BF16)
        acc = acc + aff[:, ex:ex + 1] * _dot(onehot, ye_ref[0, ex * cap:(ex + 1) * cap, :])
    out = x_ref[0] + g2_ref[0] * acc
    if final_norm:
        out = out * lax.rsqrt(jnp.mean(out * out, axis=-1, keepdims=True) + EPS) * fg_ref[...]
    o_ref[0] = out


def _combine(slott, aff, ye, x, g2, fg, cap, tt, final_norm):
    b, s, d = x.shape
    e = N_EXPERTS
    return pl.pallas_call(
        functools.partial(_combine_kernel, cap=cap, final_norm=final_norm),
        grid=(b, s // tt),
        in_specs=[pl.BlockSpec((1, tt, e), lambda i, j: (i, j, 0)),
                  pl.BlockSpec((1, tt, e), lambda i, j: (i, j, 0)),
                  pl.BlockSpec((1, e * cap, d), lambda i, j: (i, 0, 0)),
                  pl.BlockSpec((1, tt, d), lambda i, j: (i, j, 0)),
                  pl.BlockSpec((1, 1, d), lambda i, j: (i, 0, 0)),
                  pl.BlockSpec((1, d), lambda i, j: (0, 0))],
        out_specs=pl.BlockSpec((1, tt, d), lambda i, j: (i, j, 0)),
        out_shape=jax.ShapeDtypeStruct((b, s, d), F32),
        name="moe_combine",
    )(slott, aff, ye, x, g2, fg)


def _permute_in_cols(w):
    o_k = Q_W
    o_v = o_k + KV_W
    o_ca = o_v + KV_W
    o_cg = o_ca + CONV_CH
    o_p = o_cg + CONV_CH
    o_g = o_p + POOL_CH
    return jnp.concatenate([w[:, o_g:], w[:, :o_k], w[:, o_ca:o_cg], w[:, o_cg:o_p], w[:, o_p:o_g],
                            w[:, o_k:o_v], w[:, o_v:o_ca]], axis=1)


def _moe(x_mid, h2, aff, afft, g2, fg, wg, wu, wd, tt, final_norm):
    s = x_mid.shape[1]
    cap = (CAPACITY_FACTOR * s) // N_EXPERTS
    slot, slott = _topk(afft, cap)
    ye = _expert_ffn(slot, h2, wg, wu, wd, cap)
    return _combine(slott, aff, ye, x_mid, g2, fg, cap, tt, final_norm)


def kernel(x, c, ctx, c_ctx, norm1_g, norm2_g, w_mod, b_mod, w_in, attn_sink, w_attn_o, conv_dw, conv_dw_b,
           conv_ln_g, conv_ln_b, w_conv_o, w_pool, pool_scale, w_pool_o, w_out, w_router, w_e_gate, w_e_up,
           w_e_down, final_norm_g):
    b, s, d = x.shape
    l = ctx.shape[1]
    depth = w_in.shape[0]
    assert d == _D_MODEL and w_in.shape[2] == IN_W

    tabs = _rope_tables(s)
    cc = jnp.zeros((8, d), F32).at[:b].set(c).at[b].set(c_ctx)
    mod = _modulation(cc, w_mod, b_mod)
    fg = final_norm_g.reshape(1, d)

    for layer in range(depth):
        last = layer == depth - 1
        mx = mod[layer, :b].reshape(b, 1, 6, d)
        sh1, sc1, g1, sh2, sc2, g2 = [mx[:, :, i] for i in range(6)]
        mc = jnp.broadcast_to(mod[layer, b].reshape(1, 1, 6, d), (b, 1, 6, d))
        csh1, csc1, cg1, csh2, csc2, cg2 = [mc[:, :, i] for i in range(6)]
        n1g = norm1_g[layer].reshape(1, d)
        n2g = norm2_g[layer].reshape(1, d)
        w_in_l = _permute_in_cols(w_in[layer]).astype(BF16)
        lw = {'conv_dw': conv_dw[layer], 'conv_dw_b': conv_dw_b[layer].reshape(1, -1),
              'conv_ln_g': conv_ln_g[layer].reshape(1, -1), 'conv_ln_b': conv_ln_b[layer].reshape(1, -1),
              'w_pool': w_pool[layer].astype(BF16), 'pool_scale': pool_scale[layer].reshape(1, -1),
              'w_attn_o': w_attn_o[layer].astype(BF16), 'w_conv_o': w_conv_o[layer].astype(BF16),
              'w_pool_o': w_pool_o[layer].astype(BF16), 'w_out': w_out[layer].astype(BF16),
              'w_router': w_router[layer], 'w_router_t': w_router[layer].T}

        p_x = _inproj(x, n1g, sh1, sc1, w_in_l, tm=512)
        p_c = _inproj(ctx, n1g, csh1, csc1, w_in_l, tm=l)
        attn_x = _window_attention(p_x, p_c, attn_sink[layer], tabs)
        x_mid, h2, aff, afft = _merge(p_x, attn_x, x, g1, sh2, sc2, n2g, lw, tt=256)
        if not last:
            attn_c = _context_attention(p_c, attn_sink[layer])
            c_mid, ch2, caff, cafft = _merge(p_c, attn_c, ctx, cg1, csh2, csc2, n2g, lw, tt=l)
        x = _moe(x_mid, h2, aff, afft, g2, fg, w_e_gate[layer], w_e_up[layer], w_e_down[layer], 256, last)
        if not last:
            ctx = _moe(c_mid, ch2, caff, cafft, cg2, fg, w_e_gate[layer], w_e_up[layer], w_e_down[layer],
                       l, False)
    return x
```

```python
import functools

import jax
import jax.numpy as jnp
import numpy as np
from jax import lax
from jax.experimental import pallas as pl
from jax.experimental.pallas import tpu as pltpu

F32 = jnp.float32
BF16 = jnp.bfloat16
I32 = jnp.int32

EPS = 1e-6
GRID_W = 64
N_HEADS = 8
N_KV_HEADS = 2
HEAD_DIM = 64
GQA_GROUP = N_HEADS // N_KV_HEADS
WINDOW = 128
ATTN_BLOCK = 128
ROPE_BASE = 10000.0
ROPE_PAIRS = HEAD_DIM // 4
CONV_CH = 512
CONV_K = 31
CONV_PAD = CONV_K // 2
POOL_WINDOWS = (2, 4, 8, 16)
POOL_GROUP = 128
POOL_CH = POOL_GROUP * len(POOL_WINDOWS)
N_EXPERTS = 16
CAPACITY_FACTOR = 2
Q_W = N_HEADS * HEAD_DIM
KV_W = N_KV_HEADS * HEAD_DIM

LANES = 128
HALO = 16
HIGHEST = lax.Precision.HIGHEST
F32_TINY = 2.0 ** -126
F32_MANTISSA_BITS = 23
TOKEN_CHUNK = 256
GATHER_WINDOW = 128
SCATTER_WINDOW = 64
F32_SUBLANES = 8
BF16_SUBLANES = 16

_D_MODEL = 1024
COL_GATES = 0
COL_Q = 3 * _D_MODEL
COL_CONV_A = COL_Q + Q_W
COL_CONV_G = COL_CONV_A + CONV_CH
COL_POOL = COL_CONV_G + CONV_CH
COL_K = COL_POOL + POOL_CH
COL_V = COL_K + KV_W
IN_W = COL_V + KV_W


def _dot(a, b):
    return jnp.dot(a, b, preferred_element_type=F32)


def _dot_nt(a, b, precision=None):
    return lax.dot_general(a, b, (((1,), (1,)), ((), ())), preferred_element_type=F32, precision=precision)


def _sigmoid(v):
    return 0.5 * jnp.tanh(0.5 * v) + 0.5


def _rms_mod(x, g, sh, sc):
    y = x * lax.rsqrt(jnp.mean(x * x, axis=-1, keepdims=True) + EPS) * g
    return y * (1.0 + sc) + sh


def _mod_kernel(c_ref, w_ref, b_ref, o_ref):
    c = c_ref[...]
    a = c * _sigmoid(c)
    o_ref[0] = jnp.dot(a, w_ref[0], preferred_element_type=F32, precision=HIGHEST) + b_ref[0]


def _modulation(cc, w_mod, b_mod, tn=1536):
    depth, d, n = w_mod.shape
    rows = cc.shape[0]
    return pl.pallas_call(
        _mod_kernel,
        grid=(depth, n // tn),
        in_specs=[pl.BlockSpec((rows, d), lambda l, j: (0, 0)),
                  pl.BlockSpec((1, d, tn), lambda l, j: (l, 0, j)),
                  pl.BlockSpec((1, 1, tn), lambda l, j: (l, 0, j))],
        out_specs=pl.BlockSpec((1, rows, tn), lambda l, j: (l, 0, j)),
        out_shape=jax.ShapeDtypeStruct((depth, rows, n), F32),
        name="modulation",
    )(cc, w_mod, b_mod.reshape(depth, 1, n))


def _inproj_kernel(x_ref, g_ref, sh_ref, sc_ref, w_ref, o_ref, *, cn):
    h = _rms_mod(x_ref[0], g_ref[...], sh_ref[0], sc_ref[0]).astype(BF16)
    n = w_ref.shape[1]
    for j in range(n // cn):
        o_ref[0, :, j * cn:(j + 1) * cn] = _dot(h, w_ref[:, j * cn:(j + 1) * cn]).astype(o_ref.dtype)


def _inproj(x, g, sh, sc, w, tm):
    b, s, d = x.shape
    n = w.shape[1]
    cn = 256 if n % 768 else 768
    return pl.pallas_call(
        functools.partial(_inproj_kernel, cn=cn),
        grid=(b, s // tm),
        in_specs=[pl.BlockSpec((1, tm, d), lambda i, j: (i, j, 0)),
                  pl.BlockSpec((1, d), lambda i, j: (0, 0)),
                  pl.BlockSpec((1, 1, d), lambda i, j: (i, 0, 0)),
                  pl.BlockSpec((1, 1, d), lambda i, j: (i, 0, 0)),
                  pl.BlockSpec((d, n), lambda i, j: (0, 0))],
        out_specs=pl.BlockSpec((1, tm, n), lambda i, j: (i, j, 0)),
        out_shape=jax.ShapeDtypeStruct((b, s, n), BF16),
        name="inproj",
    )(x, g, sh, sc, w)


def _rope_tables(s):
    t = np.arange(s)
    row = (t // GRID_W).astype(np.float32)
    col = (t % GRID_W).astype(np.float32)
    freqs = jnp.asarray(ROPE_BASE, F32) ** (-jnp.arange(ROPE_PAIRS, dtype=F32) / ROPE_PAIRS)
    ang_r = jnp.asarray(row)[:, None] * freqs
    ang_c = jnp.asarray(col)[:, None] * freqs
    cos_h = jnp.concatenate([jnp.cos(ang_r), jnp.cos(ang_r), jnp.cos(ang_c), jnp.cos(ang_c)], axis=-1)
    sin_h = jnp.concatenate([-jnp.sin(ang_r), jnp.sin(ang_r), -jnp.sin(ang_c), jnp.sin(ang_c)], axis=-1)
    return jnp.tile(cos_h, (1, LANES // HEAD_DIM)), jnp.tile(sin_h, (1, LANES // HEAD_DIM))


def _rope(x, cos, sin_signed):
    lane = lax.broadcasted_iota(I32, x.shape, 1)
    low = (lane & (2 * ROPE_PAIRS - 1)) < ROPE_PAIRS
    partner = jnp.where(low, pltpu.roll(x, LANES - ROPE_PAIRS, 1), pltpu.roll(x, ROPE_PAIRS, 1))
    return x * cos + partner * sin_signed


def _softmax_pv(s_list, v_list, sink):
    m = sink
    for s in s_list:
        m = jnp.maximum(m, jnp.max(s, axis=-1, keepdims=True))
    denom = jnp.exp(sink - m)
    o = None
    for s, v in zip(s_list, v_list):
        e = jnp.exp(s - m)
        denom = denom + jnp.sum(e, axis=-1, keepdims=True)
        pv = _dot(e.astype(BF16), v)
        o = pv if o is None else o + pv
    return o / denom


def _lane_lo(shape):
    return lax.broadcasted_iota(I32, shape, 1) < HEAD_DIM


def _dup_heads(t):
    swapped = pltpu.roll(t, HEAD_DIM, 1)
    lo = _lane_lo(t.shape)
    return jnp.where(lo, t, swapped), jnp.where(lo, swapped, t)


def _group_attention(qps, keys, vals, masks, sinks):
    lo = _lane_lo(qps[0].shape)
    results = []
    for i, qp in enumerate(qps):
        outs = []
        for sub, keep in enumerate((lo, jnp.logical_not(lo))):
            qh = jnp.where(keep, qp, jnp.zeros_like(qp))
            scores = []
            for k, mask in zip(keys, masks):
                sc = _dot_nt(qh, k)
                scores.append(sc if mask is None else jnp.where(mask, sc, -1e30))
            outs.append(_softmax_pv(scores, vals, sinks[2 * i + sub]))
        results.append(jnp.where(lo, outs[0], outs[1]))
    return results


def _store_dup(dst_ref, t):
    d0, d1 = _dup_heads(t)
    dst_ref[0] = d0.astype(dst_ref.dtype)
    dst_ref[1] = d1.astype(dst_ref.dtype)


def _win_attn_kernel(sink_ref, q_ref, k_ref, v_ref, kc_ref, vc_ref, cosq_ref, sinq_ref, cosk_ref, sink_tab_ref,
                     o_ref, kd_ref, vd_ref, kcd_ref, vcd_ref, *, seq):
    i = pl.program_id(1)
    blk = ATTN_BLOCK
    win = 3 * blk

    @pl.when(i == 0)
    def _():
        _store_dup(kd_ref, _rope(k_ref[0].astype(F32), cosk_ref[...], sink_tab_ref[...]))
        _store_dup(vd_ref, v_ref[0].astype(F32))
        _store_dup(kcd_ref, kc_ref[0].astype(F32))
        _store_dup(vcd_ref, vc_ref[0].astype(F32))

    start = pl.multiple_of(jnp.clip((i - 1) * blk, 0, seq - win), blk)
    qpos = i * blk + lax.broadcasted_iota(I32, (blk, win), 0)
    kpos = start + lax.broadcasted_iota(I32, (blk, win), 1)
    mask = jnp.abs(kpos - qpos) <= WINDOW
    cos = cosq_ref[...]
    sin = sinq_ref[...]
    scale = HEAD_DIM ** -0.5
    pairs_per_kv = GQA_GROUP // 2
    for kh in range(N_KV_HEADS):
        pairs = range(kh * pairs_per_kv, (kh + 1) * pairs_per_kv)
        qps = [(_rope(q_ref[0, :, p * LANES:(p + 1) * LANES].astype(F32), cos, sin) * scale).astype(BF16)
               for p in pairs]
        keys = [kd_ref[kh, pl.ds(start, win), :], kcd_ref[kh]]
        vals = [vd_ref[kh, pl.ds(start, win), :], vcd_ref[kh]]
        sinks = [sink_ref[h] for h in range(kh * GQA_GROUP, (kh + 1) * GQA_GROUP)]
        for p, o in zip(pairs, _group_attention(qps, keys, vals, [mask, None], sinks)):
            o_ref[0, :, p * LANES:(p + 1) * LANES] = o.astype(o_ref.dtype)


def _window_attention(p_x, p_c, sink, tabs):
    b, s, _ = p_x.shape
    l = p_c.shape[1]
    cos, sin = tabs
    blk = ATTN_BLOCK
    kcol, vcol = COL_K // KV_W, COL_V // KV_W
    return pl.pallas_call(
        functools.partial(_win_attn_kernel, seq=s),
        grid=(b, s // blk),
        in_specs=[pl.BlockSpec(memory_space=pltpu.SMEM),
                  pl.BlockSpec((1, blk, Q_W), lambda i, j: (i, j, COL_Q // Q_W)),
                  pl.BlockSpec((1, s, KV_W), lambda i, j: (i, 0, kcol)),
                  pl.BlockSpec((1, s, KV_W), lambda i, j: (i, 0, vcol)),
                  pl.BlockSpec((1, l, KV_W), lambda i, j: (i, 0, kcol)),
                  pl.BlockSpec((1, l, KV_W), lambda i, j: (i, 0, vcol)),
                  pl.BlockSpec((blk, LANES), lambda i, j: (j, 0)),
                  pl.BlockSpec((blk, LANES), lambda i, j: (j, 0)),
                  pl.BlockSpec((s, LANES), lambda i, j: (0, 0)),
                  pl.BlockSpec((s, LANES), lambda i, j: (0, 0))],
        out_specs=pl.BlockSpec((1, blk, Q_W), lambda i, j: (i, j, 0)),
        out_shape=jax.ShapeDtypeStruct((b, s, Q_W), BF16),
        scratch_shapes=[pltpu.VMEM((N_KV_HEADS, s, LANES), BF16), pltpu.VMEM((N_KV_HEADS, s, LANES), BF16),
                        pltpu.VMEM((N_KV_HEADS, l, LANES), BF16), pltpu.VMEM((N_KV_HEADS, l, LANES), BF16)],
        name="window_attention",
    )(sink, p_x, p_x, p_x, p_c, p_c, cos, sin, cos, sin)


def _ctx_attn_kernel(sink_ref, q_ref, k_ref, v_ref, o_ref):
    kd = [t.astype(BF16) for t in _dup_heads(k_ref[0].astype(F32))]
    vd = [t.astype(BF16) for t in _dup_heads(v_ref[0].astype(F32))]
    scale = HEAD_DIM ** -0.5
    pairs_per_kv = GQA_GROUP // 2
    for kh in range(N_KV_HEADS):
        pairs = range(kh * pairs_per_kv, (kh + 1) * pairs_per_kv)
        qps = [(q_ref[0, :, p * LANES:(p + 1) * LANES].astype(F32) * scale).astype(BF16) for p in pairs]
        sinks = [sink_ref[h] for h in range(kh * GQA_GROUP, (kh + 1) * GQA_GROUP)]
        for p, o in zip(pairs, _group_attention(qps, [kd[kh]], [vd[kh]], [None], sinks)):
            o_ref[0, :, p * LANES:(p + 1) * LANES] = o.astype(o_ref.dtype)


def _context_attention(p_c, sink):
    b, l, _ = p_c.shape
    return pl.pallas_call(
        _ctx_attn_kernel,
        grid=(b,),
        in_specs=[pl.BlockSpec(memory_space=pltpu.SMEM),
                  pl.BlockSpec((1, l, Q_W), lambda i: (i, 0, COL_Q // Q_W)),
                  pl.BlockSpec((1, l, KV_W), lambda i: (i, 0, COL_K // KV_W)),
                  pl.BlockSpec((1, l, KV_W), lambda i: (i, 0, COL_V // KV_W))],
        out_specs=pl.BlockSpec((1, l, Q_W), lambda i: (i, 0, 0)),
        out_shape=jax.ShapeDtypeStruct((b, l, Q_W), BF16),
        name="context_attention",
    )(sink, p_c, p_c, p_c)


def _merge_kernel(ga_ref, gb_ref, gc_ref,
                  ca_ref, ca_p_ref, ca_n_ref, cg_ref, cg_p_ref, cg_n_ref, pz_ref, pz_p_ref, pz_n_ref,
                  attn_ref, x_ref, g1_ref, sh2_ref, sc2_ref, n2g_ref,
                  dw_ref, dwb_ref, lng_ref, lnb_ref, wpool_ref, pscale_ref,
                  wa_ref, wb_ref, wc_ref, wo_ref, wrt_ref,
                  xo_ref, h2_ref, afft_ref,
                  uwin_ref, zwin_ref, *, seq):
    t = pl.program_id(1)
    tt = x_ref.shape[1]
    has_prev = (t > 0).astype(F32)
    has_next = (t < pl.num_programs(1) - 1).astype(F32)

    def glu(a_ref, g_ref):
        return a_ref[0].astype(F32) * _sigmoid(g_ref[0].astype(F32))

    uwin_ref[0:HALO, :] = glu(ca_p_ref, cg_p_ref) * has_prev
    uwin_ref[HALO:HALO + tt, :] = glu(ca_ref, cg_ref)
    uwin_ref[HALO + tt:, :] = glu(ca_n_ref, cg_n_ref) * has_next
    first = HALO - CONV_PAD
    span = tt + F32_SUBLANES * ((first + CONV_K - 1) // F32_SUBLANES)
    acc_cols = []
    for cb in range(CONV_CH // LANES):
        cols = slice(cb * LANES, (cb + 1) * LANES)
        acc_c = jnp.zeros((tt, LANES), F32) + dwb_ref[:, cols]
        for shift in range(F32_SUBLANES):
            taps = [k for k in range(CONV_K) if (first + k) % F32_SUBLANES == shift]
            if not taps:
                continue
            shifted = uwin_ref[pl.ds(shift, span), cols]
            for k in taps:
                off = first + k - shift
                acc_c = acc_c + shifted[off:off + tt] * dw_ref[k:k + 1, cols]
        acc_cols.append(acc_c)
    acc = jnp.concatenate(acc_cols, axis=-1)
    mu = jnp.mean(acc, axis=-1, keepdims=True)
    cen = acc - mu
    var = jnp.mean(cen * cen, axis=-1, keepdims=True)
    ln = cen * lax.rsqrt(var + EPS) * lng_ref[...] + lnb_ref[...]
    feat_b = (ln * _sigmoid(ln)).astype(BF16)

    zwin_ref[0:HALO, :] = pz_p_ref[0].astype(F32) * has_prev
    zwin_ref[HALO:HALO + tt, :] = pz_ref[0].astype(F32)
    zwin_ref[HALO + tt:, :] = pz_n_ref[0].astype(F32) * has_next
    tpos = t * tt + lax.broadcasted_iota(I32, (tt, 1), 0)
    pooled = []
    for gi, w in enumerate(POOL_WINDOWS):
        cols = slice(gi * POOL_GROUP, (gi + 1) * POOL_GROUP)
        tot = zwin_ref[pl.ds(HALO - w // 2, tt), cols]
        for d in range(1 - w // 2, w - w // 2):
            tot = tot + zwin_ref[pl.ds(HALO + d, tt), cols]
        cnt = (jnp.minimum(tpos + (w - w // 2), seq) - jnp.maximum(tpos - w // 2, 0)).astype(F32)
        diff = tot / cnt - zwin_ref[pl.ds(HALO, tt), cols]
        pooled.append(_dot(diff.astype(BF16), wpool_ref[gi]))
    feat_c = (jnp.concatenate(pooled, axis=-1) * pscale_ref[...]).astype(BF16)

    y_a = _dot(attn_ref[0], wa_ref[...])
    y_b = _dot(feat_b, wb_ref[...])
    y_c = _dot(feat_c, wc_ref[...])
    merged = (_sigmoid(ga_ref[0].astype(F32)) * y_a + _sigmoid(gb_ref[0].astype(F32)) * y_b
              + _sigmoid(gc_ref[0].astype(F32)) * y_c)
    xn = x_ref[0] + g1_ref[0] * _dot(merged.astype(BF16), wo_ref[...])
    xo_ref[0] = xn

    h2 = _rms_mod(xn, n2g_ref[...], sh2_ref[0], sc2_ref[0])
    h2_hi = h2.astype(BF16)
    h2_ref[0] = h2_hi
    h2_lo = (h2 - h2_hi.astype(F32)).astype(BF16)
    ne = afft_ref.shape[1]
    by_hi = _dot_nt(wrt_ref[...], h2_hi)
    logits_t = by_hi[:ne] + by_hi[ne:] + _dot_nt(wrt_ref[:ne, :], h2_lo)
    et = jnp.exp(logits_t - jnp.max(logits_t, axis=0, keepdims=True))
    afft_ref[0] = et / jnp.sum(et, axis=0, keepdims=True)


def _merge(p, attn, x, g1, sh2, sc2, n2g, lw, tt):
    b, s, d = x.shape
    nh = tt // HALO
    last_h = s // HALO - 1
    e = N_EXPERTS

    def main(width, col):
        return pl.BlockSpec((1, tt, width), lambda i, j: (i, j, col))

    def prev(col):
        return pl.BlockSpec((1, HALO, CONV_CH), lambda i, j: (i, jnp.maximum(j * nh - 1, 0), col))

    def nxt(col):
        return pl.BlockSpec((1, HALO, CONV_CH), lambda i, j: (i, jnp.minimum((j + 1) * nh, last_h), col))

    def per_batch():
        return pl.BlockSpec((1, 1, d), lambda i, j: (i, 0, 0))

    def const(shape):
        return pl.BlockSpec(shape, lambda i, j: (0,) * len(shape))

    ca, cg, pz = COL_CONV_A // CONV_CH, COL_CONV_G // CONV_CH, COL_POOL // CONV_CH
    in_specs = [main(d, 0), main(d, 1), main(d, 2),
                main(CONV_CH, ca), prev(ca), nxt(ca), main(CONV_CH, cg), prev(cg), nxt(cg),
                main(POOL_CH, pz), prev(pz), nxt(pz),
                pl.BlockSpec((1, tt, Q_W), lambda i, j: (i, j, 0)),
                pl.BlockSpec((1, tt, d), lambda i, j: (i, j, 0)),
                per_batch(), per_batch(), per_batch(), const((1, d)),
                const((CONV_K, CONV_CH)), const((1, CONV_CH)), const((1, CONV_CH)), const((1, CONV_CH)),
                const((len(POOL_WINDOWS), POOL_GROUP, POOL_GROUP)), const((1, POOL_CH)),
                const((Q_W, d)), const((CONV_CH, d)), const((POOL_CH, d)), const((d, d)),
                const((2 * e, d))]
    out_specs = [pl.BlockSpec((1, tt, d), lambda i, j: (i, j, 0)),
                 pl.BlockSpec((1, tt, d), lambda i, j: (i, j, 0)),
                 pl.BlockSpec((1, e, tt), lambda i, j: (i, 0, j))]
    out_shape = [jax.ShapeDtypeStruct((b, s, d), F32), jax.ShapeDtypeStruct((b, s, d), BF16),
                 jax.ShapeDtypeStruct((b, e, s), F32)]
    return pl.pallas_call(
        functools.partial(_merge_kernel, seq=s),
        grid=(b, s // tt),
        in_specs=in_specs, out_specs=out_specs, out_shape=out_shape,
        scratch_shapes=[pltpu.VMEM((tt + 2 * HALO, CONV_CH), F32), pltpu.VMEM((tt + 2 * HALO, POOL_CH), F32)],
        name="mix_merge",
    )(p, p, p, p, p, p, p, p, p, p, p, p, attn, x, g1, sh2, sc2, n2g,
      lw['conv_dw'], lw['conv_dw_b'], lw['conv_ln_g'], lw['conv_ln_b'], lw['w_pool'], lw['pool_scale'],
      lw['w_attn_o'], lw['w_conv_o'], lw['w_pool_o'], lw['w_out'], lw['w_router_t'])


def _topk_kernel(afft_ref, slot_ref, slott_ref, offs_ref, *, cap, blk):
    a = afft_ref[0]
    e, s = a.shape
    def keeps_cap(cand):
        return jnp.sum((a >= cand).astype(F32), axis=-1, keepdims=True) >= cap

    tiny = jnp.full((e, 1), F32_TINY, F32)
    thr = jnp.where(keeps_cap(tiny), tiny, 0.0)
    for step in (64, 32, 16, 8, 4, 2, 1):
        cand = thr * float(2 ** step)
        thr = jnp.where(keeps_cap(cand), cand, thr)
    delta = thr
    for _ in range(F32_MANTISSA_BITS):
        delta = delta * 0.5
        cand = thr + delta
        thr = jnp.where(keeps_cap(cand), cand, thr)
    gt = a > thr
    eq = a == thr
    need = cap - jnp.sum(gt.astype(F32), axis=-1, keepdims=True)

    r = lax.broadcasted_iota(I32, (blk, blk), 0)
    c = lax.broadcasted_iota(I32, (blk, blk), 1)
    upper = (r < c).astype(BF16)
    eye = (r == c).astype(F32)

    def prefix(mask_f32):
        carry = jnp.zeros((e, 1), F32)
        parts = []
        for j in range(s // blk):
            m = mask_f32[:, j * blk:(j + 1) * blk]
            parts.append(_dot(m.astype(BF16), upper) + carry)
            carry = carry + jnp.sum(m, axis=-1, keepdims=True)
        return jnp.concatenate(parts, axis=-1)

    sel = gt | (eq & (prefix(eq.astype(F32)) < need))
    pos = prefix(sel.astype(F32))
    slot = jnp.where(sel, pos, -1.0)
    slot_ref[0] = slot.astype(I32)
    for j in range(s // blk):
        slott_ref[0, j * blk:(j + 1) * blk, :] = _dot_nt(eye, slot[:, j * blk:(j + 1) * blk],
                                                        precision=HIGHEST).astype(I32)
    tok = lax.broadcasted_iota(I32, (s, LANES), 0)
    col = lax.broadcasted_iota(I32, (s, LANES), 1)
    before = (tok < col * blk).astype(BF16)
    offs_ref[0] = _dot(sel.astype(BF16), before).astype(I32)


def _topk(afft, cap):
    b, e, s = afft.shape
    blk = min(s, TOKEN_CHUNK)
    slot, slott, offs = pl.pallas_call(
        functools.partial(_topk_kernel, cap=cap, blk=blk),
        grid=(b,),
        in_specs=[pl.BlockSpec((1, e, s), lambda i: (i, 0, 0))],
        out_specs=[pl.BlockSpec((1, e, s), lambda i: (i, 0, 0)),
                   pl.BlockSpec((1, s, e), lambda i: (i, 0, 0)),
                   pl.BlockSpec((1, e, LANES), lambda i: (i, 0, 0))],
        out_shape=[jax.ShapeDtypeStruct((b, e, s), I32), jax.ShapeDtypeStruct((b, s, e), I32),
                   jax.ShapeDtypeStruct((b, e, LANES), I32)],
        name="expert_choice_topk",
    )(afft)
    return slot, slott, offs[:, :, :s // blk + 1]


def _slot_windows(offs_ref, idx, win):
    lo = offs_ref[idx]
    hi = offs_ref[idx + 1]
    first = lo // win
    return first, jnp.where(hi > lo, (hi - 1) // win - first + 1, 0)


def _gather_rows(onehot, h, gate_row):
    picked = _dot(onehot.astype(BF16), h)
    gates = jnp.sum(jnp.where(onehot, gate_row, 0.0), axis=-1, keepdims=True)
    return picked, gates


def _ffn_kernel(offs_ref, slot_ref, afft_ref, h_ref, slotc_ref, cafft_ref, hc_ref, wg_ref, wu_ref, wd_ref,
                ye_ref, yec_ref, wg_s, wu_s, wd_s, xe_s, g_s, *, chunk, group, win, rows):
    ex = pl.program_id(0)
    bi = pl.program_id(1)
    nb = pl.num_programs(1)
    cap = xe_s.shape[0]

    @pl.when(bi == 0)
    def _():
        def cast(i, carry):
            sl = pl.ds(pl.multiple_of(i * rows, rows), rows)
            wg_s[sl, :] = wg_ref[0, 0, sl, :].astype(BF16)
            wu_s[sl, :] = wu_ref[0, 0, sl, :].astype(BF16)
            wd_s[sl, :] = wd_ref[0, 0, sl, :].astype(BF16)
            return carry
        lax.fori_loop(0, wg_s.shape[0] // rows, cast, 0)

    def ffn(xe):
        a = _dot(xe, wg_s[...])
        u = _dot(xe, wu_s[...])
        hid = (a * _sigmoid(a) * u).astype(BF16)
        return _dot(hid, wd_s[...])

    nch = h_ref.shape[1] // chunk
    base = (bi * pl.num_programs(0) + ex) * (nch + 1)
    xe_s[...] = jnp.zeros_like(xe_s)
    g_s[...] = jnp.zeros_like(g_s)
    span = group * chunk
    starts = []
    fits = None
    for p in range(nch // group):
        lo = offs_ref[base + p * group]
        hi = offs_ref[base + (p + 1) * group]
        a = pl.multiple_of(jnp.minimum((lo // F32_SUBLANES) * F32_SUBLANES, cap - win), F32_SUBLANES)
        starts.append(a)
        fits = (hi - a <= win) if fits is None else jnp.logical_and(fits, hi - a <= win)

    def add_window(a, sl, width):
        onehot = (slot_ref[0, :, sl] - a) == lax.broadcasted_iota(I32, (win, width), 0)
        picked, gates = _gather_rows(onehot, h_ref[0, sl, :], afft_ref[0, :, sl])
        xe_s[pl.ds(a, win), :] += picked
        g_s[pl.ds(a, win), :] += gates

    @pl.when(fits)
    def _():
        for p, a in enumerate(starts):
            add_window(a, slice(p * span, (p + 1) * span), span)

    @pl.when(jnp.logical_not(fits))
    def _():
        def per_chunk(j, carry):
            first, nwin = _slot_windows(offs_ref, base + j, win)
            sl = pl.ds(pl.multiple_of(j * chunk, chunk), chunk)

            def window(w, c):
                add_window(pl.multiple_of((first + w) * win, win), sl, chunk)
                return c
            lax.fori_loop(0, nwin, window, 0)
            return carry
        lax.fori_loop(0, nch, per_chunk, 0)

    ye_ref[0] = (ffn(xe_s[...].astype(BF16)) * g_s[...]).astype(ye_ref.dtype)

    if yec_ref is not None:
        @pl.when(bi == nb - 1)
        def _():
            rowc = lax.broadcasted_iota(I32, (yec_ref.shape[1], hc_ref.shape[0]), 0)
            picked, gates = _gather_rows(slotc_ref[0] == rowc, hc_ref[...], cafft_ref[0])
            yec_ref[0] = (ffn(picked.astype(BF16)) * gates).astype(yec_ref.dtype)


def _expert_ffn(layer, slot, offs, afft, h2, wg, wu, wd, cap, ctx_part=None):
    b, e, s = slot.shape
    d = h2.shape[2]
    f = wg.shape[3]
    assert f == d
    chunk = min(s, TOKEN_CHUNK)
    nch = s // chunk
    group = 2 if nch % 2 == 0 else 1
    win = min(cap, GATHER_WINDOW)
    assert cap % win == 0
    w_spec = pl.BlockSpec((1, 1, d, f), lambda j, i, o: (layer, j, 0, 0))
    row_spec = pl.BlockSpec((1, 1, s), lambda j, i, o: (i * e + j, 0, 0))
    in_specs = [row_spec, row_spec, pl.BlockSpec((1, s, d), lambda j, i, o: (i, 0, 0))]
    out_specs = [pl.BlockSpec((1, cap, d), lambda j, i, o: (i * e + j, 0, 0))]
    out_shape = [jax.ShapeDtypeStruct((b * e, cap, d), BF16)]
    args = [slot.reshape(b * e, 1, s), afft.reshape(b * e, 1, s), h2]
    body = functools.partial(_ffn_kernel, chunk=chunk, group=group, win=win, rows=128)
    if ctx_part is None:
        def kern(offs_ref, slot_ref, afft_ref, h_ref, wg_ref, wu_ref, wd_ref, ye_ref, *scratch):
            body(offs_ref, slot_ref, afft_ref, h_ref, None, None, None, wg_ref, wu_ref, wd_ref, ye_ref, None,
                 *scratch)
    else:
        slot_c, afft_c, h_c, rows_c = ctx_part
        n_c = h_c.shape[0]
        rowc_spec = pl.BlockSpec((1, 1, n_c), lambda j, i, o: (j, 0, 0))
        in_specs += [rowc_spec, rowc_spec, pl.BlockSpec((n_c, d), lambda j, i, o: (0, 0))]
        out_specs.append(pl.BlockSpec((1, rows_c, d), lambda j, i, o: (j, 0, 0)))
        out_shape.append(jax.ShapeDtypeStruct((e, rows_c, d), BF16))
        args += [slot_c, afft_c, h_c]
        kern = body
    outs = pl.pallas_call(
        kern,
        grid_spec=pltpu.PrefetchScalarGridSpec(
            num_scalar_prefetch=1, grid=(e, b),
            in_specs=in_specs + [w_spec, w_spec, w_spec], out_specs=out_specs,
            scratch_shapes=[pltpu.VMEM((d, f), BF16), pltpu.VMEM((d, f), BF16), pltpu.VMEM((f, d), BF16),
                            pltpu.VMEM((cap, d), F32), pltpu.VMEM((cap, 1), F32)]),
        out_shape=out_shape,
        name="expert_ffn",
    )(offs.reshape(-1), *args, wg, wu, wd)
    ye = outs[0].reshape(b, e * cap, d)
    return ye if ctx_part is None else (ye, outs[1])


def _combine_kernel(offs_ref, slott_ref, ye_ref, x_ref, g2_ref, fg_ref, o_ref, acc_s, *, cap, win, final_norm):
    bi = pl.program_id(0)
    j = pl.program_id(1)
    nch = pl.num_programs(1)
    tt = x_ref.shape[1]
    per_block = LANES // win

    starts = []
    fits = None
    for ex in range(N_EXPERTS):
        idx = (bi * N_EXPERTS + ex) * (nch + 1) + j
        a = pl.multiple_of(jnp.minimum((offs_ref[idx] // BF16_SUBLANES) * BF16_SUBLANES, cap - win), BF16_SUBLANES)
        starts.append(a)
        ok = offs_ref[idx + 1] - a <= win
        fits = ok if fits is None else jnp.logical_and(fits, ok)

    @pl.when(fits)
    def _():
        lane = lax.broadcasted_iota(I32, (tt, LANES), 1)
        blocks = []
        for blk in range(N_EXPERTS // per_block):
            target = None
            for q in range(per_block):
                ex = blk * per_block + q
                t = slott_ref[0, :, ex:ex + 1] - starts[ex] + q * win
                target = t if target is None else jnp.where(lane >= q * win, t, target)
            blocks.append((target == lane).astype(BF16))
        rows = [ye_ref[0, pl.ds(ex * cap + starts[ex], win), :] for ex in range(N_EXPERTS)]
        acc_s[...] = _dot(jnp.concatenate(blocks, axis=1), jnp.concatenate(rows, axis=0))

    @pl.when(jnp.logical_not(fits))
    def _():
        lane = lax.broadcasted_iota(I32, (tt, win), 1)
        acc_s[...] = jnp.zeros_like(acc_s)
        for ex in range(N_EXPERTS):
            first, nwin = _slot_windows(offs_ref, (bi * N_EXPERTS + ex) * (nch + 1) + j, win)

            def window(w, c, ex=ex, first=first):
                a = pl.multiple_of((first + w) * win, win)
                onehot = ((slott_ref[0, :, ex:ex + 1] - a) == lane).astype(BF16)
                acc_s[...] += _dot(onehot, ye_ref[0, pl.ds(ex * cap + a, win), :])
                return c
            lax.fori_loop(0, nwin, window, 0)

    out = x_ref[0] + g2_ref[0] * acc_s[...]
    if final_norm:
        out = out * lax.rsqrt(jnp.mean(out * out, axis=-1, keepdims=True) + EPS) * fg_ref[...]
    o_ref[0] = out


def _combine(slott, offs, ye, x, g2, fg, cap, final_norm):
    b, s, d = x.shape
    e = N_EXPERTS
    tt = min(s, TOKEN_CHUNK)
    win = min(cap, SCATTER_WINDOW)
    assert cap % win == 0 and LANES % win == 0 and e % (LANES // win) == 0
    return pl.pallas_call(
        functools.partial(_combine_kernel, cap=cap, win=win, final_norm=final_norm),
        grid_spec=pltpu.PrefetchScalarGridSpec(
            num_scalar_prefetch=1, grid=(b, s // tt),
            in_specs=[pl.BlockSpec((1, tt, e), lambda i, j, o: (i, j, 0)),
                      pl.BlockSpec((1, e * cap, d), lambda i, j, o: (i, 0, 0)),
                      pl.BlockSpec((1, tt, d), lambda i, j, o: (i, j, 0)),
                      pl.BlockSpec((1, 1, d), lambda i, j, o: (i, 0, 0)),
                      pl.BlockSpec((1, d), lambda i, j, o: (0, 0))],
            out_specs=pl.BlockSpec((1, tt, d), lambda i, j, o: (i, j, 0)),
            scratch_shapes=[pltpu.VMEM((tt, d), F32)]),
        out_shape=jax.ShapeDtypeStruct((b, s, d), F32),
        name="moe_combine",
    )(offs.reshape(-1), slott, ye, x, g2, fg)


def _split_hi_lo(w):
    hi = w.astype(BF16)
    return jnp.concatenate([hi, (w - hi.astype(F32)).astype(BF16)], axis=0)


def _permute_in_cols(w):
    o_k = Q_W
    o_v = o_k + KV_W
    o_ca = o_v + KV_W
    o_cg = o_ca + CONV_CH
    o_p = o_cg + CONV_CH
    o_g = o_p + POOL_CH
    return jnp.concatenate([w[:, o_g:], w[:, :o_k], w[:, o_ca:o_cg], w[:, o_cg:o_p], w[:, o_p:o_g],
                            w[:, o_k:o_v], w[:, o_v:o_ca]], axis=1)


def kernel(x, c, ctx, c_ctx, norm1_g, norm2_g, w_mod, b_mod, w_in, attn_sink, w_attn_o, conv_dw, conv_dw_b,
           conv_ln_g, conv_ln_b, w_conv_o, w_pool, pool_scale, w_pool_o, w_out, w_router, w_e_gate, w_e_up,
           w_e_down, final_norm_g):
    b, s, d = x.shape
    l = ctx.shape[1]
    depth = w_in.shape[0]
    assert d == _D_MODEL and w_in.shape[2] == IN_W

    tabs = _rope_tables(s)
    cc = jnp.zeros((8, d), F32).at[:b].set(c).at[b].set(c_ctx)
    mod = _modulation(cc, w_mod, b_mod)
    fg = final_norm_g.reshape(1, d)

    for layer in range(depth):
        last = layer == depth - 1
        mx = mod[layer, :b].reshape(b, 1, 6, d)
        sh1, sc1, g1, sh2, sc2, g2 = [mx[:, :, i] for i in range(6)]
        mc = jnp.broadcast_to(mod[layer, b].reshape(1, 1, 6, d), (b, 1, 6, d))
        csh1, csc1, cg1, csh2, csc2, cg2 = [mc[:, :, i] for i in range(6)]
        n1g = norm1_g[layer].reshape(1, d)
        n2g = norm2_g[layer].reshape(1, d)
        w_in_l = _permute_in_cols(w_in[layer]).astype(BF16)
        lw = {'conv_dw': conv_dw[layer], 'conv_dw_b': conv_dw_b[layer].reshape(1, -1),
              'conv_ln_g': conv_ln_g[layer].reshape(1, -1), 'conv_ln_b': conv_ln_b[layer].reshape(1, -1),
              'w_pool': w_pool[layer].astype(BF16), 'pool_scale': pool_scale[layer].reshape(1, -1),
              'w_attn_o': w_attn_o[layer].astype(BF16), 'w_conv_o': w_conv_o[layer].astype(BF16),
              'w_pool_o': w_pool_o[layer].astype(BF16), 'w_out': w_out[layer].astype(BF16),
              'w_router_t': _split_hi_lo(w_router[layer].T)}

        p_x = _inproj(x, n1g, sh1, sc1, w_in_l, tm=512)
        p_c = _inproj(ctx, n1g, csh1, csc1, w_in_l, tm=l)
        attn_x = _window_attention(p_x, p_c, attn_sink[layer], tabs)
        x_mid, h2, afft = _merge(p_x, attn_x, x, g1, sh2, sc2, n2g, lw, tt=256)
        if not last:
            attn_c = _context_attention(p_c, attn_sink[layer])
            c_mid, ch2, cafft = _merge(p_c, attn_c, ctx, cg1, csh2, csc2, n2g, lw, tt=l)
        cap = (CAPACITY_FACTOR * s) // N_EXPERTS
        slot, slott, offs = _topk(afft, cap)
        if last:
            ye = _expert_ffn(layer, slot, offs, afft, h2, w_e_gate, w_e_up, w_e_down, cap)
        else:
            cap_c = (CAPACITY_FACTOR * l) // N_EXPERTS
            cslot, cslott, coffs = _topk(cafft, cap_c)
            sample_base = (jnp.arange(b, dtype=I32) * cap_c)[:, None, None]
            cslot_all = jnp.where(cslot >= 0, cslot + sample_base, -1)
            cslot_all = cslot_all.transpose(1, 0, 2).reshape(N_EXPERTS, 1, b * l)
            cafft_all = cafft.transpose(1, 0, 2).reshape(N_EXPERTS, 1, b * l)
            ye, yec = _expert_ffn(layer, slot, offs, afft, h2, w_e_gate, w_e_up, w_e_down, cap,
                                  ctx_part=(cslot_all, cafft_all, ch2.reshape(b * l, d), b * cap_c))
            yec = yec.reshape(N_EXPERTS, b, cap_c, d).transpose(1, 0, 2, 3).reshape(b, N_EXPERTS * cap_c, d)
            ctx = _combine(cslott, coffs, yec, c_mid, cg2, fg, cap_c, False)
        x = _combine(slott, offs, ye, x_mid, g2, fg, cap, last)
    return x
```

```python
import functools

import jax
import jax.numpy as jnp
import numpy as np
from jax import lax
from jax.experimental import pallas as pl
from jax.experimental.pallas import tpu as pltpu

F32 = jnp.float32
BF16 = jnp.bfloat16
I32 = jnp.int32

EPS = 1e-6
GRID_W = 64
N_HEADS = 8
N_KV_HEADS = 2
HEAD_DIM = 64
GQA_GROUP = N_HEADS // N_KV_HEADS
WINDOW = 128
ATTN_BLOCK = 128
ROPE_BASE = 10000.0
ROPE_PAIRS = HEAD_DIM // 4
CONV_CH = 512
CONV_K = 31
CONV_PAD = CONV_K // 2
POOL_WINDOWS = (2, 4, 8, 16)
POOL_GROUP = 128
POOL_CH = POOL_GROUP * len(POOL_WINDOWS)
N_EXPERTS = 16
CAPACITY_FACTOR = 2
Q_W = N_HEADS * HEAD_DIM
KV_W = N_KV_HEADS * HEAD_DIM

LANES = 128
HALO = 16
HIGHEST = lax.Precision.HIGHEST
F32_TINY = 2.0 ** -126
F32_MANTISSA_BITS = 23
TOKEN_CHUNK = 256
GATHER_WINDOW = 128
SCATTER_WINDOW = 64
F32_SUBLANES = 8
BF16_SUBLANES = 16
ATTN_LOOKAHEAD = 3

_D_MODEL = 1024
COL_GATES = 0
COL_Q = 3 * _D_MODEL
COL_CONV_A = COL_Q + Q_W
COL_CONV_G = COL_CONV_A + CONV_CH
COL_POOL = COL_CONV_G + CONV_CH
COL_K = COL_POOL + POOL_CH
COL_V = COL_K + KV_W
IN_W = COL_V + KV_W


def _dot(a, b):
    return jnp.dot(a, b, preferred_element_type=F32)


def _dot_nt(a, b, precision=None):
    return lax.dot_general(a, b, (((1,), (1,)), ((), ())), preferred_element_type=F32, precision=precision)


def _sigmoid(v):
    return 0.5 * jnp.tanh(0.5 * v) + 0.5


def _rms_mod(x, g, sh, sc):
    y = x * lax.rsqrt(jnp.mean(x * x, axis=-1, keepdims=True) + EPS) * g
    return y * (1.0 + sc) + sh


def _mod_kernel(c_ref, w_ref, b_ref, o_ref):
    c = c_ref[...]
    a = c * _sigmoid(c)
    o_ref[0] = jnp.dot(a, w_ref[0], preferred_element_type=F32, precision=HIGHEST) + b_ref[0]


def _modulation(cc, w_mod, b_mod, tn=1536):
    depth, d, n = w_mod.shape
    rows = cc.shape[0]
    return pl.pallas_call(
        _mod_kernel,
        grid=(depth, n // tn),
        in_specs=[pl.BlockSpec((rows, d), lambda l, j: (0, 0)),
                  pl.BlockSpec((1, d, tn), lambda l, j: (l, 0, j)),
                  pl.BlockSpec((1, 1, tn), lambda l, j: (l, 0, j))],
        out_specs=pl.BlockSpec((1, rows, tn), lambda l, j: (l, 0, j)),
        out_shape=jax.ShapeDtypeStruct((depth, rows, n), F32),
        name="modulation",
    )(cc, w_mod, b_mod.reshape(depth, 1, n))


def _inproj_kernel(x_ref, g_ref, sh_ref, sc_ref, w_ref, o_ref, *, cn):
    h = _rms_mod(x_ref[0], g_ref[...], sh_ref[0], sc_ref[0]).astype(BF16)
    n = w_ref.shape[1]
    for j in range(n // cn):
        o_ref[0, :, j * cn:(j + 1) * cn] = _dot(h, w_ref[:, j * cn:(j + 1) * cn]).astype(o_ref.dtype)


def _inproj(x, g, sh, sc, w, tm):
    b, s, d = x.shape
    n = w.shape[1]
    cn = 256 if n % 768 else 768
    return pl.pallas_call(
        functools.partial(_inproj_kernel, cn=cn),
        grid=(b, s // tm),
        in_specs=[pl.BlockSpec((1, tm, d), lambda i, j: (i, j, 0)),
                  pl.BlockSpec((1, d), lambda i, j: (0, 0)),
                  pl.BlockSpec((1, 1, d), lambda i, j: (i, 0, 0)),
                  pl.BlockSpec((1, 1, d), lambda i, j: (i, 0, 0)),
                  pl.BlockSpec((d, n), lambda i, j: (0, 0))],
        out_specs=pl.BlockSpec((1, tm, n), lambda i, j: (i, j, 0)),
        out_shape=jax.ShapeDtypeStruct((b, s, n), BF16),
        name="inproj",
    )(x, g, sh, sc, w)


def _rope_tables(s):
    t = np.arange(s)
    row = (t // GRID_W).astype(np.float32)
    col = (t % GRID_W).astype(np.float32)
    freqs = jnp.asarray(ROPE_BASE, F32) ** (-jnp.arange(ROPE_PAIRS, dtype=F32) / ROPE_PAIRS)
    ang_r = jnp.asarray(row)[:, None] * freqs
    ang_c = jnp.asarray(col)[:, None] * freqs
    cos_h = jnp.concatenate([jnp.cos(ang_r), jnp.cos(ang_r), jnp.cos(ang_c), jnp.cos(ang_c)], axis=-1)
    sin_h = jnp.concatenate([-jnp.sin(ang_r), jnp.sin(ang_r), -jnp.sin(ang_c), jnp.sin(ang_c)], axis=-1)
    return jnp.tile(cos_h, (1, LANES // HEAD_DIM)), jnp.tile(sin_h, (1, LANES // HEAD_DIM))


def _rope(x, cos, sin_signed):
    lane = lax.broadcasted_iota(I32, x.shape, 1)
    low = (lane & (2 * ROPE_PAIRS - 1)) < ROPE_PAIRS
    partner = jnp.where(low, pltpu.roll(x, LANES - ROPE_PAIRS, 1), pltpu.roll(x, ROPE_PAIRS, 1))
    return x * cos + partner * sin_signed


def _softmax_pv(s_list, v_list, sink):
    m = sink
    for s in s_list:
        m = jnp.maximum(m, jnp.max(s, axis=-1, keepdims=True))
    denom = jnp.exp(sink - m)
    o = None
    for s, v in zip(s_list, v_list):
        e = jnp.exp(s - m)
        denom = denom + jnp.sum(e, axis=-1, keepdims=True)
        pv = _dot(e.astype(BF16), v)
        o = pv if o is None else o + pv
    return o / denom


def _lane_lo(shape):
    return lax.broadcasted_iota(I32, shape, 1) < HEAD_DIM


def _dup_heads(t):
    swapped = pltpu.roll(t, HEAD_DIM, 1)
    lo = _lane_lo(t.shape)
    return jnp.where(lo, t, swapped), jnp.where(lo, swapped, t)


def _heads_attention(qps, keys, vals, masks, sinks):
    lo = _lane_lo(qps[0].shape)
    keeps = (lo, jnp.logical_not(lo))
    tiles_per_kv = len(qps) // len(keys)

    def head_scores(h):
        qp = qps[h // 2]
        qh = jnp.where(keeps[h % 2], qp, jnp.zeros_like(qp))
        return [sc if mask is None else jnp.where(mask, sc, -1e30)
                for sc, mask in zip([_dot_nt(qh, k) for k in keys[h // 2 // tiles_per_kv]], masks)]

    n_heads = 2 * len(qps)
    outs = []
    pending = [head_scores(h) for h in range(min(ATTN_LOOKAHEAD, n_heads))]
    for h in range(n_heads):
        if h + ATTN_LOOKAHEAD < n_heads:
            pending.append(head_scores(h + ATTN_LOOKAHEAD))
        outs.append(_softmax_pv(pending.pop(0), vals[h // 2 // tiles_per_kv], sinks[h]))
    return [jnp.where(lo, outs[2 * i], outs[2 * i + 1]) for i in range(len(qps))]


def _store_dup(dst_ref, t):
    d0, d1 = _dup_heads(t)
    dst_ref[0] = d0.astype(dst_ref.dtype)
    dst_ref[1] = d1.astype(dst_ref.dtype)


def _win_attn_kernel(sink_ref, q_ref, k_ref, v_ref, kc_ref, vc_ref, cosq_ref, sinq_ref, cosk_ref, sink_tab_ref,
                     o_ref, kd_ref, vd_ref, kcd_ref, vcd_ref, *, seq):
    i = pl.program_id(1)
    blk = ATTN_BLOCK
    win = 3 * blk

    @pl.when(i == 0)
    def _():
        _store_dup(kd_ref, _rope(k_ref[0].astype(F32), cosk_ref[...], sink_tab_ref[...]))
        _store_dup(vd_ref, v_ref[0].astype(F32))
        _store_dup(kcd_ref, kc_ref[0].astype(F32))
        _store_dup(vcd_ref, vc_ref[0].astype(F32))

    start = pl.multiple_of(jnp.clip((i - 1) * blk, 0, seq - win), blk)
    qpos = i * blk + lax.broadcasted_iota(I32, (blk, win), 0)
    kpos = start + lax.broadcasted_iota(I32, (blk, win), 1)
    mask = jnp.abs(kpos - qpos) <= WINDOW
    cos = cosq_ref[...]
    sin = sinq_ref[...]
    scale = HEAD_DIM ** -0.5
    qps = [(_rope(q_ref[0, :, p * LANES:(p + 1) * LANES].astype(F32), cos, sin) * scale).astype(BF16)
           for p in range(N_HEADS // 2)]
    keys = [[kd_ref[kh, pl.ds(start, win), :], kcd_ref[kh]] for kh in range(N_KV_HEADS)]
    vals = [[vd_ref[kh, pl.ds(start, win), :], vcd_ref[kh]] for kh in range(N_KV_HEADS)]
    sinks = [sink_ref[h] for h in range(N_HEADS)]
    for p, o in enumerate(_heads_attention(qps, keys, vals, [mask, None], sinks)):
        o_ref[0, :, p * LANES:(p + 1) * LANES] = o.astype(o_ref.dtype)


def _window_attention(p_x, p_c, sink, tabs):
    b, s, _ = p_x.shape
    l = p_c.shape[1]
    cos, sin = tabs
    blk = ATTN_BLOCK
    kcol, vcol = COL_K // KV_W, COL_V // KV_W
    return pl.pallas_call(
        functools.partial(_win_attn_kernel, seq=s),
        grid=(b, s // blk),
        in_specs=[pl.BlockSpec(memory_space=pltpu.SMEM),
                  pl.BlockSpec((1, blk, Q_W), lambda i, j: (i, j, COL_Q // Q_W)),
                  pl.BlockSpec((1, s, KV_W), lambda i, j: (i, 0, kcol)),
                  pl.BlockSpec((1, s, KV_W), lambda i, j: (i, 0, vcol)),
                  pl.BlockSpec((1, l, KV_W), lambda i, j: (i, 0, kcol)),
                  pl.BlockSpec((1, l, KV_W), lambda i, j: (i, 0, vcol)),
                  pl.BlockSpec((blk, LANES), lambda i, j: (j, 0)),
                  pl.BlockSpec((blk, LANES), lambda i, j: (j, 0)),
                  pl.BlockSpec((s, LANES), lambda i, j: (0, 0)),
                  pl.BlockSpec((s, LANES), lambda i, j: (0, 0))],
        out_specs=pl.BlockSpec((1, blk, Q_W), lambda i, j: (i, j, 0)),
        out_shape=jax.ShapeDtypeStruct((b, s, Q_W), BF16),
        scratch_shapes=[pltpu.VMEM((N_KV_HEADS, s, LANES), BF16), pltpu.VMEM((N_KV_HEADS, s, LANES), BF16),
                        pltpu.VMEM((N_KV_HEADS, l, LANES), BF16), pltpu.VMEM((N_KV_HEADS, l, LANES), BF16)],
        name="window_attention",
    )(sink, p_x, p_x, p_x, p_c, p_c, cos, sin, cos, sin)


def _ctx_attn_kernel(sink_ref, q_ref, k_ref, v_ref, o_ref):
    kd = [t.astype(BF16) for t in _dup_heads(k_ref[0].astype(F32))]
    vd = [t.astype(BF16) for t in _dup_heads(v_ref[0].astype(F32))]
    scale = HEAD_DIM ** -0.5
    qps = [(q_ref[0, :, p * LANES:(p + 1) * LANES].astype(F32) * scale).astype(BF16) for p in range(N_HEADS // 2)]
    sinks = [sink_ref[h] for h in range(N_HEADS)]
    outs = _heads_attention(qps, [[k] for k in kd], [[v] for v in vd], [None], sinks)
    for p, o in enumerate(outs):
        o_ref[0, :, p * LANES:(p + 1) * LANES] = o.astype(o_ref.dtype)


def _context_attention(p_c, sink):
    b, l, _ = p_c.shape
    return pl.pallas_call(
        _ctx_attn_kernel,
        grid=(b,),
        in_specs=[pl.BlockSpec(memory_space=pltpu.SMEM),
                  pl.BlockSpec((1, l, Q_W), lambda i: (i, 0, COL_Q // Q_W)),
                  pl.BlockSpec((1, l, KV_W), lambda i: (i, 0, COL_K // KV_W)),
                  pl.BlockSpec((1, l, KV_W), lambda i: (i, 0, COL_V // KV_W))],
        out_specs=pl.BlockSpec((1, l, Q_W), lambda i: (i, 0, 0)),
        out_shape=jax.ShapeDtypeStruct((b, l, Q_W), BF16),
        name="context_attention",
    )(sink, p_c, p_c, p_c)


def _merge_kernel(ga_ref, gb_ref, gc_ref,
                  ca_ref, ca_p_ref, ca_n_ref, cg_ref, cg_p_ref, cg_n_ref, pz_ref, pz_p_ref, pz_n_ref,
                  attn_ref, x_ref, g1_ref, sh2_ref, sc2_ref, n2g_ref,
                  dw_ref, dwb_ref, lng_ref, lnb_ref, wpool_ref, pscale_ref,
                  wa_ref, wb_ref, wc_ref, wo_ref, wrt_ref,
                  xo_ref, h2_ref, afft_ref,
                  uwin_ref, zwin_ref, *, seq):
    t = pl.program_id(1)
    tt = x_ref.shape[1]
    has_prev = (t > 0).astype(F32)
    has_next = (t < pl.num_programs(1) - 1).astype(F32)

    def glu(a_ref, g_ref):
        return a_ref[0].astype(F32) * _sigmoid(g_ref[0].astype(F32))

    uwin_ref[0:HALO, :] = glu(ca_p_ref, cg_p_ref) * has_prev
    uwin_ref[HALO:HALO + tt, :] = glu(ca_ref, cg_ref)
    uwin_ref[HALO + tt:, :] = glu(ca_n_ref, cg_n_ref) * has_next
    first = HALO - CONV_PAD
    rows = tt + 2 * HALO
    acc_cols = []
    for cb in range(CONV_CH // LANES):
        cols = slice(cb * LANES, (cb + 1) * LANES)
        window = uwin_ref[:, cols]
        acc_c = jnp.zeros((tt, LANES), F32) + dwb_ref[:, cols]
        for shift in range(F32_SUBLANES):
            taps = [k for k in range(CONV_K) if (first + k) % F32_SUBLANES == shift]
            if not taps:
                continue
            shifted = window if shift == 0 else pltpu.roll(window, rows - shift, 0)
            for k in taps:
                off = first + k - shift
                acc_c = acc_c + shifted[off:off + tt] * dw_ref[k:k + 1, cols]
        acc_cols.append(acc_c)
    acc = jnp.concatenate(acc_cols, axis=-1)
    mu = jnp.mean(acc, axis=-1, keepdims=True)
    cen = acc - mu
    var = jnp.mean(cen * cen, axis=-1, keepdims=True)
    ln = cen * lax.rsqrt(var + EPS) * lng_ref[...] + lnb_ref[...]
    feat_b = (ln * _sigmoid(ln)).astype(BF16)

    zwin_ref[0:HALO, :] = pz_p_ref[0].astype(F32) * has_prev
    zwin_ref[HALO:HALO + tt, :] = pz_ref[0].astype(F32)
    zwin_ref[HALO + tt:, :] = pz_n_ref[0].astype(F32) * has_next
    tpos = t * tt + lax.broadcasted_iota(I32, (tt, 1), 0)
    pooled = []
    for gi, w in enumerate(POOL_WINDOWS):
        cols = slice(gi * POOL_GROUP, (gi + 1) * POOL_GROUP)
        tot = zwin_ref[pl.ds(HALO - w // 2, tt), cols]
        for d in range(1 - w // 2, w - w // 2):
            tot = tot + zwin_ref[pl.ds(HALO + d, tt), cols]
        cnt = (jnp.minimum(tpos + (w - w // 2), seq) - jnp.maximum(tpos - w // 2, 0)).astype(F32)
        diff = tot / cnt - zwin_ref[pl.ds(HALO, tt), cols]
        pooled.append(_dot(diff.astype(BF16), wpool_ref[gi]))
    feat_c = (jnp.concatenate(pooled, axis=-1) * pscale_ref[...]).astype(BF16)

    y_a = _dot(attn_ref[0], wa_ref[...])
    y_b = _dot(feat_b, wb_ref[...])
    y_c = _dot(feat_c, wc_ref[...])
    merged = (_sigmoid(ga_ref[0].astype(F32)) * y_a + _sigmoid(gb_ref[0].astype(F32)) * y_b
              + _sigmoid(gc_ref[0].astype(F32)) * y_c)
    xn = x_ref[0] + g1_ref[0] * _dot(merged.astype(BF16), wo_ref[...])
    xo_ref[0] = xn

    h2 = _rms_mod(xn, n2g_ref[...], sh2_ref[0], sc2_ref[0])
    h2_hi = h2.astype(BF16)
    h2_ref[0] = h2_hi
    h2_lo = (h2 - h2_hi.astype(F32)).astype(BF16)
    ne = afft_ref.shape[1]
    by_hi = _dot_nt(wrt_ref[...], h2_hi)
    logits_t = by_hi[:ne] + by_hi[ne:] + _dot_nt(wrt_ref[:ne, :], h2_lo)
    et = jnp.exp(logits_t - jnp.max(logits_t, axis=0, keepdims=True))
    afft_ref[0] = et / jnp.sum(et, axis=0, keepdims=True)


def _merge(p, attn, x, g1, sh2, sc2, n2g, lw, tt):
    b, s, d = x.shape
    nh = tt // HALO
    last_h = s // HALO - 1
    e = N_EXPERTS

    def main(width, col):
        return pl.BlockSpec((1, tt, width), lambda i, j: (i, j, col))

    def prev(col):
        return pl.BlockSpec((1, HALO, CONV_CH), lambda i, j: (i, jnp.maximum(j * nh - 1, 0), col))

    def nxt(col):
        return pl.BlockSpec((1, HALO, CONV_CH), lambda i, j: (i, jnp.minimum((j + 1) * nh, last_h), col))

    def per_batch():
        return pl.BlockSpec((1, 1, d), lambda i, j: (i, 0, 0))

    def const(shape):
        return pl.BlockSpec(shape, lambda i, j: (0,) * len(shape))

    ca, cg, pz = COL_CONV_A // CONV_CH, COL_CONV_G // CONV_CH, COL_POOL // CONV_CH
    in_specs = [main(d, 0), main(d, 1), main(d, 2),
                main(CONV_CH, ca), prev(ca), nxt(ca), main(CONV_CH, cg), prev(cg), nxt(cg),
                main(POOL_CH, pz), prev(pz), nxt(pz),
                pl.BlockSpec((1, tt, Q_W), lambda i, j: (i, j, 0)),
                pl.BlockSpec((1, tt, d), lambda i, j: (i, j, 0)),
                per_batch(), per_batch(), per_batch(), const((1, d)),
                const((CONV_K, CONV_CH)), const((1, CONV_CH)), const((1, CONV_CH)), const((1, CONV_CH)),
                const((len(POOL_WINDOWS), POOL_GROUP, POOL_GROUP)), const((1, POOL_CH)),
                const((Q_W, d)), const((CONV_CH, d)), const((POOL_CH, d)), const((d, d)),
                const((2 * e, d))]
    out_specs = [pl.BlockSpec((1, tt, d), lambda i, j: (i, j, 0)),
                 pl.BlockSpec((1, tt, d), lambda i, j: (i, j, 0)),
                 pl.BlockSpec((1, e, tt), lambda i, j: (i, 0, j))]
    out_shape = [jax.ShapeDtypeStruct((b, s, d), F32), jax.ShapeDtypeStruct((b, s, d), BF16),
                 jax.ShapeDtypeStruct((b, e, s), F32)]
    return pl.pallas_call(
        functools.partial(_merge_kernel, seq=s),
        grid=(b, s // tt),
        in_specs=in_specs, out_specs=out_specs, out_shape=out_shape,
        scratch_shapes=[pltpu.VMEM((tt + 2 * HALO, CONV_CH), F32), pltpu.VMEM((tt + 2 * HALO, POOL_CH), F32)],
        name="mix_merge",
    )(p, p, p, p, p, p, p, p, p, p, p, p, attn, x, g1, sh2, sc2, n2g,
      lw['conv_dw'], lw['conv_dw_b'], lw['conv_ln_g'], lw['conv_ln_b'], lw['w_pool'], lw['pool_scale'],
      lw['w_attn_o'], lw['w_conv_o'], lw['w_pool_o'], lw['w_out'], lw['w_router_t'])


def _topk_kernel(afft_ref, slot_ref, slott_ref, offs_ref, *, cap, blk):
    a = afft_ref[0]
    e, s = a.shape
    def keeps_cap(cand):
        return jnp.sum((a >= cand).astype(F32), axis=-1, keepdims=True) >= cap

    tiny = jnp.full((e, 1), F32_TINY, F32)
    thr = jnp.where(keeps_cap(tiny), tiny, 0.0)
    for step in (64, 32, 16, 8, 4, 2, 1):
        cand = thr * float(2 ** step)
        thr = jnp.where(keeps_cap(cand), cand, thr)
    delta = thr
    for _ in range(F32_MANTISSA_BITS):
        delta = delta * 0.5
        cand = thr + delta
        thr = jnp.where(keeps_cap(cand), cand, thr)
    gt = a > thr
    eq = a == thr
    need = cap - jnp.sum(gt.astype(F32), axis=-1, keepdims=True)

    r = lax.broadcasted_iota(I32, (blk, blk), 0)
    c = lax.broadcasted_iota(I32, (blk, blk), 1)
    upper = (r < c).astype(BF16)
    eye = (r == c).astype(F32)

    def prefix(mask_f32):
        carry = jnp.zeros((e, 1), F32)
        parts = []
        for j in range(s // blk):
            m = mask_f32[:, j * blk:(j + 1) * blk]
            parts.append(_dot(m.astype(BF16), upper) + carry)
            carry = carry + jnp.sum(m, axis=-1, keepdims=True)
        return jnp.concatenate(parts, axis=-1)

    sel = gt | (eq & (prefix(eq.astype(F32)) < need))
    pos = prefix(sel.astype(F32))
    slot = jnp.where(sel, pos, -1.0)
    slot_ref[0] = slot.astype(I32)
    for j in range(s // blk):
        slott_ref[0, j * blk:(j + 1) * blk, :] = _dot_nt(eye, slot[:, j * blk:(j + 1) * blk],
                                                        precision=HIGHEST).astype(I32)
    tok = lax.broadcasted_iota(I32, (s, LANES), 0)
    col = lax.broadcasted_iota(I32, (s, LANES), 1)
    before = (tok < col * blk).astype(BF16)
    offs_ref[0] = _dot(sel.astype(BF16), before).astype(I32)


def _topk(afft, cap):
    b, e, s = afft.shape
    blk = min(s, TOKEN_CHUNK)
    slot, slott, offs = pl.pallas_call(
        functools.partial(_topk_kernel, cap=cap, blk=blk),
        grid=(b,),
        in_specs=[pl.BlockSpec((1, e, s), lambda i: (i, 0, 0))],
        out_specs=[pl.BlockSpec((1, e, s), lambda i: (i, 0, 0)),
                   pl.BlockSpec((1, s, e), lambda i: (i, 0, 0)),
                   pl.BlockSpec((1, e, LANES), lambda i: (i, 0, 0))],
        out_shape=[jax.ShapeDtypeStruct((b, e, s), I32), jax.ShapeDtypeStruct((b, s, e), I32),
                   jax.ShapeDtypeStruct((b, e, LANES), I32)],
        name="expert_choice_topk",
    )(afft)
    return slot, slott, offs[:, :, :s // blk + 1]


def _slot_windows(offs_ref, idx, win):
    lo = offs_ref[idx]
    hi = offs_ref[idx + 1]
    first = lo // win
    return first, jnp.where(hi > lo, (hi - 1) // win - first + 1, 0)


def _gather_rows(onehot, h, gate_row):
    picked = _dot(onehot.astype(BF16), h)
    gates = jnp.sum(jnp.where(onehot, gate_row, 0.0), axis=-1, keepdims=True)
    return picked, gates


def _ffn_kernel(offs_ref, slot_ref, afft_ref, h_ref, slotc_ref, cafft_ref, hc_ref, wg_ref, wu_ref, wd_ref,
                ye_ref, yec_ref, wg_s, wu_s, wd_s, xe_s, g_s, *, chunk, group, win, rows):
    ex = pl.program_id(0)
    bi = pl.program_id(1)
    nb = pl.num_programs(1)
    cap = xe_s.shape[0]

    @pl.when(bi == 0)
    def _():
        def cast(i, carry):
            sl = pl.ds(pl.multiple_of(i * rows, rows), rows)
            wg_s[sl, :] = wg_ref[0, 0, sl, :].astype(BF16)
            wu_s[sl, :] = wu_ref[0, 0, sl, :].astype(BF16)
            wd_s[sl, :] = wd_ref[0, 0, sl, :].astype(BF16)
            return carry
        lax.fori_loop(0, wg_s.shape[0] // rows, cast, 0)

    def ffn(xe):
        a = _dot(xe, wg_s[...])
        u = _dot(xe, wu_s[...])
        hid = (a * _sigmoid(a) * u).astype(BF16)
        return _dot(hid, wd_s[...])

    nch = h_ref.shape[1] // chunk
    base = (bi * pl.num_programs(0) + ex) * (nch + 1)
    xe_s[...] = jnp.zeros_like(xe_s)
    g_s[...] = jnp.zeros_like(g_s)
    span = group * chunk
    starts = []
    fits = None
    for p in range(nch // group):
        lo = offs_ref[base + p * group]
        hi = offs_ref[base + (p + 1) * group]
        a = pl.multiple_of(jnp.minimum((lo // F32_SUBLANES) * F32_SUBLANES, cap - win), F32_SUBLANES)
        starts.append(a)
        fits = (hi - a <= win) if fits is None else jnp.logical_and(fits, hi - a <= win)

    def add_window(a, sl, width):
        onehot = (slot_ref[0, :, sl] - a) == lax.broadcasted_iota(I32, (win, width), 0)
        picked, gates = _gather_rows(onehot, h_ref[0, sl, :], afft_ref[0, :, sl])
        xe_s[pl.ds(a, win), :] += picked
        g_s[pl.ds(a, win), :] += gates

    @pl.when(fits)
    def _():
        for p, a in enumerate(starts):
            add_window(a, slice(p * span, (p + 1) * span), span)

    @pl.when(jnp.logical_not(fits))
    def _():
        def per_chunk(j, carry):
            first, nwin = _slot_windows(offs_ref, base + j, win)
            sl = pl.ds(pl.multiple_of(j * chunk, chunk), chunk)

            def window(w, c):
                add_window(pl.multiple_of((first + w) * win, win), sl, chunk)
                return c
            lax.fori_loop(0, nwin, window, 0)
            return carry
        lax.fori_loop(0, nch, per_chunk, 0)

    ye_ref[0] = (ffn(xe_s[...].astype(BF16)) * g_s[...]).astype(ye_ref.dtype)

    if yec_ref is not None:
        @pl.when(bi == nb - 1)
        def _():
            rowc = lax.broadcasted_iota(I32, (yec_ref.shape[1], hc_ref.shape[0]), 0)
            picked, gates = _gather_rows(slotc_ref[0] == rowc, hc_ref[...], cafft_ref[0])
            yec_ref[0] = (ffn(picked.astype(BF16)) * gates).astype(yec_ref.dtype)


def _expert_ffn(layer, slot, offs, afft, h2, wg, wu, wd, cap, ctx_part=None):
    b, e, s = slot.shape
    d = h2.shape[2]
    f = wg.shape[3]
    assert f == d
    chunk = min(s, TOKEN_CHUNK)
    nch = s // chunk
    group = 2 if nch % 2 == 0 else 1
    win = min(cap, GATHER_WINDOW)
    assert cap % win == 0
    w_spec = pl.BlockSpec((1, 1, d, f), lambda j, i, o: (layer, j, 0, 0))
    row_spec = pl.BlockSpec((1, 1, s), lambda j, i, o: (i * e + j, 0, 0))
    in_specs = [row_spec, row_spec, pl.BlockSpec((1, s, d), lambda j, i, o: (i, 0, 0))]
    out_specs = [pl.BlockSpec((1, cap, d), lambda j, i, o: (i * e + j, 0, 0))]
    out_shape = [jax.ShapeDtypeStruct((b * e, cap, d), BF16)]
    args = [slot.reshape(b * e, 1, s), afft.reshape(b * e, 1, s), h2]
    body = functools.partial(_ffn_kernel, chunk=chunk, group=group, win=win, rows=128)
    if ctx_part is None:
        def kern(offs_ref, slot_ref, afft_ref, h_ref, wg_ref, wu_ref, wd_ref, ye_ref, *scratch):
            body(offs_ref, slot_ref, afft_ref, h_ref, None, None, None, wg_ref, wu_ref, wd_ref, ye_ref, None,
                 *scratch)
    else:
        slot_c, afft_c, h_c, rows_c = ctx_part
        n_c = h_c.shape[0]
        rowc_spec = pl.BlockSpec((1, 1, n_c), lambda j, i, o: (j, 0, 0))
        in_specs += [rowc_spec, rowc_spec, pl.BlockSpec((n_c, d), lambda j, i, o: (0, 0))]
        out_specs.append(pl.BlockSpec((1, rows_c, d), lambda j, i, o: (j, 0, 0)))
        out_shape.append(jax.ShapeDtypeStruct((e, rows_c, d), BF16))
        args += [slot_c, afft_c, h_c]
        kern = body
    outs = pl.pallas_call(
        kern,
        grid_spec=pltpu.PrefetchScalarGridSpec(
            num_scalar_prefetch=1, grid=(e, b),
            in_specs=in_specs + [w_spec, w_spec, w_spec], out_specs=out_specs,
            scratch_shapes=[pltpu.VMEM((d, f), BF16), pltpu.VMEM((d, f), BF16), pltpu.VMEM((f, d), BF16),
                            pltpu.VMEM((cap, d), F32), pltpu.VMEM((cap, 1), F32)]),
        out_shape=out_shape,
        name="expert_ffn",
    )(offs.reshape(-1), *args, wg, wu, wd)
    ye = outs[0].reshape(b, e * cap, d)
    return ye if ctx_part is None else (ye, outs[1])


def _combine_kernel(offs_ref, slott_ref, ye_ref, x_ref, g2_ref, fg_ref, o_ref, acc_s, *, cap, win, final_norm):
    bi = pl.program_id(0)
    j = pl.program_id(1)
    nch = pl.num_programs(1)
    tt = x_ref.shape[1]
    per_block = LANES // win

    starts = []
    fits = None
    for ex in range(N_EXPERTS):
        idx = (bi * N_EXPERTS + ex) * (nch + 1) + j
        a = pl.multiple_of(jnp.minimum((offs_ref[idx] // BF16_SUBLANES) * BF16_SUBLANES, cap - win), BF16_SUBLANES)
        starts.append(a)
        ok = offs_ref[idx + 1] - a <= win
        fits = ok if fits is None else jnp.logical_and(fits, ok)

    @pl.when(fits)
    def _():
        lane = lax.broadcasted_iota(I32, (tt, LANES), 1)
        blocks = []
        for blk in range(N_EXPERTS // per_block):
            target = None
            for q in range(per_block):
                ex = blk * per_block + q
                t = slott_ref[0, :, ex:ex + 1] - starts[ex] + q * win
                target = t if target is None else jnp.where(lane >= q * win, t, target)
            blocks.append((target == lane).astype(BF16))
        rows = [ye_ref[0, pl.ds(ex * cap + starts[ex], win), :] for ex in range(N_EXPERTS)]
        acc_s[...] = _dot(jnp.concatenate(blocks, axis=1), jnp.concatenate(rows, axis=0))

    @pl.when(jnp.logical_not(fits))
    def _():
        lane = lax.broadcasted_iota(I32, (tt, win), 1)
        acc_s[...] = jnp.zeros_like(acc_s)
        for ex in range(N_EXPERTS):
            first, nwin = _slot_windows(offs_ref, (bi * N_EXPERTS + ex) * (nch + 1) + j, win)

            def window(w, c, ex=ex, first=first):
                a = pl.multiple_of((first + w) * win, win)
                onehot = ((slott_ref[0, :, ex:ex + 1] - a) == lane).astype(BF16)
                acc_s[...] += _dot(onehot, ye_ref[0, pl.ds(ex * cap + a, win), :])
                return c
            lax.fori_loop(0, nwin, window, 0)

    out = x_ref[0] + g2_ref[0] * acc_s[...]
    if final_norm:
        out = out * lax.rsqrt(jnp.mean(out * out, axis=-1, keepdims=True) + EPS) * fg_ref[...]
    o_ref[0] = out


def _combine(slott, offs, ye, x, g2, fg, cap, final_norm):
    b, s, d = x.shape
    e = N_EXPERTS
    tt = min(s, TOKEN_CHUNK)
    win = min(cap, SCATTER_WINDOW)
    assert cap % win == 0 and LANES % win == 0 and e % (LANES // win) == 0
    return pl.pallas_call(
        functools.partial(_combine_kernel, cap=cap, win=win, final_norm=final_norm),
        grid_spec=pltpu.PrefetchScalarGridSpec(
            num_scalar_prefetch=1, grid=(b, s // tt),
            in_specs=[pl.BlockSpec((1, tt, e), lambda i, j, o: (i, j, 0)),
                      pl.BlockSpec((1, e * cap, d), lambda i, j, o: (i, 0, 0)),
                      pl.BlockSpec((1, tt, d), lambda i, j, o: (i, j, 0)),
                      pl.BlockSpec((1, 1, d), lambda i, j, o: (i, 0, 0)),
                      pl.BlockSpec((1, d), lambda i, j, o: (0, 0))],
            out_specs=pl.BlockSpec((1, tt, d), lambda i, j, o: (i, j, 0)),
            scratch_shapes=[pltpu.VMEM((tt, d), F32)]),
        out_shape=jax.ShapeDtypeStruct((b, s, d), F32),
        name="moe_combine",
    )(offs.reshape(-1), slott, ye, x, g2, fg)


def _split_hi_lo(w):
    hi = w.astype(BF16)
    return jnp.concatenate([hi, (w - hi.astype(F32)).astype(BF16)], axis=0)


def _permute_in_cols(w):
    o_k = Q_W
    o_v = o_k + KV_W
    o_ca = o_v + KV_W
    o_cg = o_ca + CONV_CH
    o_p = o_cg + CONV_CH
    o_g = o_p + POOL_CH
    return jnp.concatenate([w[:, o_g:], w[:, :o_k], w[:, o_ca:o_cg], w[:, o_cg:o_p], w[:, o_p:o_g],
                            w[:, o_k:o_v], w[:, o_v:o_ca]], axis=1)


def kernel(x, c, ctx, c_ctx, norm1_g, norm2_g, w_mod, b_mod, w_in, attn_sink, w_attn_o, conv_dw, conv_dw_b,
           conv_ln_g, conv_ln_b, w_conv_o, w_pool, pool_scale, w_pool_o, w_out, w_router, w_e_gate, w_e_up,
           w_e_down, final_norm_g):
    b, s, d = x.shape
    l = ctx.shape[1]
    depth = w_in.shape[0]
    assert d == _D_MODEL and w_in.shape[2] == IN_W

    tabs = _rope_tables(s)
    cc = jnp.zeros((8, d), F32).at[:b].set(c).at[b].set(c_ctx)
    mod = _modulation(cc, w_mod, b_mod)
    fg = final_norm_g.reshape(1, d)

    for layer in range(depth):
        last = layer == depth - 1
        mx = mod[layer, :b].reshape(b, 1, 6, d)
        sh1, sc1, g1, sh2, sc2, g2 = [mx[:, :, i] for i in range(6)]
        mc = jnp.broadcast_to(mod[layer, b].reshape(1, 1, 6, d), (b, 1, 6, d))
        csh1, csc1, cg1, csh2, csc2, cg2 = [mc[:, :, i] for i in range(6)]
        n1g = norm1_g[layer].reshape(1, d)
        n2g = norm2_g[layer].reshape(1, d)
        w_in_l = _permute_in_cols(w_in[layer]).astype(BF16)
        lw = {'conv_dw': conv_dw[layer], 'conv_dw_b': conv_dw_b[layer].reshape(1, -1),
              'conv_ln_g': conv_ln_g[layer].reshape(1, -1), 'conv_ln_b': conv_ln_b[layer].reshape(1, -1),
              'w_pool': w_pool[layer].astype(BF16), 'pool_scale': pool_scale[layer].reshape(1, -1),
              'w_attn_o': w_attn_o[layer].astype(BF16), 'w_conv_o': w_conv_o[layer].astype(BF16),
              'w_pool_o': w_pool_o[layer].astype(BF16), 'w_out': w_out[layer].astype(BF16),
              'w_router_t': _split_hi_lo(w_router[layer].T)}

        p_x = _inproj(x, n1g, sh1, sc1, w_in_l, tm=512)
        p_c = _inproj(ctx, n1g, csh1, csc1, w_in_l, tm=l)
        attn_x = _window_attention(p_x, p_c, attn_sink[layer], tabs)
        x_mid, h2, afft = _merge(p_x, attn_x, x, g1, sh2, sc2, n2g, lw, tt=256)
        if not last:
            attn_c = _context_attention(p_c, attn_sink[layer])
            c_mid, ch2, cafft = _merge(p_c, attn_c, ctx, cg1, csh2, csc2, n2g, lw, tt=l)
        cap = (CAPACITY_FACTOR * s) // N_EXPERTS
        slot, slott, offs = _topk(afft, cap)
        if last:
            ye = _expert_ffn(layer, slot, offs, afft, h2, w_e_gate, w_e_up, w_e_down, cap)
        else:
            cap_c = (CAPACITY_FACTOR * l) // N_EXPERTS
            cslot, cslott, coffs = _topk(cafft, cap_c)
            sample_base = (jnp.arange(b, dtype=I32) * cap_c)[:, None, None]
            cslot_all = jnp.where(cslot >= 0, cslot + sample_base, -1)
            cslot_all = cslot_all.transpose(1, 0, 2).reshape(N_EXPERTS, 1, b * l)
            cafft_all = cafft.transpose(1, 0, 2).reshape(N_EXPERTS, 1, b * l)
            ye, yec = _expert_ffn(layer, slot, offs, afft, h2, w_e_gate, w_e_up, w_e_down, cap,
                                  ctx_part=(cslot_all, cafft_all, ch2.reshape(b * l, d), b * cap_c))
            yec = yec.reshape(N_EXPERTS, b, cap_c, d).transpose(1, 0, 2, 3).reshape(b, N_EXPERTS * cap_c, d)
            ctx = _combine(cslott, coffs, yec, c_mid, cg2, fg, cap_c, False)
        x = _combine(slott, offs, ye, x_mid, g2, fg, cap, last)
    return x
```

```python
import functools

import jax
import jax.numpy as jnp
import numpy as np
from jax import lax
from jax.experimental import pallas as pl
from jax.experimental.pallas import tpu as pltpu

F32 = jnp.float32
BF16 = jnp.bfloat16
I32 = jnp.int32

EPS = 1e-6
GRID_W = 64
N_HEADS = 8
N_KV_HEADS = 2
HEAD_DIM = 64
GQA_GROUP = N_HEADS // N_KV_HEADS
WINDOW = 128
ATTN_BLOCK = 128
ROPE_BASE = 10000.0
ROPE_PAIRS = HEAD_DIM // 4
CONV_CH = 512
CONV_K = 31
CONV_PAD = CONV_K // 2
POOL_WINDOWS = (2, 4, 8, 16)
POOL_GROUP = 128
POOL_CH = POOL_GROUP * len(POOL_WINDOWS)
N_EXPERTS = 16
CAPACITY_FACTOR = 2
Q_W = N_HEADS * HEAD_DIM
KV_W = N_KV_HEADS * HEAD_DIM

LANES = 128
HALO = 16
HIGHEST = lax.Precision.HIGHEST
F32_TINY = 2.0 ** -126
F32_MANTISSA_BITS = 23
TOKEN_CHUNK = 256
GATHER_WINDOW = 128
SCATTER_WINDOW = 64
F32_SUBLANES = 8
BF16_SUBLANES = 16
ATTN_LOOKAHEAD = 3

_D_MODEL = 1024
COL_GATES = 0
COL_Q = 3 * _D_MODEL
COL_CONV_A = COL_Q + Q_W
COL_CONV_G = COL_CONV_A + CONV_CH
COL_POOL = COL_CONV_G + CONV_CH
COL_K = COL_POOL + POOL_CH
COL_V = COL_K + KV_W
IN_W = COL_V + KV_W


def _dot(a, b):
    return jnp.dot(a, b, preferred_element_type=F32)


def _dot_nt(a, b, precision=None):
    return lax.dot_general(a, b, (((1,), (1,)), ((), ())), preferred_element_type=F32, precision=precision)


def _sigmoid(v):
    return 0.5 * jnp.tanh(0.5 * v) + 0.5


def _rms_mod(x, g, sh, sc):
    y = x * lax.rsqrt(jnp.mean(x * x, axis=-1, keepdims=True) + EPS) * g
    return y * (1.0 + sc) + sh


def _mod_kernel(c_ref, w_ref, b_ref, o_ref):
    c = c_ref[...]
    a = c * _sigmoid(c)
    o_ref[0] = jnp.dot(a, w_ref[0], preferred_element_type=F32, precision=HIGHEST) + b_ref[0]


def _modulation(cc, w_mod, b_mod, tn=1536):
    depth, d, n = w_mod.shape
    rows = cc.shape[0]
    return pl.pallas_call(
        _mod_kernel,
        grid=(depth, n // tn),
        in_specs=[pl.BlockSpec((rows, d), lambda l, j: (0, 0)),
                  pl.BlockSpec((1, d, tn), lambda l, j: (l, 0, j)),
                  pl.BlockSpec((1, 1, tn), lambda l, j: (l, 0, j))],
        out_specs=pl.BlockSpec((1, rows, tn), lambda l, j: (l, 0, j)),
        out_shape=jax.ShapeDtypeStruct((depth, rows, n), F32),
        name="modulation",
    )(cc, w_mod, b_mod.reshape(depth, 1, n))


def _inproj_kernel(x_ref, g_ref, sh_ref, sc_ref, w_ref, o_ref, *, cn):
    h = _rms_mod(x_ref[0], g_ref[...], sh_ref[0], sc_ref[0]).astype(BF16)
    n = w_ref.shape[1]
    for j in range(n // cn):
        o_ref[0, :, j * cn:(j + 1) * cn] = _dot(h, w_ref[:, j * cn:(j + 1) * cn]).astype(o_ref.dtype)


def _inproj(x, g, sh, sc, w, tm):
    b, s, d = x.shape
    n = w.shape[1]
    cn = 256 if n % 768 else 768
    return pl.pallas_call(
        functools.partial(_inproj_kernel, cn=cn),
        grid=(b, s // tm),
        in_specs=[pl.BlockSpec((1, tm, d), lambda i, j: (i, j, 0)),
                  pl.BlockSpec((1, d), lambda i, j: (0, 0)),
                  pl.BlockSpec((1, 1, d), lambda i, j: (i, 0, 0)),
                  pl.BlockSpec((1, 1, d), lambda i, j: (i, 0, 0)),
                  pl.BlockSpec((d, n), lambda i, j: (0, 0))],
        out_specs=pl.BlockSpec((1, tm, n), lambda i, j: (i, j, 0)),
        out_shape=jax.ShapeDtypeStruct((b, s, n), BF16),
        name="inproj",
    )(x, g, sh, sc, w)


def _rope_tables(s):
    t = np.arange(s)
    row = (t // GRID_W).astype(np.float32)
    col = (t % GRID_W).astype(np.float32)
    freqs = jnp.asarray(ROPE_BASE, F32) ** (-jnp.arange(ROPE_PAIRS, dtype=F32) / ROPE_PAIRS)
    ang_r = jnp.asarray(row)[:, None] * freqs
    ang_c = jnp.asarray(col)[:, None] * freqs
    cos_h = jnp.concatenate([jnp.cos(ang_r), jnp.cos(ang_r), jnp.cos(ang_c), jnp.cos(ang_c)], axis=-1)
    sin_h = jnp.concatenate([-jnp.sin(ang_r), jnp.sin(ang_r), -jnp.sin(ang_c), jnp.sin(ang_c)], axis=-1)
    return jnp.tile(cos_h, (1, LANES // HEAD_DIM)), jnp.tile(sin_h, (1, LANES // HEAD_DIM))


def _rope(x, cos, sin_signed):
    lane = lax.broadcasted_iota(I32, x.shape, 1)
    low = (lane & (2 * ROPE_PAIRS - 1)) < ROPE_PAIRS
    partner = jnp.where(low, pltpu.roll(x, LANES - ROPE_PAIRS, 1), pltpu.roll(x, ROPE_PAIRS, 1))
    return x * cos + partner * sin_signed


def _softmax_pv(s_list, v_list, sink):
    m = sink
    for s in s_list:
        m = jnp.maximum(m, jnp.max(s, axis=-1, keepdims=True))
    denom = jnp.exp(sink - m)
    o = None
    for s, v in zip(s_list, v_list):
        e = jnp.exp(s - m)
        denom = denom + jnp.sum(e, axis=-1, keepdims=True)
        pv = _dot(e.astype(BF16), v)
        o = pv if o is None else o + pv
    return o / denom


def _lane_lo(shape):
    return lax.broadcasted_iota(I32, shape, 1) < HEAD_DIM


def _dup_heads(t):
    swapped = pltpu.roll(t, HEAD_DIM, 1)
    lo = _lane_lo(t.shape)
    return jnp.where(lo, t, swapped), jnp.where(lo, swapped, t)


def _heads_attention(qps, keys, vals, masks, sinks):
    lo = _lane_lo(qps[0].shape)
    keeps = (lo, jnp.logical_not(lo))
    tiles_per_kv = len(qps) // len(keys)

    def head_scores(h):
        qp = qps[h // 2]
        qh = jnp.where(keeps[h % 2], qp, jnp.zeros_like(qp))
        return [sc if mask is None else jnp.where(mask, sc, -1e30)
                for sc, mask in zip([_dot_nt(qh, k) for k in keys[h // 2 // tiles_per_kv]], masks)]

    n_heads = 2 * len(qps)
    outs = []
    pending = [head_scores(h) for h in range(min(ATTN_LOOKAHEAD, n_heads))]
    for h in range(n_heads):
        if h + ATTN_LOOKAHEAD < n_heads:
            pending.append(head_scores(h + ATTN_LOOKAHEAD))
        outs.append(_softmax_pv(pending.pop(0), vals[h // 2 // tiles_per_kv], sinks[h]))
    return [jnp.where(lo, outs[2 * i], outs[2 * i + 1]) for i in range(len(qps))]


def _store_dup(dst_ref, t):
    d0, d1 = _dup_heads(t)
    dst_ref[0] = d0.astype(dst_ref.dtype)
    dst_ref[1] = d1.astype(dst_ref.dtype)


def _win_attn_kernel(sink_ref, q_ref, k_ref, v_ref, kc_ref, vc_ref, cosq_ref, sinq_ref, cosk_ref, sink_tab_ref,
                     o_ref, kd_ref, vd_ref, kcd_ref, vcd_ref, *, seq):
    i = pl.program_id(1)
    blk = ATTN_BLOCK
    win = 3 * blk

    @pl.when(i == 0)
    def _():
        _store_dup(kd_ref, _rope(k_ref[0].astype(F32), cosk_ref[...], sink_tab_ref[...]))
        _store_dup(vd_ref, v_ref[0].astype(F32))
        _store_dup(kcd_ref, kc_ref[0].astype(F32))
        _store_dup(vcd_ref, vc_ref[0].astype(F32))

    start = pl.multiple_of(jnp.clip((i - 1) * blk, 0, seq - win), blk)
    qpos = i * blk + lax.broadcasted_iota(I32, (blk, win), 0)
    kpos = start + lax.broadcasted_iota(I32, (blk, win), 1)
    mask = jnp.abs(kpos - qpos) <= WINDOW
    cos = cosq_ref[...]
    sin = sinq_ref[...]
    scale = HEAD_DIM ** -0.5
    qps = [(_rope(q_ref[0, :, p * LANES:(p + 1) * LANES].astype(F32), cos, sin) * scale).astype(BF16)
           for p in range(N_HEADS // 2)]
    keys = [[kd_ref[kh, pl.ds(start, win), :], kcd_ref[kh]] for kh in range(N_KV_HEADS)]
    vals = [[vd_ref[kh, pl.ds(start, win), :], vcd_ref[kh]] for kh in range(N_KV_HEADS)]
    sinks = [sink_ref[h] for h in range(N_HEADS)]
    for p, o in enumerate(_heads_attention(qps, keys, vals, [mask, None], sinks)):
        o_ref[0, :, p * LANES:(p + 1) * LANES] = o.astype(o_ref.dtype)


def _window_attention(p_x, p_c, sink, tabs):
    b, s, _ = p_x.shape
    l = p_c.shape[1]
    cos, sin = tabs
    blk = ATTN_BLOCK
    kcol, vcol = COL_K // KV_W, COL_V // KV_W
    return pl.pallas_call(
        functools.partial(_win_attn_kernel, seq=s),
        grid=(b, s // blk),
        in_specs=[pl.BlockSpec(memory_space=pltpu.SMEM),
                  pl.BlockSpec((1, blk, Q_W), lambda i, j: (i, j, COL_Q // Q_W)),
                  pl.BlockSpec((1, s, KV_W), lambda i, j: (i, 0, kcol)),
                  pl.BlockSpec((1, s, KV_W), lambda i, j: (i, 0, vcol)),
                  pl.BlockSpec((1, l, KV_W), lambda i, j: (i, 0, kcol)),
                  pl.BlockSpec((1, l, KV_W), lambda i, j: (i, 0, vcol)),
                  pl.BlockSpec((blk, LANES), lambda i, j: (j, 0)),
                  pl.BlockSpec((blk, LANES), lambda i, j: (j, 0)),
                  pl.BlockSpec((s, LANES), lambda i, j: (0, 0)),
                  pl.BlockSpec((s, LANES), lambda i, j: (0, 0))],
        out_specs=pl.BlockSpec((1, blk, Q_W), lambda i, j: (i, j, 0)),
        out_shape=jax.ShapeDtypeStruct((b, s, Q_W), BF16),
        scratch_shapes=[pltpu.VMEM((N_KV_HEADS, s, LANES), BF16), pltpu.VMEM((N_KV_HEADS, s, LANES), BF16),
                        pltpu.VMEM((N_KV_HEADS, l, LANES), BF16), pltpu.VMEM((N_KV_HEADS, l, LANES), BF16)],
        name="window_attention",
    )(sink, p_x, p_x, p_x, p_c, p_c, cos, sin, cos, sin)


def _ctx_attn_kernel(sink_ref, q_ref, k_ref, v_ref, o_ref):
    kd = [t.astype(BF16) for t in _dup_heads(k_ref[0].astype(F32))]
    vd = [t.astype(BF16) for t in _dup_heads(v_ref[0].astype(F32))]
    scale = HEAD_DIM ** -0.5
    qps = [(q_ref[0, :, p * LANES:(p + 1) * LANES].astype(F32) * scale).astype(BF16) for p in range(N_HEADS // 2)]
    sinks = [sink_ref[h] for h in range(N_HEADS)]
    outs = _heads_attention(qps, [[k] for k in kd], [[v] for v in vd], [None], sinks)
    for p, o in enumerate(outs):
        o_ref[0, :, p * LANES:(p + 1) * LANES] = o.astype(o_ref.dtype)


def _context_attention(p_c, sink):
    b, l, _ = p_c.shape
    return pl.pallas_call(
        _ctx_attn_kernel,
        grid=(b,),
        in_specs=[pl.BlockSpec(memory_space=pltpu.SMEM),
                  pl.BlockSpec((1, l, Q_W), lambda i: (i, 0, COL_Q // Q_W)),
                  pl.BlockSpec((1, l, KV_W), lambda i: (i, 0, COL_K // KV_W)),
                  pl.BlockSpec((1, l, KV_W), lambda i: (i, 0, COL_V // KV_W))],
        out_specs=pl.BlockSpec((1, l, Q_W), lambda i: (i, 0, 0)),
        out_shape=jax.ShapeDtypeStruct((b, l, Q_W), BF16),
        name="context_attention",
    )(sink, p_c, p_c, p_c)


def _merge_kernel(ga_ref, gb_ref, gc_ref,
                  ca_ref, ca_p_ref, ca_n_ref, cg_ref, cg_p_ref, cg_n_ref, pz_ref, pz_p_ref, pz_n_ref,
                  attn_ref, x_ref, g1_ref, sh2_ref, sc2_ref, n2g_ref,
                  dw_ref, dwb_ref, lng_ref, lnb_ref, wpool_ref, pscale_ref,
                  wa_ref, wb_ref, wc_ref, wo_ref, wrt_ref,
                  xo_ref, h2_ref, afft_ref,
                  uwin_ref, zwin_ref, *, seq):
    t = pl.program_id(1)
    tt = x_ref.shape[1]
    has_prev = (t > 0).astype(F32)
    has_next = (t < pl.num_programs(1) - 1).astype(F32)

    def glu(a_ref, g_ref):
        return a_ref[0].astype(F32) * _sigmoid(g_ref[0].astype(F32))

    y_a = _dot(attn_ref[0], wa_ref[...])

    uwin_ref[0:HALO, :] = glu(ca_p_ref, cg_p_ref) * has_prev
    uwin_ref[HALO:HALO + tt, :] = glu(ca_ref, cg_ref)
    uwin_ref[HALO + tt:, :] = glu(ca_n_ref, cg_n_ref) * has_next
    first = HALO - CONV_PAD
    rows = tt + 2 * HALO
    acc_cols = []
    for cb in range(CONV_CH // LANES):
        cols = slice(cb * LANES, (cb + 1) * LANES)
        window = uwin_ref[:, cols]
        acc_c = jnp.zeros((tt, LANES), F32) + dwb_ref[:, cols]
        for shift in range(F32_SUBLANES):
            taps = [k for k in range(CONV_K) if (first + k) % F32_SUBLANES == shift]
            if not taps:
                continue
            shifted = window if shift == 0 else pltpu.roll(window, rows - shift, 0)
            for k in taps:
                off = first + k - shift
                acc_c = acc_c + shifted[off:off + tt] * dw_ref[k:k + 1, cols]
        acc_cols.append(acc_c)
    acc = jnp.concatenate(acc_cols, axis=-1)
    mu = jnp.mean(acc, axis=-1, keepdims=True)
    cen = acc - mu
    var = jnp.mean(cen * cen, axis=-1, keepdims=True)
    ln = cen * lax.rsqrt(var + EPS) * lng_ref[...] + lnb_ref[...]
    feat_b = (ln * _sigmoid(ln)).astype(BF16)

    zwin_ref[0:HALO, :] = pz_p_ref[0].astype(F32) * has_prev
    zwin_ref[HALO:HALO + tt, :] = pz_ref[0].astype(F32)
    zwin_ref[HALO + tt:, :] = pz_n_ref[0].astype(F32) * has_next
    tpos = t * tt + lax.broadcasted_iota(I32, (tt, 1), 0)
    pooled = []
    for gi, w in enumerate(POOL_WINDOWS):
        cols = slice(gi * POOL_GROUP, (gi + 1) * POOL_GROUP)
        tot = zwin_ref[pl.ds(HALO - w // 2, tt), cols]
        for d in range(1 - w // 2, w - w // 2):
            tot = tot + zwin_ref[pl.ds(HALO + d, tt), cols]
        cnt = (jnp.minimum(tpos + (w - w // 2), seq) - jnp.maximum(tpos - w // 2, 0)).astype(F32)
        diff = tot / cnt - zwin_ref[pl.ds(HALO, tt), cols]
        pooled.append(_dot(diff.astype(BF16), wpool_ref[gi]))
    feat_c = (jnp.concatenate(pooled, axis=-1) * pscale_ref[...]).astype(BF16)

    y_b = _dot(feat_b, wb_ref[...])
    y_c = _dot(feat_c, wc_ref[...])
    merged = (_sigmoid(ga_ref[0].astype(F32)) * y_a + _sigmoid(gb_ref[0].astype(F32)) * y_b
              + _sigmoid(gc_ref[0].astype(F32)) * y_c)
    xn = x_ref[0] + g1_ref[0] * _dot(merged.astype(BF16), wo_ref[...])
    xo_ref[0] = xn

    h2 = _rms_mod(xn, n2g_ref[...], sh2_ref[0], sc2_ref[0])
    h2_hi = h2.astype(BF16)
    h2_ref[0] = h2_hi
    h2_lo = (h2 - h2_hi.astype(F32)).astype(BF16)
    ne = afft_ref.shape[1]
    by_hi = _dot_nt(wrt_ref[...], h2_hi)
    logits_t = by_hi[:ne] + by_hi[ne:] + _dot_nt(wrt_ref[:ne, :], h2_lo)
    et = jnp.exp(logits_t - jnp.max(logits_t, axis=0, keepdims=True))
    afft_ref[0] = et / jnp.sum(et, axis=0, keepdims=True)


def _merge(p, attn, x, g1, sh2, sc2, n2g, lw, tt):
    b, s, d = x.shape
    nh = tt // HALO
    last_h = s // HALO - 1
    e = N_EXPERTS

    def main(width, col):
        return pl.BlockSpec((1, tt, width), lambda i, j: (i, j, col))

    def prev(col):
        return pl.BlockSpec((1, HALO, CONV_CH), lambda i, j: (i, jnp.maximum(j * nh - 1, 0), col))

    def nxt(col):
        return pl.BlockSpec((1, HALO, CONV_CH), lambda i, j: (i, jnp.minimum((j + 1) * nh, last_h), col))

    def per_batch():
        return pl.BlockSpec((1, 1, d), lambda i, j: (i, 0, 0))

    def const(shape):
        return pl.BlockSpec(shape, lambda i, j: (0,) * len(shape))

    ca, cg, pz = COL_CONV_A // CONV_CH, COL_CONV_G // CONV_CH, COL_POOL // CONV_CH
    in_specs = [main(d, 0), main(d, 1), main(d, 2),
                main(CONV_CH, ca), prev(ca), nxt(ca), main(CONV_CH, cg), prev(cg), nxt(cg),
                main(POOL_CH, pz), prev(pz), nxt(pz),
                pl.BlockSpec((1, tt, Q_W), lambda i, j: (i, j, 0)),
                pl.BlockSpec((1, tt, d), lambda i, j: (i, j, 0)),
                per_batch(), per_batch(), per_batch(), const((1, d)),
                const((CONV_K, CONV_CH)), const((1, CONV_CH)), const((1, CONV_CH)), const((1, CONV_CH)),
                const((len(POOL_WINDOWS), POOL_GROUP, POOL_GROUP)), const((1, POOL_CH)),
                const((Q_W, d)), const((CONV_CH, d)), const((POOL_CH, d)), const((d, d)),
                const((2 * e, d))]
    out_specs = [pl.BlockSpec((1, tt, d), lambda i, j: (i, j, 0)),
                 pl.BlockSpec((1, tt, d), lambda i, j: (i, j, 0)),
                 pl.BlockSpec((1, e, tt), lambda i, j: (i, 0, j))]
    out_shape = [jax.ShapeDtypeStruct((b, s, d), F32), jax.ShapeDtypeStruct((b, s, d), BF16),
                 jax.ShapeDtypeStruct((b, e, s), F32)]
    return pl.pallas_call(
        functools.partial(_merge_kernel, seq=s),
        grid=(b, s // tt),
        in_specs=in_specs, out_specs=out_specs, out_shape=out_shape,
        scratch_shapes=[pltpu.VMEM((tt + 2 * HALO, CONV_CH), F32), pltpu.VMEM((tt + 2 * HALO, POOL_CH), F32)],
        name="mix_merge",
    )(p, p, p, p, p, p, p, p, p, p, p, p, attn, x, g1, sh2, sc2, n2g,
      lw['conv_dw'], lw['conv_dw_b'], lw['conv_ln_g'], lw['conv_ln_b'], lw['w_pool'], lw['pool_scale'],
      lw['w_attn_o'], lw['w_conv_o'], lw['w_pool_o'], lw['w_out'], lw['w_router_t'])


def _topk_kernel(afft_ref, slot_ref, slott_ref, offs_ref, *, cap, blk):
    a = afft_ref[0]
    e, s = a.shape
    def keeps_cap(cand):
        return jnp.sum((a >= cand).astype(F32), axis=-1, keepdims=True) >= cap

    tiny = jnp.full((e, 1), F32_TINY, F32)
    thr = jnp.where(keeps_cap(tiny), tiny, 0.0)
    for step in (64, 32, 16, 8, 4, 2, 1):
        cand = thr * float(2 ** step)
        thr = jnp.where(keeps_cap(cand), cand, thr)
    delta = thr
    for _ in range(F32_MANTISSA_BITS):
        delta = delta * 0.5
        cand = thr + delta
        thr = jnp.where(keeps_cap(cand), cand, thr)
    gt = a > thr
    eq = a == thr
    need = cap - jnp.sum(gt.astype(F32), axis=-1, keepdims=True)

    r = lax.broadcasted_iota(I32, (blk, blk), 0)
    c = lax.broadcasted_iota(I32, (blk, blk), 1)
    upper = (r < c).astype(BF16)
    eye = (r == c).astype(F32)

    def prefix(mask_f32):
        carry = jnp.zeros((e, 1), F32)
        parts = []
        for j in range(s // blk):
            m = mask_f32[:, j * blk:(j + 1) * blk]
            parts.append(_dot(m.astype(BF16), upper) + carry)
            carry = carry + jnp.sum(m, axis=-1, keepdims=True)
        return jnp.concatenate(parts, axis=-1)

    sel = gt | (eq & (prefix(eq.astype(F32)) < need))
    pos = prefix(sel.astype(F32))
    slot = jnp.where(sel, pos, -1.0)
    slot_ref[0] = slot.astype(I32)
    for j in range(s // blk):
        slott_ref[0, j * blk:(j + 1) * blk, :] = _dot_nt(eye, slot[:, j * blk:(j + 1) * blk],
                                                        precision=HIGHEST).astype(I32)
    tok = lax.broadcasted_iota(I32, (s, LANES), 0)
    col = lax.broadcasted_iota(I32, (s, LANES), 1)
    before = (tok < col * blk).astype(BF16)
    offs_ref[0] = _dot(sel.astype(BF16), before).astype(I32)


def _topk(afft, cap):
    b, e, s = afft.shape
    blk = min(s, TOKEN_CHUNK)
    slot, slott, offs = pl.pallas_call(
        functools.partial(_topk_kernel, cap=cap, blk=blk),
        grid=(b,),
        in_specs=[pl.BlockSpec((1, e, s), lambda i: (i, 0, 0))],
        out_specs=[pl.BlockSpec((1, e, s), lambda i: (i, 0, 0)),
                   pl.BlockSpec((1, s, e), lambda i: (i, 0, 0)),
                   pl.BlockSpec((1, e, LANES), lambda i: (i, 0, 0))],
        out_shape=[jax.ShapeDtypeStruct((b, e, s), I32), jax.ShapeDtypeStruct((b, s, e), I32),
                   jax.ShapeDtypeStruct((b, e, LANES), I32)],
        name="expert_choice_topk",
    )(afft)
    return slot, slott, offs[:, :, :s // blk + 1]


def _slot_windows(offs_ref, idx, win):
    lo = offs_ref[idx]
    hi = offs_ref[idx + 1]
    first = lo // win
    return first, jnp.where(hi > lo, (hi - 1) // win - first + 1, 0)


def _gather_rows(onehot, h, gate_row):
    picked = _dot(onehot.astype(BF16), h)
    gates = jnp.sum(jnp.where(onehot, gate_row, 0.0), axis=-1, keepdims=True)
    return picked, gates


def _gather_kernel(offs_ref, slot_ref, afft_ref, h_ref, xe_ref, g_ref, xe_s, g_s, *, chunk, group, win):
    cap = xe_s.shape[0]
    nch = h_ref.shape[1] // chunk
    base = (pl.program_id(0) * pl.num_programs(1) + pl.program_id(1)) * (nch + 1)
    xe_s[...] = jnp.zeros_like(xe_s)
    g_s[...] = jnp.zeros_like(g_s)
    span = group * chunk
    starts = []
    fits = None
    for p in range(nch // group):
        lo = offs_ref[base + p * group]
        hi = offs_ref[base + (p + 1) * group]
        a = pl.multiple_of(jnp.minimum((lo // F32_SUBLANES) * F32_SUBLANES, cap - win), F32_SUBLANES)
        starts.append(a)
        fits = (hi - a <= win) if fits is None else jnp.logical_and(fits, hi - a <= win)

    def add_window(a, sl, width):
        onehot = (slot_ref[0, :, sl] - a) == lax.broadcasted_iota(I32, (win, width), 0)
        picked, gates = _gather_rows(onehot, h_ref[0, sl, :], afft_ref[0, :, sl])
        xe_s[pl.ds(a, win), :] += picked
        g_s[pl.ds(a, win), :] += gates

    @pl.when(fits)
    def _():
        for p, a in enumerate(starts):
            add_window(a, slice(p * span, (p + 1) * span), span)

    @pl.when(jnp.logical_not(fits))
    def _():
        def per_chunk(j, carry):
            first, nwin = _slot_windows(offs_ref, base + j, win)
            sl = pl.ds(pl.multiple_of(j * chunk, chunk), chunk)

            def window(w, c):
                add_window(pl.multiple_of((first + w) * win, win), sl, chunk)
                return c
            lax.fori_loop(0, nwin, window, 0)
            return carry
        lax.fori_loop(0, nch, per_chunk, 0)

    xe_ref[0] = xe_s[...].astype(xe_ref.dtype)
    g_ref[0] = g_s[...]


def _expert_gather(slot, offs, afft, h, cap):
    nb, e, s = slot.shape
    d = h.shape[2]
    chunk = min(s, TOKEN_CHUNK)
    nch = s // chunk
    group = 2 if nch % 2 == 0 else 1
    win = min(cap, GATHER_WINDOW)
    assert cap % win == 0
    row_spec = pl.BlockSpec((1, 1, s), lambda i, j, o: (i * e + j, 0, 0))
    return pl.pallas_call(
        functools.partial(_gather_kernel, chunk=chunk, group=group, win=win),
        grid_spec=pltpu.PrefetchScalarGridSpec(
            num_scalar_prefetch=1, grid=(nb, e),
            in_specs=[row_spec, row_spec, pl.BlockSpec((1, s, d), lambda i, j, o: (i, 0, 0))],
            out_specs=[pl.BlockSpec((1, cap, d), lambda i, j, o: (i * e + j, 0, 0)),
                       pl.BlockSpec((1, cap, 1), lambda i, j, o: (i * e + j, 0, 0))],
            scratch_shapes=[pltpu.VMEM((cap, d), F32), pltpu.VMEM((cap, 1), F32)]),
        out_shape=[jax.ShapeDtypeStruct((nb * e, cap, d), BF16), jax.ShapeDtypeStruct((nb * e, cap, 1), F32)],
        name="expert_gather",
    )(offs.reshape(-1), slot.reshape(nb * e, 1, s), afft.reshape(nb * e, 1, s), h)


def _ffn_kernel(xe_ref, g_ref, xc_ref, gc_ref, wg_ref, wu_ref, wd_ref, ye_ref, yec_ref, wg_s, wu_s, wd_s, *, rows):
    bi = pl.program_id(1)

    @pl.when(bi == 0)
    def _():
        def cast(i, carry):
            sl = pl.ds(pl.multiple_of(i * rows, rows), rows)
            wg_s[sl, :] = wg_ref[0, 0, sl, :].astype(BF16)
            wu_s[sl, :] = wu_ref[0, 0, sl, :].astype(BF16)
            wd_s[sl, :] = wd_ref[0, 0, sl, :].astype(BF16)
            return carry
        lax.fori_loop(0, wg_s.shape[0] // rows, cast, 0)

    def ffn(xe, gates):
        a = _dot(xe, wg_s[...])
        u = _dot(xe, wu_s[...])
        hid = (a * _sigmoid(a) * u).astype(BF16)
        return _dot(hid, wd_s[...]) * gates

    ye_ref[0] = ffn(xe_ref[0], g_ref[0]).astype(ye_ref.dtype)
    if yec_ref is not None:
        @pl.when(bi == pl.num_programs(1) - 1)
        def _():
            yec_ref[0] = ffn(xc_ref[0], gc_ref[0]).astype(yec_ref.dtype)


def _expert_ffn(layer, xe, g, wg, wu, wd, nb, ctx_part=None):
    e = xe.shape[0] // nb
    cap, d = xe.shape[1:]
    f = wg.shape[3]
    assert f == d
    w_spec = pl.BlockSpec((1, 1, d, f), lambda j, i: (layer, j, 0, 0))
    in_specs = [pl.BlockSpec((1, cap, d), lambda j, i: (i * e + j, 0, 0)),
                pl.BlockSpec((1, cap, 1), lambda j, i: (i * e + j, 0, 0))]
    out_specs = [pl.BlockSpec((1, cap, d), lambda j, i: (i * e + j, 0, 0))]
    out_shape = [jax.ShapeDtypeStruct((nb * e, cap, d), BF16)]
    args = [xe, g]
    body = functools.partial(_ffn_kernel, rows=128)
    if ctx_part is None:
        def kern(xe_ref, g_ref, wg_ref, wu_ref, wd_ref, ye_ref, *scratch):
            body(xe_ref, g_ref, None, None, wg_ref, wu_ref, wd_ref, ye_ref, None, *scratch)
    else:
        xc, gc = ctx_part
        rows_c = xc.shape[1]
        in_specs += [pl.BlockSpec((1, rows_c, d), lambda j, i: (j, 0, 0)),
                     pl.BlockSpec((1, rows_c, 1), lambda j, i: (j, 0, 0))]
        out_specs.append(pl.BlockSpec((1, rows_c, d), lambda j, i: (j, 0, 0)))
        out_shape.append(jax.ShapeDtypeStruct((e, rows_c, d), BF16))
        args += [xc, gc]
        kern = body
    outs = pl.pallas_call(
        kern,
        grid=(e, nb),
        in_specs=in_specs + [w_spec, w_spec, w_spec], out_specs=out_specs,
        scratch_shapes=[pltpu.VMEM((d, f), BF16), pltpu.VMEM((d, f), BF16), pltpu.VMEM((f, d), BF16)],
        out_shape=out_shape,
        name="expert_ffn",
    )(*args, wg, wu, wd)
    ye = outs[0].reshape(nb, e * cap, d)
    return ye if ctx_part is None else (ye, outs[1])


def _combine_kernel(offs_ref, slott_ref, ye_ref, x_ref, g2_ref, fg_ref, o_ref, acc_s, *, cap, win, final_norm):
    bi = pl.program_id(0)
    j = pl.program_id(1)
    nch = pl.num_programs(1)
    tt = x_ref.shape[1]
    per_block = LANES // win

    starts = []
    fits = None
    for ex in range(N_EXPERTS):
        idx = (bi * N_EXPERTS + ex) * (nch + 1) + j
        a = pl.multiple_of(jnp.minimum((offs_ref[idx] // BF16_SUBLANES) * BF16_SUBLANES, cap - win), BF16_SUBLANES)
        starts.append(a)
        ok = offs_ref[idx + 1] - a <= win
        fits = ok if fits is None else jnp.logical_and(fits, ok)

    @pl.when(fits)
    def _():
        lane = lax.broadcasted_iota(I32, (tt, LANES), 1)
        blocks = []
        for blk in range(N_EXPERTS // per_block):
            target = None
            for q in range(per_block):
                ex = blk * per_block + q
                t = slott_ref[0, :, ex:ex + 1] - starts[ex] + q * win
                target = t if target is None else jnp.where(lane >= q * win, t, target)
            blocks.append((target == lane).astype(BF16))
        rows = [ye_ref[0, pl.ds(ex * cap + starts[ex], win), :] for ex in range(N_EXPERTS)]
        acc_s[...] = _dot(jnp.concatenate(blocks, axis=1), jnp.concatenate(rows, axis=0))

    @pl.when(jnp.logical_not(fits))
    def _():
        lane = lax.broadcasted_iota(I32, (tt, win), 1)
        acc_s[...] = jnp.zeros_like(acc_s)
        for ex in range(N_EXPERTS):
            first, nwin = _slot_windows(offs_ref, (bi * N_EXPERTS + ex) * (nch + 1) + j, win)

            def window(w, c, ex=ex, first=first):
                a = pl.multiple_of((first + w) * win, win)
                onehot = ((slott_ref[0, :, ex:ex + 1] - a) == lane).astype(BF16)
                acc_s[...] += _dot(onehot, ye_ref[0, pl.ds(ex * cap + a, win), :])
                return c
            lax.fori_loop(0, nwin, window, 0)

    out = x_ref[0] + g2_ref[0] * acc_s[...]
    if final_norm:
        out = out * lax.rsqrt(jnp.mean(out * out, axis=-1, keepdims=True) + EPS) * fg_ref[...]
    o_ref[0] = out


def _combine(slott, offs, ye, x, g2, fg, cap, final_norm):
    b, s, d = x.shape
    e = N_EXPERTS
    tt = min(s, TOKEN_CHUNK)
    win = min(cap, SCATTER_WINDOW)
    assert cap % win == 0 and LANES % win == 0 and e % (LANES // win) == 0
    return pl.pallas_call(
        functools.partial(_combine_kernel, cap=cap, win=win, final_norm=final_norm),
        grid_spec=pltpu.PrefetchScalarGridSpec(
            num_scalar_prefetch=1, grid=(b, s // tt),
            in_specs=[pl.BlockSpec((1, tt, e), lambda i, j, o: (i, j, 0)),
                      pl.BlockSpec((1, e * cap, d), lambda i, j, o: (i, 0, 0)),
                      pl.BlockSpec((1, tt, d), lambda i, j, o: (i, j, 0)),
                      pl.BlockSpec((1, 1, d), lambda i, j, o: (i, 0, 0)),
                      pl.BlockSpec((1, d), lambda i, j, o: (0, 0))],
            out_specs=pl.BlockSpec((1, tt, d), lambda i, j, o: (i, j, 0)),
            scratch_shapes=[pltpu.VMEM((tt, d), F32)]),
        out_shape=jax.ShapeDtypeStruct((b, s, d), F32),
        name="moe_combine",
    )(offs.reshape(-1), slott, ye, x, g2, fg)


def _split_hi_lo(w):
    hi = w.astype(BF16)
    return jnp.concatenate([hi, (w - hi.astype(F32)).astype(BF16)], axis=0)


def _permute_in_cols(w):
    o_k = Q_W
    o_v = o_k + KV_W
    o_ca = o_v + KV_W
    o_cg = o_ca + CONV_CH
    o_p = o_cg + CONV_CH
    o_g = o_p + POOL_CH
    return jnp.concatenate([w[:, o_g:], w[:, :o_k], w[:, o_ca:o_cg], w[:, o_cg:o_p], w[:, o_p:o_g],
                            w[:, o_k:o_v], w[:, o_v:o_ca]], axis=1)


def kernel(x, c, ctx, c_ctx, norm1_g, norm2_g, w_mod, b_mod, w_in, attn_sink, w_attn_o, conv_dw, conv_dw_b,
           conv_ln_g, conv_ln_b, w_conv_o, w_pool, pool_scale, w_pool_o, w_out, w_router, w_e_gate, w_e_up,
           w_e_down, final_norm_g):
    b, s, d = x.shape
    l = ctx.shape[1]
    depth = w_in.shape[0]
    assert d == _D_MODEL and w_in.shape[2] == IN_W

    tabs = _rope_tables(s)
    cc = jnp.zeros((8, d), F32).at[:b].set(c).at[b].set(c_ctx)
    mod = _modulation(cc, w_mod, b_mod)
    fg = final_norm_g.reshape(1, d)

    for layer in range(depth):
        last = layer == depth - 1
        mx = mod[layer, :b].reshape(b, 1, 6, d)
        sh1, sc1, g1, sh2, sc2, g2 = [mx[:, :, i] for i in range(6)]
        mc = jnp.broadcast_to(mod[layer, b].reshape(1, 1, 6, d), (b, 1, 6, d))
        csh1, csc1, cg1, csh2, csc2, cg2 = [mc[:, :, i] for i in range(6)]
        n1g = norm1_g[layer].reshape(1, d)
        n2g = norm2_g[layer].reshape(1, d)
        w_in_l = _permute_in_cols(w_in[layer]).astype(BF16)
        lw = {'conv_dw': conv_dw[layer], 'conv_dw_b': conv_dw_b[layer].reshape(1, -1),
              'conv_ln_g': conv_ln_g[layer].reshape(1, -1), 'conv_ln_b': conv_ln_b[layer].reshape(1, -1),
              'w_pool': w_pool[layer].astype(BF16), 'pool_scale': pool_scale[layer].reshape(1, -1),
              'w_attn_o': w_attn_o[layer].astype(BF16), 'w_conv_o': w_conv_o[layer].astype(BF16),
              'w_pool_o': w_pool_o[layer].astype(BF16), 'w_out': w_out[layer].astype(BF16),
              'w_router_t': _split_hi_lo(w_router[layer].T)}

        p_x = _inproj(x, n1g, sh1, sc1, w_in_l, tm=512)
        p_c = _inproj(ctx, n1g, csh1, csc1, w_in_l, tm=l)
        attn_x = _window_attention(p_x, p_c, attn_sink[layer], tabs)
        x_mid, h2, afft = _merge(p_x, attn_x, x, g1, sh2, sc2, n2g, lw, tt=256)
        if not last:
            attn_c = _context_attention(p_c, attn_sink[layer])
            c_mid, ch2, cafft = _merge(p_c, attn_c, ctx, cg1, csh2, csc2, n2g, lw, tt=l)
        cap = (CAPACITY_FACTOR * s) // N_EXPERTS
        slot, slott, offs = _topk(afft, cap)
        xe, ge = _expert_gather(slot, offs, afft, h2, cap)
        if last:
            ye = _expert_ffn(layer, xe, ge, w_e_gate, w_e_up, w_e_down, b)
        else:
            assert l == TOKEN_CHUNK
            cap_c = (CAPACITY_FACTOR * l) // N_EXPERTS
            cslot, cslott, coffs = _topk(cafft, cap_c)
            sample_base = (jnp.arange(b, dtype=I32) * cap_c)[:, None, None]
            cslot_all = jnp.where(cslot >= 0, cslot + sample_base, -1)
            cslot_all = cslot_all.transpose(1, 0, 2).reshape(1, N_EXPERTS, b * l)
            cafft_all = cafft.transpose(1, 0, 2).reshape(1, N_EXPERTS, b * l)
            coffs_all = jnp.broadcast_to(jnp.arange(b + 1, dtype=I32) * cap_c, (1, N_EXPERTS, b + 1))
            xc, gc = _expert_gather(cslot_all, coffs_all, cafft_all, ch2.reshape(1, b * l, d), b * cap_c)
            ye, yec = _expert_ffn(layer, xe, ge, w_e_gate, w_e_up, w_e_down, b, ctx_part=(xc, gc))
            yec = yec.reshape(N_EXPERTS, b, cap_c, d).transpose(1, 0, 2, 3).reshape(b, N_EXPERTS * cap_c, d)
            ctx = _combine(cslott, coffs, yec, c_mid, cg2, fg, cap_c, False)
        x = _combine(slott, offs, ye, x_mid, g2, fg, cap, last)
    return x
```

```python
import functools

import jax
import jax.numpy as jnp
import numpy as np
from jax import lax
from jax.experimental import pallas as pl
from jax.experimental.pallas import tpu as pltpu

F32 = jnp.float32
BF16 = jnp.bfloat16
I32 = jnp.int32

EPS = 1e-6
GRID_W = 64
N_HEADS = 8
N_KV_HEADS = 2
HEAD_DIM = 64
GQA_GROUP = N_HEADS // N_KV_HEADS
WINDOW = 128
ATTN_BLOCK = 128
ROPE_BASE = 10000.0
ROPE_PAIRS = HEAD_DIM // 4
CONV_CH = 512
CONV_K = 31
CONV_PAD = CONV_K // 2
POOL_WINDOWS = (2, 4, 8, 16)
POOL_GROUP = 128
POOL_CH = POOL_GROUP * len(POOL_WINDOWS)
N_EXPERTS = 16
CAPACITY_FACTOR = 2
Q_W = N_HEADS * HEAD_DIM
KV_W = N_KV_HEADS * HEAD_DIM

LANES = 128
HALO = 16
HIGHEST = lax.Precision.HIGHEST
F32_TINY = 2.0 ** -126
F32_MANTISSA_BITS = 23
TOKEN_CHUNK = 256
GATHER_WINDOW = 128
SCATTER_WINDOW = 64
F32_SUBLANES = 8
BF16_SUBLANES = 16
ATTN_LOOKAHEAD = 3
COMBINE_CHUNKS = 2

_D_MODEL = 1024
N_GATE_COLS = 3 * _D_MODEL
WCOL_Q = N_GATE_COLS
WCOL_CONV_A = WCOL_Q + Q_W
WCOL_CONV_G = WCOL_CONV_A + CONV_CH
WCOL_POOL = WCOL_CONV_G + CONV_CH
WCOL_K = WCOL_POOL + POOL_CH
IN_W = WCOL_K + 2 * KV_W
COL_Q = N_GATE_COLS
COL_CONV_U = COL_Q + Q_W
COL_POOL = COL_CONV_U + CONV_CH
COL_K = COL_POOL + POOL_CH
COL_V = COL_K + KV_W
OUT_W = COL_V + KV_W
GATE_CHUNK = 768


def _dot(a, b):
    return jnp.dot(a, b, preferred_element_type=F32)


def _dot_nt(a, b, precision=None):
    return lax.dot_general(a, b, (((1,), (1,)), ((), ())), preferred_element_type=F32, precision=precision)


def _sigmoid(v):
    return 0.5 * jnp.tanh(0.5 * v) + 0.5


def _rms_mod(x, g, sh, sc):
    y = x * lax.rsqrt(jnp.mean(x * x, axis=-1, keepdims=True) + EPS) * g
    return y * (1.0 + sc) + sh


def _mod_kernel(c_ref, w_ref, b_ref, o_ref):
    c = c_ref[...]
    a = c * _sigmoid(c)
    o_ref[0] = jnp.dot(a, w_ref[0], preferred_element_type=F32, precision=HIGHEST) + b_ref[0]


def _modulation(cc, w_mod, b_mod, tn=1536):
    depth, d, n = w_mod.shape
    rows = cc.shape[0]
    return pl.pallas_call(
        _mod_kernel,
        grid=(depth, n // tn),
        in_specs=[pl.BlockSpec((rows, d), lambda l, j: (0, 0)),
                  pl.BlockSpec((1, d, tn), lambda l, j: (l, 0, j)),
                  pl.BlockSpec((1, 1, tn), lambda l, j: (l, 0, j))],
        out_specs=pl.BlockSpec((1, rows, tn), lambda l, j: (l, 0, j)),
        out_shape=jax.ShapeDtypeStruct((depth, rows, n), F32),
        name="modulation",
    )(cc, w_mod, b_mod.reshape(depth, 1, n))


def _inproj_kernel(x_ref, g_ref, sh_ref, sc_ref, w_ref, o_ref, *, mixer_epilogue):
    h = _rms_mod(x_ref[0], g_ref[...], sh_ref[0], sc_ref[0]).astype(BF16)
    if not mixer_epilogue:
        o_ref[0] = _dot(h, w_ref[...]).astype(o_ref.dtype)
        return
    for c0 in range(0, N_GATE_COLS, GATE_CHUNK):
        cols = slice(c0, c0 + GATE_CHUNK)
        o_ref[0, :, cols] = _sigmoid(_dot(h, w_ref[:, cols])).astype(o_ref.dtype)
    o_ref[0, :, COL_Q:COL_CONV_U] = _dot(h, w_ref[:, WCOL_Q:WCOL_CONV_A]).astype(o_ref.dtype)
    glu = _dot(h, w_ref[:, WCOL_CONV_A:WCOL_POOL])
    o_ref[0, :, COL_CONV_U:COL_POOL] = (glu[:, :CONV_CH] * _sigmoid(glu[:, CONV_CH:])).astype(o_ref.dtype)
    o_ref[0, :, COL_POOL:] = _dot(h, w_ref[:, WCOL_POOL:]).astype(o_ref.dtype)


def _inproj(x, g, sh, sc, w, tm, mixer_epilogue=True):
    b, s, d = x.shape
    n = w.shape[1]
    n_out = OUT_W if mixer_epilogue else n
    assert not mixer_epilogue or n == IN_W
    return pl.pallas_call(
        functools.partial(_inproj_kernel, mixer_epilogue=mixer_epilogue),
        grid=(b, s // tm),
        in_specs=[pl.BlockSpec((1, tm, d), lambda i, j: (i, j, 0)),
                  pl.BlockSpec((1, d), lambda i, j: (0, 0)),
                  pl.BlockSpec((1, 1, d), lambda i, j: (i, 0, 0)),
                  pl.BlockSpec((1, 1, d), lambda i, j: (i, 0, 0)),
                  pl.BlockSpec((d, n), lambda i, j: (0, 0))],
        out_specs=pl.BlockSpec((1, tm, n_out), lambda i, j: (i, j, 0)),
        out_shape=jax.ShapeDtypeStruct((b, s, n_out), BF16),
        name="inproj",
    )(x, g, sh, sc, w)


def _rope_tables(s):
    t = np.arange(s)
    row = (t // GRID_W).astype(np.float32)
    col = (t % GRID_W).astype(np.float32)
    freqs = jnp.asarray(ROPE_BASE, F32) ** (-jnp.arange(ROPE_PAIRS, dtype=F32) / ROPE_PAIRS)
    ang_r = jnp.asarray(row)[:, None] * freqs
    ang_c = jnp.asarray(col)[:, None] * freqs
    cos_h = jnp.concatenate([jnp.cos(ang_r), jnp.cos(ang_r), jnp.cos(ang_c), jnp.cos(ang_c)], axis=-1)
    sin_h = jnp.concatenate([-jnp.sin(ang_r), jnp.sin(ang_r), -jnp.sin(ang_c), jnp.sin(ang_c)], axis=-1)
    return jnp.tile(cos_h, (1, LANES // HEAD_DIM)), jnp.tile(sin_h, (1, LANES // HEAD_DIM))


def _rope(x, cos, sin_signed):
    lane = lax.broadcasted_iota(I32, x.shape, 1)
    low = (lane & (2 * ROPE_PAIRS - 1)) < ROPE_PAIRS
    partner = jnp.where(low, pltpu.roll(x, LANES - ROPE_PAIRS, 1), pltpu.roll(x, ROPE_PAIRS, 1))
    return x * cos + partner * sin_signed


def _softmax_pv(s_list, v_list, sink):
    m = sink
    for s in s_list:
        m = jnp.maximum(m, jnp.max(s, axis=-1, keepdims=True))
    denom = jnp.exp(sink - m)
    o = None
    for s, v in zip(s_list, v_list):
        e = jnp.exp(s - m)
        denom = denom + jnp.sum(e, axis=-1, keepdims=True)
        pv = _dot(e.astype(BF16), v)
        o = pv if o is None else o + pv
    return o / denom


def _lane_lo(shape):
    return lax.broadcasted_iota(I32, shape, 1) < HEAD_DIM


def _dup_heads(t):
    swapped = pltpu.roll(t, HEAD_DIM, 1)
    lo = _lane_lo(t.shape)
    return jnp.where(lo, t, swapped), jnp.where(lo, swapped, t)


def _heads_attention(qps, keys, vals, masks, sinks):
    lo = _lane_lo(qps[0].shape)
    keeps = (lo, jnp.logical_not(lo))
    tiles_per_kv = len(qps) // len(keys)

    def head_scores(h):
        qp = qps[h // 2]
        qh = jnp.where(keeps[h % 2], qp, jnp.zeros_like(qp))
        return [sc if mask is None else jnp.where(mask, sc, -1e30)
                for sc, mask in zip([_dot_nt(qh, k) for k in keys[h // 2 // tiles_per_kv]], masks)]

    n_heads = 2 * len(qps)
    outs = []
    pending = [head_scores(h) for h in range(min(ATTN_LOOKAHEAD, n_heads))]
    for h in range(n_heads):
        if h + ATTN_LOOKAHEAD < n_heads:
            pending.append(head_scores(h + ATTN_LOOKAHEAD))
        outs.append(_softmax_pv(pending.pop(0), vals[h // 2 // tiles_per_kv], sinks[h]))
    return [jnp.where(lo, outs[2 * i], outs[2 * i + 1]) for i in range(len(qps))]


def _store_dup(dst_ref, t):
    d0, d1 = _dup_heads(t)
    dst_ref[0] = d0.astype(dst_ref.dtype)
    dst_ref[1] = d1.astype(dst_ref.dtype)


def _win_attn_kernel(sink_ref, q_ref, k_ref, v_ref, kc_ref, vc_ref, cosq_ref, sinq_ref, cosk_ref, sink_tab_ref,
                     o_ref, kd_ref, vd_ref, kcd_ref, vcd_ref, *, seq):
    i = pl.program_id(1)
    blk = ATTN_BLOCK
    win = 3 * blk

    @pl.when(i == 0)
    def _():
        _store_dup(kd_ref, _rope(k_ref[0].astype(F32), cosk_ref[...], sink_tab_ref[...]))
        _store_dup(vd_ref, v_ref[0].astype(F32))
        _store_dup(kcd_ref, kc_ref[0].astype(F32))
        _store_dup(vcd_ref, vc_ref[0].astype(F32))

    start = pl.multiple_of(jnp.clip((i - 1) * blk, 0, seq - win), blk)
    qpos = i * blk + lax.broadcasted_iota(I32, (blk, win), 0)
    kpos = start + lax.broadcasted_iota(I32, (blk, win), 1)
    mask = jnp.abs(kpos - qpos) <= WINDOW
    cos = cosq_ref[...]
    sin = sinq_ref[...]
    scale = HEAD_DIM ** -0.5
    qps = [(_rope(q_ref[0, :, p * LANES:(p + 1) * LANES].astype(F32), cos, sin) * scale).astype(BF16)
           for p in range(N_HEADS // 2)]
    keys = [[kd_ref[kh, pl.ds(start, win), :], kcd_ref[kh]] for kh in range(N_KV_HEADS)]
    vals = [[vd_ref[kh, pl.ds(start, win), :], vcd_ref[kh]] for kh in range(N_KV_HEADS)]
    sinks = [sink_ref[h] for h in range(N_HEADS)]
    for p, o in enumerate(_heads_attention(qps, keys, vals, [mask, None], sinks)):
        o_ref[0, :, p * LANES:(p + 1) * LANES] = o.astype(o_ref.dtype)


def _window_attention(p_x, p_c, ctx_kv_cols, sink, tabs):
    b, s, _ = p_x.shape
    l = p_c.shape[1]
    cos, sin = tabs
    blk = ATTN_BLOCK
    kcol, vcol = COL_K // KV_W, COL_V // KV_W
    kccol, vccol = ctx_kv_cols[0] // KV_W, ctx_kv_cols[1] // KV_W
    return pl.pallas_call(
        functools.partial(_win_attn_kernel, seq=s),
        grid=(b, s // blk),
        in_specs=[pl.BlockSpec(memory_space=pltpu.SMEM),
                  pl.BlockSpec((1, blk, Q_W), lambda i, j: (i, j, COL_Q // Q_W)),
                  pl.BlockSpec((1, s, KV_W), lambda i, j: (i, 0, kcol)),
                  pl.BlockSpec((1, s, KV_W), lambda i, j: (i, 0, vcol)),
                  pl.BlockSpec((1, l, KV_W), lambda i, j: (i, 0, kccol)),
                  pl.BlockSpec((1, l, KV_W), lambda i, j: (i, 0, vccol)),
                  pl.BlockSpec((blk, LANES), lambda i, j: (j, 0)),
                  pl.BlockSpec((blk, LANES), lambda i, j: (j, 0)),
                  pl.BlockSpec((s, LANES), lambda i, j: (0, 0)),
                  pl.BlockSpec((s, LANES), lambda i, j: (0, 0))],
        out_specs=pl.BlockSpec((1, blk, Q_W), lambda i, j: (i, j, 0)),
        out_shape=jax.ShapeDtypeStruct((b, s, Q_W), BF16),
        scratch_shapes=[pltpu.VMEM((N_KV_HEADS, s, LANES), BF16), pltpu.VMEM((N_KV_HEADS, s, LANES), BF16),
                        pltpu.VMEM((N_KV_HEADS, l, LANES), BF16), pltpu.VMEM((N_KV_HEADS, l, LANES), BF16)],
        name="window_attention",
    )(sink, p_x, p_x, p_x, p_c, p_c, cos, sin, cos, sin)


def _ctx_attn_kernel(sink_ref, q_ref, k_ref, v_ref, o_ref):
    kd = [t.astype(BF16) for t in _dup_heads(k_ref[0].astype(F32))]
    vd = [t.astype(BF16) for t in _dup_heads(v_ref[0].astype(F32))]
    scale = HEAD_DIM ** -0.5
    qps = [(q_ref[0, :, p * LANES:(p + 1) * LANES].astype(F32) * scale).astype(BF16) for p in range(N_HEADS // 2)]
    sinks = [sink_ref[h] for h in range(N_HEADS)]
    outs = _heads_attention(qps, [[k] for k in kd], [[v] for v in vd], [None], sinks)
    for p, o in enumerate(outs):
        o_ref[0, :, p * LANES:(p + 1) * LANES] = o.astype(o_ref.dtype)


def _context_attention(p_c, sink):
    b, l, _ = p_c.shape
    return pl.pallas_call(
        _ctx_attn_kernel,
        grid=(b,),
        in_specs=[pl.BlockSpec(memory_space=pltpu.SMEM),
                  pl.BlockSpec((1, l, Q_W), lambda i: (i, 0, COL_Q // Q_W)),
                  pl.BlockSpec((1, l, KV_W), lambda i: (i, 0, COL_K // KV_W)),
                  pl.BlockSpec((1, l, KV_W), lambda i: (i, 0, COL_V // KV_W))],
        out_specs=pl.BlockSpec((1, l, Q_W), lambda i: (i, 0, 0)),
        out_shape=jax.ShapeDtypeStruct((b, l, Q_W), BF16),
        name="context_attention",
    )(sink, p_c, p_c, p_c)


def _merge_kernel(ga_ref, gb_ref, gc_ref,
                  u_ref, u_p_ref, u_n_ref, pz_ref, pz_p_ref, pz_n_ref,
                  attn_ref, x_ref, g1_ref, sh2_ref, sc2_ref, n2g_ref,
                  dw_ref, dwb_ref, lng_ref, lnb_ref, wpool_ref, pscale_ref,
                  wa_ref, wb_ref, wc_ref, wo_ref, wrt_ref,
                  xo_ref, h2_ref, afft_ref,
                  uwin_ref, zwin_ref, *, seq):
    t = pl.program_id(1)
    tt = x_ref.shape[1]
    has_prev = (t > 0).astype(F32)
    has_next = (t < pl.num_programs(1) - 1).astype(F32)

    uwin_ref[0:HALO, :] = u_p_ref[0].astype(F32) * has_prev
    uwin_ref[HALO:HALO + tt, :] = u_ref[0].astype(F32)
    uwin_ref[HALO + tt:, :] = u_n_ref[0].astype(F32) * has_next
    first = HALO - CONV_PAD
    rows = tt + 2 * HALO
    acc_cols = []
    for cb in range(CONV_CH // LANES):
        cols = slice(cb * LANES, (cb + 1) * LANES)
        window = uwin_ref[:, cols]
        acc_c = jnp.zeros((tt, LANES), F32) + dwb_ref[:, cols]
        for shift in range(F32_SUBLANES):
            taps = [k for k in range(CONV_K) if (first + k) % F32_SUBLANES == shift]
            if not taps:
                continue
            shifted = window if shift == 0 else pltpu.roll(window, rows - shift, 0)
            for k in taps:
                off = first + k - shift
                acc_c = acc_c + shifted[off:off + tt] * dw_ref[k:k + 1, cols]
        acc_cols.append(acc_c)
    acc = jnp.concatenate(acc_cols, axis=-1)
    mu = jnp.mean(acc, axis=-1, keepdims=True)
    cen = acc - mu
    var = jnp.mean(cen * cen, axis=-1, keepdims=True)
    ln = cen * lax.rsqrt(var + EPS) * lng_ref[...] + lnb_ref[...]
    feat_b = (ln * _sigmoid(ln)).astype(BF16)

    zwin_ref[0:HALO, :] = pz_p_ref[0].astype(F32) * has_prev
    zwin_ref[HALO:HALO + tt, :] = pz_ref[0].astype(F32)
    zwin_ref[HALO + tt:, :] = pz_n_ref[0].astype(F32) * has_next
    tpos = t * tt + lax.broadcasted_iota(I32, (tt, 1), 0)
    pooled = []
    for gi, w in enumerate(POOL_WINDOWS):
        cols = slice(gi * POOL_GROUP, (gi + 1) * POOL_GROUP)
        tot = zwin_ref[pl.ds(HALO - w // 2, tt), cols]
        for d in range(1 - w // 2, w - w // 2):
            tot = tot + zwin_ref[pl.ds(HALO + d, tt), cols]
        cnt = (jnp.minimum(tpos + (w - w // 2), seq) - jnp.maximum(tpos - w // 2, 0)).astype(F32)
        diff = tot / cnt - zwin_ref[pl.ds(HALO, tt), cols]
        pooled.append(_dot(diff.astype(BF16), wpool_ref[gi]))
    feat_c = (jnp.concatenate(pooled, axis=-1) * pscale_ref[...]).astype(BF16)

    y_a = _dot(attn_ref[0], wa_ref[...])
    y_b = _dot(feat_b, wb_ref[...])
    y_c = _dot(feat_c, wc_ref[...])
    merged = ga_ref[0].astype(F32) * y_a + gb_ref[0].astype(F32) * y_b + gc_ref[0].astype(F32) * y_c
    xn = x_ref[0] + g1_ref[0] * _dot(merged.astype(BF16), wo_ref[...])
    xo_ref[0] = xn

    h2 = _rms_mod(xn, n2g_ref[...], sh2_ref[0], sc2_ref[0])
    h2_hi = h2.astype(BF16)
    h2_ref[0] = h2_hi
    h2_lo = (h2 - h2_hi.astype(F32)).astype(BF16)
    ne = afft_ref.shape[1]
    by_hi = _dot_nt(wrt_ref[...], h2_hi)
    logits_t = by_hi[:ne] + by_hi[ne:] + _dot_nt(wrt_ref[:ne, :], h2_lo)
    et = jnp.exp(logits_t - jnp.max(logits_t, axis=0, keepdims=True))
    afft_ref[0] = et / jnp.sum(et, axis=0, keepdims=True)


def _merge(p, attn, x, g1, sh2, sc2, n2g, lw, tt):
    b, s, d = x.shape
    nh = tt // HALO
    last_h = s // HALO - 1
    e = N_EXPERTS

    def main(width, col):
        return pl.BlockSpec((1, tt, width), lambda i, j: (i, j, col))

    def prev(col):
        return pl.BlockSpec((1, HALO, CONV_CH), lambda i, j: (i, jnp.maximum(j * nh - 1, 0), col))

    def nxt(col):
        return pl.BlockSpec((1, HALO, CONV_CH), lambda i, j: (i, jnp.minimum((j + 1) * nh, last_h), col))

    def per_batch():
        return pl.BlockSpec((1, 1, d), lambda i, j: (i, 0, 0))

    def const(shape):
        return pl.BlockSpec(shape, lambda i, j: (0,) * len(shape))

    cu, pz = COL_CONV_U // CONV_CH, COL_POOL // CONV_CH
    in_specs = [main(d, 0), main(d, 1), main(d, 2),
                main(CONV_CH, cu), prev(cu), nxt(cu), main(POOL_CH, pz), prev(pz), nxt(pz),
                pl.BlockSpec((1, tt, Q_W), lambda i, j: (i, j, 0)),
                pl.BlockSpec((1, tt, d), lambda i, j: (i, j, 0)),
                per_batch(), per_batch(), per_batch(), const((1, d)),
                const((CONV_K, CONV_CH)), const((1, CONV_CH)), const((1, CONV_CH)), const((1, CONV_CH)),
                const((len(POOL_WINDOWS), POOL_GROUP, POOL_GROUP)), const((1, POOL_CH)),
                const((Q_W, d)), const((CONV_CH, d)), const((POOL_CH, d)), const((d, d)),
                const((2 * e, d))]
    out_specs = [pl.BlockSpec((1, tt, d), lambda i, j: (i, j, 0)),
                 pl.BlockSpec((1, tt, d), lambda i, j: (i, j, 0)),
                 pl.BlockSpec((1, e, tt), lambda i, j: (i, 0, j))]
    out_shape = [jax.ShapeDtypeStruct((b, s, d), F32), jax.ShapeDtypeStruct((b, s, d), BF16),
                 jax.ShapeDtypeStruct((b, e, s), F32)]
    return pl.pallas_call(
        functools.partial(_merge_kernel, seq=s),
        grid=(b, s // tt),
        in_specs=in_specs, out_specs=out_specs, out_shape=out_shape,
        scratch_shapes=[pltpu.VMEM((tt + 2 * HALO, CONV_CH), F32), pltpu.VMEM((tt + 2 * HALO, POOL_CH), F32)],
        name="mix_merge",
    )(p, p, p, p, p, p, p, p, p, attn, x, g1, sh2, sc2, n2g,
      lw['conv_dw'], lw['conv_dw_b'], lw['conv_ln_g'], lw['conv_ln_b'], lw['w_pool'], lw['pool_scale'],
      lw['w_attn_o'], lw['w_conv_o'], lw['w_pool_o'], lw['w_out'], lw['w_router_t'])


def _topk_kernel(afft_ref, slot_ref, slott_ref, offs_ref, *, cap, blk):
    a = afft_ref[0]
    e, s = a.shape
    def keeps_cap(cand):
        return jnp.sum((a >= cand).astype(F32), axis=-1, keepdims=True) >= cap

    tiny = jnp.full((e, 1), F32_TINY, F32)
    thr = jnp.where(keeps_cap(tiny), tiny, 0.0)
    for step in (64, 32, 16, 8, 4, 2, 1):
        cand = thr * float(2 ** step)
        thr = jnp.where(keeps_cap(cand), cand, thr)
    delta = thr
    for _ in range(F32_MANTISSA_BITS):
        delta = delta * 0.5
        cand = thr + delta
        thr = jnp.where(keeps_cap(cand), cand, thr)
    gt = a > thr
    eq = a == thr
    need = cap - jnp.sum(gt.astype(F32), axis=-1, keepdims=True)

    r = lax.broadcasted_iota(I32, (blk, blk), 0)
    c = lax.broadcasted_iota(I32, (blk, blk), 1)
    upper = (r < c).astype(BF16)
    eye = (r == c).astype(F32)

    def prefix(mask_f32):
        carry = jnp.zeros((e, 1), F32)
        parts = []
        for j in range(s // blk):
            m = mask_f32[:, j * blk:(j + 1) * blk]
            parts.append(_dot(m.astype(BF16), upper) + carry)
            carry = carry + jnp.sum(m, axis=-1, keepdims=True)
        return jnp.concatenate(parts, axis=-1)

    sel = gt | (eq & (prefix(eq.astype(F32)) < need))
    pos = prefix(sel.astype(F32))
    slot = jnp.where(sel, pos, -1.0)
    slot_ref[0] = slot.astype(I32)
    for j in range(s // blk):
        slott_ref[0, j * blk:(j + 1) * blk, :] = _dot_nt(eye, slot[:, j * blk:(j + 1) * blk],
                                                        precision=HIGHEST).astype(I32)
    tok = lax.broadcasted_iota(I32, (s, LANES), 0)
    col = lax.broadcasted_iota(I32, (s, LANES), 1)
    before = (tok < col * blk).astype(BF16)
    offs_ref[0] = _dot(sel.astype(BF16), before).astype(I32)


def _topk(afft, cap):
    b, e, s = afft.shape
    blk = min(s, TOKEN_CHUNK)
    slot, slott, offs = pl.pallas_call(
        functools.partial(_topk_kernel, cap=cap, blk=blk),
        grid=(b,),
        in_specs=[pl.BlockSpec((1, e, s), lambda i: (i, 0, 0))],
        out_specs=[pl.BlockSpec((1, e, s), lambda i: (i, 0, 0)),
                   pl.BlockSpec((1, s, e), lambda i: (i, 0, 0)),
                   pl.BlockSpec((1, e, LANES), lambda i: (i, 0, 0))],
        out_shape=[jax.ShapeDtypeStruct((b, e, s), I32), jax.ShapeDtypeStruct((b, s, e), I32),
                   jax.ShapeDtypeStruct((b, e, LANES), I32)],
        name="expert_choice_topk",
    )(afft)
    return slot, slott, offs[:, :, :s // blk + 1]


def _slot_windows(offs_ref, idx, win):
    lo = offs_ref[idx]
    hi = offs_ref[idx + 1]
    first = lo // win
    return first, jnp.where(hi > lo, (hi - 1) // win - first + 1, 0)


def _gather_rows(onehot, h, gate_row):
    picked = _dot(onehot.astype(BF16), h)
    gates = jnp.sum(jnp.where(onehot, gate_row, 0.0), axis=-1, keepdims=True)
    return picked, gates


def _ffn_kernel(offs_ref, slot_ref, afft_ref, h_ref, slotc_ref, cafft_ref, hc_ref, wg_ref, wu_ref, wd_ref,
                ye_ref, yec_ref, wg_s, wu_s, wd_s, xe_s, g_s, *, chunk, group, win, rows):
    ex = pl.program_id(0)
    bi = pl.program_id(1)
    nb = pl.num_programs(1)
    cap = xe_s.shape[0]

    @pl.when(bi == 0)
    def _():
        def cast(i, carry):
            sl = pl.ds(pl.multiple_of(i * rows, rows), rows)
            wg_s[sl, :] = wg_ref[0, 0, sl, :].astype(BF16)
            wu_s[sl, :] = wu_ref[0, 0, sl, :].astype(BF16)
            wd_s[sl, :] = wd_ref[0, 0, sl, :].astype(BF16)
            return carry
        lax.fori_loop(0, wg_s.shape[0] // rows, cast, 0)

    def ffn(xe):
        a = _dot(xe, wg_s[...])
        u = _dot(xe, wu_s[...])
        hid = (a * _sigmoid(a) * u).astype(BF16)
        return _dot(hid, wd_s[...])

    nch = h_ref.shape[1] // chunk
    base = (bi * pl.num_programs(0) + ex) * (nch + 1)
    xe_s[...] = jnp.zeros_like(xe_s)
    g_s[...] = jnp.zeros_like(g_s)
    span = group * chunk
    starts = []
    fits = None
    for p in range(nch // group):
        lo = offs_ref[base + p * group]
        hi = offs_ref[base + (p + 1) * group]
        a = pl.multiple_of(jnp.minimum((lo // F32_SUBLANES) * F32_SUBLANES, cap - win), F32_SUBLANES)
        starts.append(a)
        fits = (hi - a <= win) if fits is None else jnp.logical_and(fits, hi - a <= win)

    def add_window(a, sl, width):
        onehot = (slot_ref[0, :, sl] - a) == lax.broadcasted_iota(I32, (win, width), 0)
        picked, gates = _gather_rows(onehot, h_ref[0, sl, :], afft_ref[0, :, sl])
        xe_s[pl.ds(a, win), :] += picked
        g_s[pl.ds(a, win), :] += gates

    @pl.when(fits)
    def _():
        for p, a in enumerate(starts):
            add_window(a, slice(p * span, (p + 1) * span), span)

    @pl.when(jnp.logical_not(fits))
    def _():
        def per_chunk(j, carry):
            first, nwin = _slot_windows(offs_ref, base + j, win)
            sl = pl.ds(pl.multiple_of(j * chunk, chunk), chunk)

            def window(w, c):
                add_window(pl.multiple_of((first + w) * win, win), sl, chunk)
                return c
            lax.fori_loop(0, nwin, window, 0)
            return carry
        lax.fori_loop(0, nch, per_chunk, 0)

    ye_ref[0] = (ffn(xe_s[...].astype(BF16)) * g_s[...]).astype(ye_ref.dtype)

    if yec_ref is not None:
        @pl.when(bi == nb - 1)
        def _():
            rowc = lax.broadcasted_iota(I32, (yec_ref.shape[1], hc_ref.shape[0]), 0)
            picked, gates = _gather_rows(slotc_ref[0] == rowc, hc_ref[...], cafft_ref[0])
            yec_ref[0] = (ffn(picked.astype(BF16)) * gates).astype(yec_ref.dtype)


def _expert_ffn(layer, slot, offs, afft, h2, wg, wu, wd, cap, ctx_part=None):
    b, e, s = slot.shape
    d = h2.shape[2]
    f = wg.shape[3]
    assert f == d
    chunk = min(s, TOKEN_CHUNK)
    nch = s // chunk
    group = 2 if nch % 2 == 0 else 1
    win = min(cap, GATHER_WINDOW)
    assert cap % win == 0
    w_spec = pl.BlockSpec((1, 1, d, f), lambda j, i, o: (layer, j, 0, 0))
    row_spec = pl.BlockSpec((1, 1, s), lambda j, i, o: (i * e + j, 0, 0))
    in_specs = [row_spec, row_spec, pl.BlockSpec((1, s, d), lambda j, i, o: (i, 0, 0))]
    out_specs = [pl.BlockSpec((1, cap, d), lambda j, i, o: (i * e + j, 0, 0))]
    out_shape = [jax.ShapeDtypeStruct((b * e, cap, d), BF16)]
    args = [slot.reshape(b * e, 1, s), afft.reshape(b * e, 1, s), h2]
    body = functools.partial(_ffn_kernel, chunk=chunk, group=group, win=win, rows=128)
    if ctx_part is None:
        def kern(offs_ref, slot_ref, afft_ref, h_ref, wg_ref, wu_ref, wd_ref, ye_ref, *scratch):
            body(offs_ref, slot_ref, afft_ref, h_ref, None, None, None, wg_ref, wu_ref, wd_ref, ye_ref, None,
                 *scratch)
    else:
        slot_c, afft_c, h_c, rows_c = ctx_part
        n_c = h_c.shape[0]
        rowc_spec = pl.BlockSpec((1, 1, n_c), lambda j, i, o: (j, 0, 0))
        in_specs += [rowc_spec, rowc_spec, pl.BlockSpec((n_c, d), lambda j, i, o: (0, 0))]
        out_specs.append(pl.BlockSpec((1, rows_c, d), lambda j, i, o: (j, 0, 0)))
        out_shape.append(jax.ShapeDtypeStruct((e, rows_c, d), BF16))
        args += [slot_c, afft_c, h_c]
        kern = body
    outs = pl.pallas_call(
        kern,
        grid_spec=pltpu.PrefetchScalarGridSpec(
            num_scalar_prefetch=1, grid=(e, b),
            in_specs=in_specs + [w_spec, w_spec, w_spec], out_specs=out_specs,
            scratch_shapes=[pltpu.VMEM((d, f), BF16), pltpu.VMEM((d, f), BF16), pltpu.VMEM((f, d), BF16),
                            pltpu.VMEM((cap, d), F32), pltpu.VMEM((cap, 1), F32)]),
        out_shape=out_shape,
        name="expert_ffn",
    )(offs.reshape(-1), *args, wg, wu, wd)
    ye = outs[0].reshape(b, e * cap, d)
    return ye if ctx_part is None else (ye, outs[1])


def _combine_kernel(offs_ref, slott_ref, ye_ref, x_ref, g2_ref, fg_ref, o_ref, acc_s, *, cap, win, final_norm):
    bi = pl.program_id(0)
    tt = acc_s.shape[0]
    n_sub = x_ref.shape[1] // tt
    nch = pl.num_programs(1) * n_sub
    per_block = LANES // win

    for sub in range(n_sub):
        j = pl.program_id(1) * n_sub + sub
        tok = slice(sub * tt, (sub + 1) * tt)

        starts = []
        fits = None
        for ex in range(N_EXPERTS):
            idx = (bi * N_EXPERTS + ex) * (nch + 1) + j
            a = pl.multiple_of(jnp.minimum((offs_ref[idx] // BF16_SUBLANES) * BF16_SUBLANES, cap - win),
                               BF16_SUBLANES)
            starts.append(a)
            ok = offs_ref[idx + 1] - a <= win
            fits = ok if fits is None else jnp.logical_and(fits, ok)

        @pl.when(fits)
        def _(starts=starts, tok=tok):
            lane = lax.broadcasted_iota(I32, (tt, LANES), 1)
            blocks = []
            for blk in range(N_EXPERTS // per_block):
                target = None
                for q in range(per_block):
                    ex = blk * per_block + q
                    t = slott_ref[0, tok, ex:ex + 1] - starts[ex] + q * win
                    target = t if target is None else jnp.where(lane >= q * win, t, target)
                blocks.append((target == lane).astype(BF16))
            rows = [ye_ref[0, pl.ds(ex * cap + starts[ex], win), :] for ex in range(N_EXPERTS)]
            acc_s[...] = _dot(jnp.concatenate(blocks, axis=1), jnp.concatenate(rows, axis=0))

        @pl.when(jnp.logical_not(fits))
        def _(j=j, tok=tok):
            lane = lax.broadcasted_iota(I32, (tt, win), 1)
            acc_s[...] = jnp.zeros_like(acc_s)
            for ex in range(N_EXPERTS):
                first, nwin = _slot_windows(offs_ref, (bi * N_EXPERTS + ex) * (nch + 1) + j, win)

                def window(w, c, ex=ex, first=first):
                    a = pl.multiple_of((first + w) * win, win)
                    onehot = ((slott_ref[0, tok, ex:ex + 1] - a) == lane).astype(BF16)
                    acc_s[...] += _dot(onehot, ye_ref[0, pl.ds(ex * cap + a, win), :])
                    return c
                lax.fori_loop(0, nwin, window, 0)

        out = x_ref[0, tok, :] + g2_ref[0] * acc_s[...]
        if final_norm:
            out = out * lax.rsqrt(jnp.mean(out * out, axis=-1, keepdims=True) + EPS) * fg_ref[...]
        o_ref[0, tok, :] = out


def _combine(slott, offs, ye, x, g2, fg, cap, final_norm):
    b, s, d = x.shape
    e = N_EXPERTS
    chunk = min(s, TOKEN_CHUNK)
    tt = min(s, COMBINE_CHUNKS * chunk)
    win = min(cap, SCATTER_WINDOW)
    assert cap % win == 0 and LANES % win == 0 and e % (LANES // win) == 0 and s % tt == 0
    return pl.pallas_call(
        functools.partial(_combine_kernel, cap=cap, win=win, final_norm=final_norm),
        grid_spec=pltpu.PrefetchScalarGridSpec(
            num_scalar_prefetch=1, grid=(b, s // tt),
            in_specs=[pl.BlockSpec((1, tt, e), lambda i, j, o: (i, j, 0)),
                      pl.BlockSpec((1, e * cap, d), lambda i, j, o: (i, 0, 0)),
                      pl.BlockSpec((1, tt, d), lambda i, j, o: (i, j, 0)),
                      pl.BlockSpec((1, 1, d), lambda i, j, o: (i, 0, 0)),
                      pl.BlockSpec((1, d), lambda i, j, o: (0, 0))],
            out_specs=pl.BlockSpec((1, tt, d), lambda i, j, o: (i, j, 0)),
            scratch_shapes=[pltpu.VMEM((chunk, d), F32)]),
        out_shape=jax.ShapeDtypeStruct((b, s, d), F32),
        name="moe_combine",
    )(offs.reshape(-1), slott, ye, x, g2, fg)


def _split_hi_lo(w):
    hi = w.astype(BF16)
    return jnp.concatenate([hi, (w - hi.astype(F32)).astype(BF16)], axis=0)


def _permute_in_cols(w):
    o_k = Q_W
    o_v = o_k + KV_W
    o_ca = o_v + KV_W
    o_cg = o_ca + CONV_CH
    o_p = o_cg + CONV_CH
    o_g = o_p + POOL_CH
    return jnp.concatenate([w[:, o_g:], w[:, :o_k], w[:, o_ca:o_cg], w[:, o_cg:o_p], w[:, o_p:o_g],
                            w[:, o_k:o_v], w[:, o_v:o_ca]], axis=1)


def kernel(x, c, ctx, c_ctx, norm1_g, norm2_g, w_mod, b_mod, w_in, attn_sink, w_attn_o, conv_dw, conv_dw_b,
           conv_ln_g, conv_ln_b, w_conv_o, w_pool, pool_scale, w_pool_o, w_out, w_router, w_e_gate, w_e_up,
           w_e_down, final_norm_g):
    b, s, d = x.shape
    l = ctx.shape[1]
    depth = w_in.shape[0]
    assert d == _D_MODEL and w_in.shape[2] == IN_W and CONV_CH == POOL_CH

    tabs = _rope_tables(s)
    cc = jnp.zeros((8, d), F32).at[:b].set(c).at[b].set(c_ctx)
    mod = _modulation(cc, w_mod, b_mod)
    fg = final_norm_g.reshape(1, d)

    for layer in range(depth):
        last = layer == depth - 1
        mx = mod[layer, :b].reshape(b, 1, 6, d)
        sh1, sc1, g1, sh2, sc2, g2 = [mx[:, :, i] for i in range(6)]
        mc = jnp.broadcast_to(mod[layer, b].reshape(1, 1, 6, d), (b, 1, 6, d))
        csh1, csc1, cg1, csh2, csc2, cg2 = [mc[:, :, i] for i in range(6)]
        n1g = norm1_g[layer].reshape(1, d)
        n2g = norm2_g[layer].reshape(1, d)
        w_in_l = _permute_in_cols(w_in[layer]).astype(BF16)
        lw = {'conv_dw': conv_dw[layer], 'conv_dw_b': conv_dw_b[layer].reshape(1, -1),
              'conv_ln_g': conv_ln_g[layer].reshape(1, -1), 'conv_ln_b': conv_ln_b[layer].reshape(1, -1),
              'w_pool': w_pool[layer].astype(BF16), 'pool_scale': pool_scale[layer].reshape(1, -1),
              'w_attn_o': w_attn_o[layer].astype(BF16), 'w_conv_o': w_conv_o[layer].astype(BF16),
              'w_pool_o': w_pool_o[layer].astype(BF16), 'w_out': w_out[layer].astype(BF16),
              'w_router_t': _split_hi_lo(w_router[layer].T)}

        p_x = _inproj(x, n1g, sh1, sc1, w_in_l, tm=512)
        if last:
            p_c = _inproj(ctx, n1g, csh1, csc1, w_in_l[:, WCOL_K:], tm=l, mixer_epilogue=False)
            ctx_kv_cols = (0, KV_W)
        else:
            p_c = _inproj(ctx, n1g, csh1, csc1, w_in_l, tm=l)
            ctx_kv_cols = (COL_K, COL_V)
        attn_x = _window_attention(p_x, p_c, ctx_kv_cols, attn_sink[layer], tabs)
        x_mid, h2, afft = _merge(p_x, attn_x, x, g1, sh2, sc2, n2g, lw, tt=256)
        if not last:
            attn_c = _context_attention(p_c, attn_sink[layer])
            c_mid, ch2, cafft = _merge(p_c, attn_c, ctx, cg1, csh2, csc2, n2g, lw, tt=l)
        cap = (CAPACITY_FACTOR * s) // N_EXPERTS
        slot, slott, offs = _topk(afft, cap)
        if last:
            ye = _expert_ffn(layer, slot, offs, afft, h2, w_e_gate, w_e_up, w_e_down, cap)
        else:
            cap_c = (CAPACITY_FACTOR * l) // N_EXPERTS
            cslot, cslott, coffs = _topk(cafft, cap_c)
            sample_base = (jnp.arange(b, dtype=I32) * cap_c)[:, None, None]
            cslot_all = jnp.where(cslot >= 0, cslot + sample_base, -1)
            cslot_all = cslot_all.transpose(1, 0, 2).reshape(N_EXPERTS, 1, b * l)
            cafft_all = cafft.transpose(1, 0, 2).reshape(N_EXPERTS, 1, b * l)
            ye, yec = _expert_ffn(layer, slot, offs, afft, h2, w_e_gate, w_e_up, w_e_down, cap,
                                  ctx_part=(cslot_all, cafft_all, ch2.reshape(b * l, d), b * cap_c))
            yec = yec.reshape(N_EXPERTS, b, cap_c, d).transpose(1, 0, 2, 3).reshape(b, N_EXPERTS * cap_c, d)
            ctx = _combine(cslott, coffs, yec, c_mid, cg2, fg, cap_c, False)
        x = _combine(slott, offs, ye, x_mid, g2, fg, cap, last)
    return x
```

```python
import functools

import jax
import jax.numpy as jnp
import numpy as np
from jax import lax
from jax.experimental import pallas as pl
from jax.experimental.pallas import tpu as pltpu

F32 = jnp.float32
BF16 = jnp.bfloat16
I32 = jnp.int32

EPS = 1e-6
GRID_W = 64
N_HEADS = 8
N_KV_HEADS = 2
HEAD_DIM = 64
GQA_GROUP = N_HEADS // N_KV_HEADS
WINDOW = 128
ATTN_BLOCK = 128
ROPE_BASE = 10000.0
ROPE_PAIRS = HEAD_DIM // 4
CONV_CH = 512
CONV_K = 31
CONV_PAD = CONV_K // 2
POOL_WINDOWS = (2, 4, 8, 16)
POOL_GROUP = 128
POOL_CH = POOL_GROUP * len(POOL_WINDOWS)
N_EXPERTS = 16
CAPACITY_FACTOR = 2
Q_W = N_HEADS * HEAD_DIM
KV_W = N_KV_HEADS * HEAD_DIM

LANES = 128
HALO = 16
HIGHEST = lax.Precision.HIGHEST
F32_TINY = 2.0 ** -126
F32_MANTISSA_BITS = 23
TOKEN_CHUNK = 256
GATHER_WINDOW = 128
SCATTER_WINDOW = 64
F32_SUBLANES = 8
BF16_SUBLANES = 16
ATTN_LOOKAHEAD = 3
ATTN_BLOCKS_PER_STEP = 4
COMBINE_CHUNKS = 2

_D_MODEL = 1024
N_GATE_COLS = 3 * _D_MODEL
WCOL_Q = N_GATE_COLS
WCOL_CONV_A = WCOL_Q + Q_W
WCOL_CONV_G = WCOL_CONV_A + CONV_CH
WCOL_POOL = WCOL_CONV_G + CONV_CH
WCOL_K = WCOL_POOL + POOL_CH
IN_W = WCOL_K + 2 * KV_W
COL_Q = N_GATE_COLS
COL_CONV_U = COL_Q + Q_W
COL_POOL = COL_CONV_U + CONV_CH
COL_K = COL_POOL + POOL_CH
COL_V = COL_K + KV_W
OUT_W = COL_V + KV_W
GATE_CHUNK = 768


def _dot(a, b):
    return jnp.dot(a, b, preferred_element_type=F32)


def _dot_nt(a, b, precision=None):
    return lax.dot_general(a, b, (((1,), (1,)), ((), ())), preferred_element_type=F32, precision=precision)


def _sigmoid(v):
    return 0.5 * jnp.tanh(0.5 * v) + 0.5


def _rms_mod(x, g, sh, sc):
    y = x * lax.rsqrt(jnp.mean(x * x, axis=-1, keepdims=True) + EPS) * g
    return y * (1.0 + sc) + sh


def _hi_lo(v):
    hi = v.astype(BF16)
    return hi, (v - hi.astype(F32)).astype(BF16)


def _mod_kernel(c_ref, w_ref, b_ref, o_ref):
    c = c_ref[...]
    a_hi, a_lo = _hi_lo(c * _sigmoid(c))
    w_hi, w_lo = _hi_lo(w_ref[0])
    o_ref[0] = _dot(a_hi, w_hi) + _dot(a_lo, w_hi) + _dot(a_hi, w_lo) + b_ref[0]


def _modulation(cc, w_mod, b_mod, tn=1536):
    depth, d, n = w_mod.shape
    rows = cc.shape[0]
    return pl.pallas_call(
        _mod_kernel,
        grid=(depth, n // tn),
        in_specs=[pl.BlockSpec((rows, d), lambda l, j: (0, 0)),
                  pl.BlockSpec((1, d, tn), lambda l, j: (l, 0, j)),
                  pl.BlockSpec((1, 1, tn), lambda l, j: (l, 0, j))],
        out_specs=pl.BlockSpec((1, rows, tn), lambda l, j: (l, 0, j)),
        out_shape=jax.ShapeDtypeStruct((depth, rows, n), F32),
        name="modulation",
    )(cc, w_mod, b_mod.reshape(depth, 1, n))


def _inproj_kernel(x_ref, g_ref, sh_ref, sc_ref, w_ref, o_ref, *, mixer_epilogue):
    h = _rms_mod(x_ref[0], g_ref[...], sh_ref[0], sc_ref[0]).astype(BF16)
    if not mixer_epilogue:
        o_ref[0] = _dot(h, w_ref[...]).astype(o_ref.dtype)
        return
    for c0 in range(0, N_GATE_COLS, GATE_CHUNK):
        cols = slice(c0, c0 + GATE_CHUNK)
        o_ref[0, :, cols] = _sigmoid(_dot(h, w_ref[:, cols])).astype(o_ref.dtype)
    o_ref[0, :, COL_Q:COL_CONV_U] = _dot(h, w_ref[:, WCOL_Q:WCOL_CONV_A]).astype(o_ref.dtype)
    glu = _dot(h, w_ref[:, WCOL_CONV_A:WCOL_POOL])
    o_ref[0, :, COL_CONV_U:COL_POOL] = (glu[:, :CONV_CH] * _sigmoid(glu[:, CONV_CH:])).astype(o_ref.dtype)
    o_ref[0, :, COL_POOL:] = _dot(h, w_ref[:, WCOL_POOL:]).astype(o_ref.dtype)


def _inproj(x, g, sh, sc, w, tm, mixer_epilogue=True):
    b, s, d = x.shape
    n = w.shape[1]
    n_out = OUT_W if mixer_epilogue else n
    assert not mixer_epilogue or n == IN_W
    return pl.pallas_call(
        functools.partial(_inproj_kernel, mixer_epilogue=mixer_epilogue),
        grid=(b, s // tm),
        in_specs=[pl.BlockSpec((1, tm, d), lambda i, j: (i, j, 0)),
                  pl.BlockSpec((1, d), lambda i, j: (0, 0)),
                  pl.BlockSpec((1, 1, d), lambda i, j: (i, 0, 0)),
                  pl.BlockSpec((1, 1, d), lambda i, j: (i, 0, 0)),
                  pl.BlockSpec((d, n), lambda i, j: (0, 0))],
        out_specs=pl.BlockSpec((1, tm, n_out), lambda i, j: (i, j, 0)),
        out_shape=jax.ShapeDtypeStruct((b, s, n_out), BF16),
        name="inproj",
    )(x, g, sh, sc, w)


def _rope_tables(s):
    t = np.arange(s)
    row = (t // GRID_W).astype(np.float32)
    col = (t % GRID_W).astype(np.float32)
    freqs = jnp.asarray(ROPE_BASE, F32) ** (-jnp.arange(ROPE_PAIRS, dtype=F32) / ROPE_PAIRS)
    ang_r = jnp.asarray(row)[:, None] * freqs
    ang_c = jnp.asarray(col)[:, None] * freqs
    cos_h = jnp.concatenate([jnp.cos(ang_r), jnp.cos(ang_r), jnp.cos(ang_c), jnp.cos(ang_c)], axis=-1)
    sin_h = jnp.concatenate([-jnp.sin(ang_r), jnp.sin(ang_r), -jnp.sin(ang_c), jnp.sin(ang_c)], axis=-1)
    return jnp.tile(cos_h, (1, LANES // HEAD_DIM)), jnp.tile(sin_h, (1, LANES // HEAD_DIM))


def _rope(x, cos, sin_signed):
    lane = lax.broadcasted_iota(I32, x.shape, 1)
    low = (lane & (2 * ROPE_PAIRS - 1)) < ROPE_PAIRS
    partner = jnp.where(low, pltpu.roll(x, LANES - ROPE_PAIRS, 1), pltpu.roll(x, ROPE_PAIRS, 1))
    return x * cos + partner * sin_signed


def _softmax_pv(s_list, v_list, sink):
    m = sink
    for s in s_list:
        m = jnp.maximum(m, jnp.max(s, axis=-1, keepdims=True))
    denom = jnp.exp(sink - m)
    o = None
    for s, v in zip(s_list, v_list):
        e = jnp.exp(s - m)
        denom = denom + jnp.sum(e, axis=-1, keepdims=True)
        pv = _dot(e.astype(BF16), v)
        o = pv if o is None else o + pv
    return o / denom


def _lane_lo(shape):
    return lax.broadcasted_iota(I32, shape, 1) < HEAD_DIM


def _dup_heads(t):
    swapped = pltpu.roll(t, HEAD_DIM, 1)
    lo = _lane_lo(t.shape)
    return jnp.where(lo, t, swapped), jnp.where(lo, swapped, t)


def _heads_attention(qps, keys, vals, masks, sinks):
    lo = _lane_lo(qps[0].shape)
    keeps = (lo, jnp.logical_not(lo))
    tiles_per_kv = len(qps) // len(keys)

    def head_scores(h):
        qp = qps[h // 2]
        qh = jnp.where(keeps[h % 2], qp, jnp.zeros_like(qp))
        g = h // 2 // tiles_per_kv
        return [sc if mask is None else jnp.where(mask, sc, -1e30)
                for sc, mask in zip([_dot_nt(qh, k) for k in keys[g]], masks[g])]

    n_heads = 2 * len(qps)
    outs = []
    pending = [head_scores(h) for h in range(min(ATTN_LOOKAHEAD, n_heads))]
    for h in range(n_heads):
        if h + ATTN_LOOKAHEAD < n_heads:
            pending.append(head_scores(h + ATTN_LOOKAHEAD))
        outs.append(_softmax_pv(pending.pop(0), vals[h // 2 // tiles_per_kv], sinks[h]))
    return [jnp.where(lo, outs[2 * i], outs[2 * i + 1]) for i in range(len(qps))]


def _store_dup(dst_ref, t):
    d0, d1 = _dup_heads(t)
    dst_ref[0] = d0.astype(dst_ref.dtype)
    dst_ref[1] = d1.astype(dst_ref.dtype)


def _win_attn_kernel(sink_ref, q_ref, k_ref, v_ref, kc_ref, vc_ref, cosq_ref, sinq_ref, cosk_ref, sink_tab_ref,
                     o_ref, kd_ref, vd_ref, kcd_ref, vcd_ref, *, seq):
    i = pl.program_id(1)
    blk = ATTN_BLOCK
    win = 3 * blk

    @pl.when(i == 0)
    def _():
        _store_dup(kd_ref, _rope(k_ref[0].astype(F32), cosk_ref[...], sink_tab_ref[...]))
        _store_dup(vd_ref, v_ref[0].astype(F32))
        _store_dup(kcd_ref, kc_ref[0].astype(F32))
        _store_dup(vcd_ref, vc_ref[0].astype(F32))

    scale = HEAD_DIM ** -0.5
    n_sub = q_ref.shape[1] // blk
    qps, keys, vals, masks = [], [], [], []
    for sub in range(n_sub):
        qb = i * n_sub + sub
        rows = slice(sub * blk, (sub + 1) * blk)
        start = pl.multiple_of(jnp.clip((qb - 1) * blk, 0, seq - win), blk)
        qpos = qb * blk + lax.broadcasted_iota(I32, (blk, win), 0)
        kpos = start + lax.broadcasted_iota(I32, (blk, win), 1)
        mask = jnp.abs(kpos - qpos) <= WINDOW
        cos = cosq_ref[rows, :]
        sin = sinq_ref[rows, :]
        qps += [(_rope(q_ref[0, rows, p * LANES:(p + 1) * LANES].astype(F32), cos, sin) * scale).astype(BF16)
                for p in range(N_HEADS // 2)]
        keys += [[kd_ref[kh, pl.ds(start, win), :], kcd_ref[kh]] for kh in range(N_KV_HEADS)]
        vals += [[vd_ref[kh, pl.ds(start, win), :], vcd_ref[kh]] for kh in range(N_KV_HEADS)]
        masks += [[mask, None]] * N_KV_HEADS
    sinks = [sink_ref[h] for h in range(N_HEADS)] * n_sub
    for t, o in enumerate(_heads_attention(qps, keys, vals, masks, sinks)):
        sub, p = divmod(t, N_HEADS // 2)
        o_ref[0, sub * blk:(sub + 1) * blk, p * LANES:(p + 1) * LANES] = o.astype(o_ref.dtype)


def _window_attention(p_x, p_c, ctx_kv_cols, sink, tabs):
    b, s, _ = p_x.shape
    l = p_c.shape[1]
    cos, sin = tabs
    blk = ATTN_BLOCKS_PER_STEP * ATTN_BLOCK
    assert s % blk == 0
    kcol, vcol = COL_K // KV_W, COL_V // KV_W
    kccol, vccol = ctx_kv_cols[0] // KV_W, ctx_kv_cols[1] // KV_W
    return pl.pallas_call(
        functools.partial(_win_attn_kernel, seq=s),
        grid=(b, s // blk),
        in_specs=[pl.BlockSpec(memory_space=pltpu.SMEM),
                  pl.BlockSpec((1, blk, Q_W), lambda i, j: (i, j, COL_Q // Q_W)),
                  pl.BlockSpec((1, s, KV_W), lambda i, j: (i, 0, kcol)),
                  pl.BlockSpec((1, s, KV_W), lambda i, j: (i, 0, vcol)),
                  pl.BlockSpec((1, l, KV_W), lambda i, j: (i, 0, kccol)),
                  pl.BlockSpec((1, l, KV_W), lambda i, j: (i, 0, vccol)),
                  pl.BlockSpec((blk, LANES), lambda i, j: (j, 0)),
                  pl.BlockSpec((blk, LANES), lambda i, j: (j, 0)),
                  pl.BlockSpec((s, LANES), lambda i, j: (0, 0)),
                  pl.BlockSpec((s, LANES), lambda i, j: (0, 0))],
        out_specs=pl.BlockSpec((1, blk, Q_W), lambda i, j: (i, j, 0)),
        out_shape=jax.ShapeDtypeStruct((b, s, Q_W), BF16),
        scratch_shapes=[pltpu.VMEM((N_KV_HEADS, s, LANES), BF16), pltpu.VMEM((N_KV_HEADS, s, LANES), BF16),
                        pltpu.VMEM((N_KV_HEADS, l, LANES), BF16), pltpu.VMEM((N_KV_HEADS, l, LANES), BF16)],
        name="window_attention",
    )(sink, p_x, p_x, p_x, p_c, p_c, cos, sin, cos, sin)


def _ctx_attn_kernel(sink_ref, q_ref, k_ref, v_ref, o_ref):
    kd = [t.astype(BF16) for t in _dup_heads(k_ref[0].astype(F32))]
    vd = [t.astype(BF16) for t in _dup_heads(v_ref[0].astype(F32))]
    scale = HEAD_DIM ** -0.5
    qps = [(q_ref[0, :, p * LANES:(p + 1) * LANES].astype(F32) * scale).astype(BF16) for p in range(N_HEADS // 2)]
    sinks = [sink_ref[h] for h in range(N_HEADS)]
    outs = _heads_attention(qps, [[k] for k in kd], [[v] for v in vd], [[None]] * N_KV_HEADS, sinks)
    for p, o in enumerate(outs):
        o_ref[0, :, p * LANES:(p + 1) * LANES] = o.astype(o_ref.dtype)


def _context_attention(p_c, sink):
    b, l, _ = p_c.shape
    return pl.pallas_call(
        _ctx_attn_kernel,
        grid=(b,),
        in_specs=[pl.BlockSpec(memory_space=pltpu.SMEM),
                  pl.BlockSpec((1, l, Q_W), lambda i: (i, 0, COL_Q // Q_W)),
                  pl.BlockSpec((1, l, KV_W), lambda i: (i, 0, COL_K // KV_W)),
                  pl.BlockSpec((1, l, KV_W), lambda i: (i, 0, COL_V // KV_W))],
        out_specs=pl.BlockSpec((1, l, Q_W), lambda i: (i, 0, 0)),
        out_shape=jax.ShapeDtypeStruct((b, l, Q_W), BF16),
        name="context_attention",
    )(sink, p_c, p_c, p_c)


def _merge_kernel(ga_ref, gb_ref, gc_ref,
                  u_ref, u_p_ref, u_n_ref, pz_ref, pz_p_ref, pz_n_ref,
                  attn_ref, x_ref, g1_ref, sh2_ref, sc2_ref, n2g_ref,
                  dw_ref, dwb_ref, lng_ref, lnb_ref, wpool_ref, pscale_ref,
                  wa_ref, wb_ref, wc_ref, wo_ref, wrt_ref,
                  xo_ref, h2_ref, afft_ref,
                  uwin_ref, zwin_ref, *, seq):
    t = pl.program_id(1)
    tt = x_ref.shape[1]
    has_prev = (t > 0).astype(F32)
    has_next = (t < pl.num_programs(1) - 1).astype(F32)

    uwin_ref[0:HALO, :] = u_p_ref[0].astype(F32) * has_prev
    uwin_ref[HALO:HALO + tt, :] = u_ref[0].astype(F32)
    uwin_ref[HALO + tt:, :] = u_n_ref[0].astype(F32) * has_next
    first = HALO - CONV_PAD
    rows = tt + 2 * HALO
    acc_cols = []
    for cb in range(CONV_CH // LANES):
        cols = slice(cb * LANES, (cb + 1) * LANES)
        window = uwin_ref[:, cols]
        acc_c = jnp.zeros((tt, LANES), F32) + dwb_ref[:, cols]
        for shift in range(F32_SUBLANES):
            taps = [k for k in range(CONV_K) if (first + k) % F32_SUBLANES == shift]
            if not taps:
                continue
            shifted = window if shift == 0 else pltpu.roll(window, rows - shift, 0)
            for k in taps:
                off = first + k - shift
                acc_c = acc_c + shifted[off:off + tt] * dw_ref[k:k + 1, cols]
        acc_cols.append(acc_c)
    acc = jnp.concatenate(acc_cols, axis=-1)
    mu = jnp.mean(acc, axis=-1, keepdims=True)
    cen = acc - mu
    var = jnp.mean(cen * cen, axis=-1, keepdims=True)
    ln = cen * lax.rsqrt(var + EPS) * lng_ref[...] + lnb_ref[...]
    feat_b = (ln * _sigmoid(ln)).astype(BF16)

    zwin_ref[0:HALO, :] = pz_p_ref[0].astype(F32) * has_prev
    zwin_ref[HALO:HALO + tt, :] = pz_ref[0].astype(F32)
    zwin_ref[HALO + tt:, :] = pz_n_ref[0].astype(F32) * has_next
    tpos = t * tt + lax.broadcasted_iota(I32, (tt, 1), 0)
    pooled = []
    for gi, w in enumerate(POOL_WINDOWS):
        cols = slice(gi * POOL_GROUP, (gi + 1) * POOL_GROUP)
        tot = zwin_ref[pl.ds(HALO - w // 2, tt), cols]
        for d in range(1 - w // 2, w - w // 2):
            tot = tot + zwin_ref[pl.ds(HALO + d, tt), cols]
        cnt = (jnp.minimum(tpos + (w - w // 2), seq) - jnp.maximum(tpos - w // 2, 0)).astype(F32)
        diff = tot / cnt - zwin_ref[pl.ds(HALO, tt), cols]
        pooled.append(_dot(diff.astype(BF16), wpool_ref[gi]))
    feat_c = (jnp.concatenate(pooled, axis=-1) * pscale_ref[...]).astype(BF16)

    y_a = _dot(attn_ref[0], wa_ref[...])
    y_b = _dot(feat_b, wb_ref[...])
    y_c = _dot(feat_c, wc_ref[...])
    merged = ga_ref[0].astype(F32) * y_a + gb_ref[0].astype(F32) * y_b + gc_ref[0].astype(F32) * y_c
    xn = x_ref[0] + g1_ref[0] * _dot(merged.astype(BF16), wo_ref[...])
    xo_ref[0] = xn

    h2 = _rms_mod(xn, n2g_ref[...], sh2_ref[0], sc2_ref[0])
    h2_hi, h2_lo = _hi_lo(h2)
    h2_ref[0] = h2_hi
    ne = afft_ref.shape[1]
    by_hi = _dot_nt(wrt_ref[...], h2_hi)
    logits_t = by_hi[:ne] + by_hi[ne:] + _dot_nt(wrt_ref[:ne, :], h2_lo)
    et = jnp.exp(logits_t - jnp.max(logits_t, axis=0, keepdims=True))
    afft_ref[0] = et / jnp.sum(et, axis=0, keepdims=True)


def _merge(p, attn, x, g1, sh2, sc2, n2g, lw, tt):
    b, s, d = x.shape
    nh = tt // HALO
    last_h = s // HALO - 1
    e = N_EXPERTS

    def main(width, col):
        return pl.BlockSpec((1, tt, width), lambda i, j: (i, j, col))

    def prev(col):
        return pl.BlockSpec((1, HALO, CONV_CH), lambda i, j: (i, jnp.maximum(j * nh - 1, 0), col))

    def nxt(col):
        return pl.BlockSpec((1, HALO, CONV_CH), lambda i, j: (i, jnp.minimum((j + 1) * nh, last_h), col))

    def per_batch():
        return pl.BlockSpec((1, 1, d), lambda i, j: (i, 0, 0))

    def const(shape):
        return pl.BlockSpec(shape, lambda i, j: (0,) * len(shape))

    cu, pz = COL_CONV_U // CONV_CH, COL_POOL // CONV_CH
    in_specs = [main(d, 0), main(d, 1), main(d, 2),
                main(CONV_CH, cu), prev(cu), nxt(cu), main(POOL_CH, pz), prev(pz), nxt(pz),
                pl.BlockSpec((1, tt, Q_W), lambda i, j: (i, j, 0)),
                pl.BlockSpec((1, tt, d), lambda i, j: (i, j, 0)),
                per_batch(), per_batch(), per_batch(), const((1, d)),
                const((CONV_K, CONV_CH)), const((1, CONV_CH)), const((1, CONV_CH)), const((1, CONV_CH)),
                const((len(POOL_WINDOWS), POOL_GROUP, POOL_GROUP)), const((1, POOL_CH)),
                const((Q_W, d)), const((CONV_CH, d)), const((POOL_CH, d)), const((d, d)),
                const((2 * e, d))]
    out_specs = [pl.BlockSpec((1, tt, d), lambda i, j: (i, j, 0)),
                 pl.BlockSpec((1, tt, d), lambda i, j: (i, j, 0)),
                 pl.BlockSpec((1, e, tt), lambda i, j: (i, 0, j))]
    out_shape = [jax.ShapeDtypeStruct((b, s, d), F32), jax.ShapeDtypeStruct((b, s, d), BF16),
                 jax.ShapeDtypeStruct((b, e, s), F32)]
    return pl.pallas_call(
        functools.partial(_merge_kernel, seq=s),
        grid=(b, s // tt),
        in_specs=in_specs, out_specs=out_specs, out_shape=out_shape,
        scratch_shapes=[pltpu.VMEM((tt + 2 * HALO, CONV_CH), F32), pltpu.VMEM((tt + 2 * HALO, POOL_CH), F32)],
        name="mix_merge",
    )(p, p, p, p, p, p, p, p, p, attn, x, g1, sh2, sc2, n2g,
      lw['conv_dw'], lw['conv_dw_b'], lw['conv_ln_g'], lw['conv_ln_b'], lw['w_pool'], lw['pool_scale'],
      lw['w_attn_o'], lw['w_conv_o'], lw['w_pool_o'], lw['w_out'], lw['w_router_t'])


def _topk_kernel(afft_ref, slot_ref, slott_ref, offs_ref, *, cap, blk):
    a = afft_ref[0]
    e, s = a.shape

    def keeps_cap(cand):
        return jnp.sum((a >= cand).astype(F32), axis=-1, keepdims=True) >= cap

    tiny = jnp.full((e, 1), F32_TINY, F32)
    thr = jnp.where(keeps_cap(tiny), tiny, 0.0)
    for step in (64, 32, 16, 8, 4, 2, 1):
        cand = thr * float(2 ** step)
        thr = jnp.where(keeps_cap(cand), cand, thr)
    delta = thr
    for _ in range(F32_MANTISSA_BITS):
        delta = delta * 0.5
        cand = thr + delta
        thr = jnp.where(keeps_cap(cand), cand, thr)
    gt = a > thr
    eq = a == thr
    need = cap - jnp.sum(gt.astype(F32), axis=-1, keepdims=True)

    r = lax.broadcasted_iota(I32, (blk, blk), 0)
    c = lax.broadcasted_iota(I32, (blk, blk), 1)
    upper = (r < c).astype(BF16)
    eye = (r == c).astype(F32)

    def prefix(mask_f32):
        carry = jnp.zeros((e, 1), F32)
        parts = []
        for j in range(s // blk):
            m = mask_f32[:, j * blk:(j + 1) * blk]
            parts.append(_dot(m.astype(BF16), upper) + carry)
            carry = carry + jnp.sum(m, axis=-1, keepdims=True)
        return jnp.concatenate(parts, axis=-1)

    sel = gt | (eq & (prefix(eq.astype(F32)) < need))
    pos = prefix(sel.astype(F32))
    slot = jnp.where(sel, pos, -1.0)
    slot_ref[0] = slot.astype(I32)
    for j in range(s // blk):
        slott_ref[0, j * blk:(j + 1) * blk, :] = _dot_nt(eye, slot[:, j * blk:(j + 1) * blk],
                                                        precision=HIGHEST).astype(I32)
    tok = lax.broadcasted_iota(I32, (s, LANES), 0)
    col = lax.broadcasted_iota(I32, (s, LANES), 1)
    before = (tok < col * blk).astype(BF16)
    offs_ref[0] = _dot(sel.astype(BF16), before).astype(I32)


def _topk(afft, cap):
    b, e, s = afft.shape
    blk = min(s, TOKEN_CHUNK)
    slot, slott, offs = pl.pallas_call(
        functools.partial(_topk_kernel, cap=cap, blk=blk),
        grid=(b,),
        in_specs=[pl.BlockSpec((1, e, s), lambda i: (i, 0, 0))],
        out_specs=[pl.BlockSpec((1, e, s), lambda i: (i, 0, 0)),
                   pl.BlockSpec((1, s, e), lambda i: (i, 0, 0)),
                   pl.BlockSpec((1, e, LANES), lambda i: (i, 0, 0))],
        out_shape=[jax.ShapeDtypeStruct((b, e, s), I32), jax.ShapeDtypeStruct((b, s, e), I32),
                   jax.ShapeDtypeStruct((b, e, LANES), I32)],
        name="expert_choice_topk",
    )(afft)
    return slot, slott, offs[:, :, :s // blk + 1]


def _slot_windows(offs_ref, idx, win):
    lo = offs_ref[idx]
    hi = offs_ref[idx + 1]
    first = lo // win
    return first, jnp.where(hi > lo, (hi - 1) // win - first + 1, 0)


def _gather_rows(onehot, h, gate_row):
    picked = _dot(onehot.astype(BF16), h)
    gates = jnp.sum(jnp.where(onehot, gate_row, 0.0), axis=-1, keepdims=True)
    return picked, gates


def _ffn_kernel(offs_ref, slot_ref, afft_ref, h_ref, slotc_ref, cafft_ref, hc_ref, wg_ref, wu_ref, wd_ref,
                ye_ref, yec_ref, wg_s, wu_s, wd_s, xe_s, g_s, *, chunk, group, win, rows):
    ex = pl.program_id(0)
    bi = pl.program_id(1)
    nb = pl.num_programs(1)
    cap = xe_s.shape[0]

    @pl.when(bi == 0)
    def _():
        def cast(i, carry):
            sl = pl.ds(pl.multiple_of(i * rows, rows), rows)
            wg_s[sl, :] = wg_ref[0, 0, sl, :].astype(BF16)
            wu_s[sl, :] = wu_ref[0, 0, sl, :].astype(BF16)
            wd_s[sl, :] = wd_ref[0, 0, sl, :].astype(BF16)
            return carry
        lax.fori_loop(0, wg_s.shape[0] // rows, cast, 0)

    def ffn(xe):
        a = _dot(xe, wg_s[...])
        u = _dot(xe, wu_s[...])
        hid = (a * _sigmoid(a) * u).astype(BF16)
        return _dot(hid, wd_s[...])

    nch = h_ref.shape[1] // chunk
    base = (bi * pl.num_programs(0) + ex) * (nch + 1)
    xe_s[...] = jnp.zeros_like(xe_s)
    g_s[...] = jnp.zeros_like(g_s)
    span = group * chunk
    starts = []
    fits = None
    for p in range(nch // group):
        lo = offs_ref[base + p * group]
        hi = offs_ref[base + (p + 1) * group]
        a = pl.multiple_of(jnp.minimum((lo // F32_SUBLANES) * F32_SUBLANES, cap - win), F32_SUBLANES)
        starts.append(a)
        fits = (hi - a <= win) if fits is None else jnp.logical_and(fits, hi - a <= win)

    def add_window(a, sl, width):
        onehot = (slot_ref[0, pl.ds(ex, 1), sl] - a) == lax.broadcasted_iota(I32, (win, width), 0)
        picked, gates = _gather_rows(onehot, h_ref[0, sl, :], afft_ref[0, pl.ds(ex, 1), sl])
        xe_s[pl.ds(a, win), :] += picked
        g_s[pl.ds(a, win), :] += gates

    @pl.when(fits)
    def _():
        for p, a in enumerate(starts):
            add_window(a, slice(p * span, (p + 1) * span), span)

    @pl.when(jnp.logical_not(fits))
    def _():
        def per_chunk(j, carry):
            first, nwin = _slot_windows(offs_ref, base + j, win)
            sl = pl.ds(pl.multiple_of(j * chunk, chunk), chunk)

            def window(w, c):
                add_window(pl.multiple_of((first + w) * win, win), sl, chunk)
                return c
            lax.fori_loop(0, nwin, window, 0)
            return carry
        lax.fori_loop(0, nch, per_chunk, 0)

    ye_ref[0] = (ffn(xe_s[...].astype(BF16)) * g_s[...]).astype(ye_ref.dtype)

    if yec_ref is not None:
        @pl.when(bi == nb - 1)
        def _():
            rowc = lax.broadcasted_iota(I32, (yec_ref.shape[1], hc_ref.shape[0]), 0)
            picked, gates = _gather_rows(slotc_ref[0] == rowc, hc_ref[...], cafft_ref[0])
            yec_ref[0] = (ffn(picked.astype(BF16)) * gates).astype(yec_ref.dtype)


def _expert_ffn(layer, slot, offs, afft, h2, wg, wu, wd, cap, ctx_part=None):
    b, e, s = slot.shape
    d = h2.shape[2]
    f = wg.shape[3]
    assert f == d
    chunk = min(s, TOKEN_CHUNK)
    nch = s // chunk
    group = 2 if nch % 2 == 0 else 1
    win = min(cap, GATHER_WINDOW)
    assert cap % win == 0
    w_spec = pl.BlockSpec((1, 1, d, f), lambda j, i, o: (layer, j, 0, 0))
    row_spec = pl.BlockSpec((1, e, s), lambda j, i, o: (i, 0, 0))
    in_specs = [row_spec, row_spec, pl.BlockSpec((1, s, d), lambda j, i, o: (i, 0, 0))]
    out_specs = [pl.BlockSpec((1, cap, d), lambda j, i, o: (i * e + j, 0, 0))]
    out_shape = [jax.ShapeDtypeStruct((b * e, cap, d), BF16)]
    args = [slot, afft, h2]
    body = functools.partial(_ffn_kernel, chunk=chunk, group=group, win=win, rows=128)
    if ctx_part is None:
        def kern(offs_ref, slot_ref, afft_ref, h_ref, wg_ref, wu_ref, wd_ref, ye_ref, *scratch):
            body(offs_ref, slot_ref, afft_ref, h_ref, None, None, None, wg_ref, wu_ref, wd_ref, ye_ref, None,
                 *scratch)
    else:
        slot_c, afft_c, h_c, rows_c = ctx_part
        n_c = h_c.shape[0]
        rowc_spec = pl.BlockSpec((1, 1, n_c), lambda j, i, o: (j, 0, 0))
        in_specs += [rowc_spec, rowc_spec, pl.BlockSpec((n_c, d), lambda j, i, o: (0, 0))]
        out_specs.append(pl.BlockSpec((1, rows_c, d), lambda j, i, o: (j, 0, 0)))
        out_shape.append(jax.ShapeDtypeStruct((e, rows_c, d), BF16))
        args += [slot_c, afft_c, h_c]
        kern = body
    outs = pl.pallas_call(
        kern,
        grid_spec=pltpu.PrefetchScalarGridSpec(
            num_scalar_prefetch=1, grid=(e, b),
            in_specs=in_specs + [w_spec, w_spec, w_spec], out_specs=out_specs,
            scratch_shapes=[pltpu.VMEM((d, f), BF16), pltpu.VMEM((d, f), BF16), pltpu.VMEM((f, d), BF16),
                            pltpu.VMEM((cap, d), F32), pltpu.VMEM((cap, 1), F32)]),
        out_shape=out_shape,
        name="expert_ffn",
    )(offs.reshape(-1), *args, wg, wu, wd)
    ye = outs[0].reshape(b, e * cap, d)
    return ye if ctx_part is None else (ye, outs[1])


def _combine_kernel(offs_ref, slott_ref, ye_ref, x_ref, g2_ref, fg_ref, o_ref, acc_s, *, cap, win, final_norm):
    bi = pl.program_id(0)
    tt = acc_s.shape[0]
    n_sub = x_ref.shape[1] // tt
    nch = pl.num_programs(1) * n_sub
    per_block = LANES // win

    for sub in range(n_sub):
        j = pl.program_id(1) * n_sub + sub
        tok = slice(sub * tt, (sub + 1) * tt)

        starts = []
        fits = None
        for ex in range(N_EXPERTS):
            idx = (bi * N_EXPERTS + ex) * (nch + 1) + j
            a = pl.multiple_of(jnp.minimum((offs_ref[idx] // BF16_SUBLANES) * BF16_SUBLANES, cap - win),
                               BF16_SUBLANES)
            starts.append(a)
            ok = offs_ref[idx + 1] - a <= win
            fits = ok if fits is None else jnp.logical_and(fits, ok)

        @pl.when(fits)
        def _(starts=starts, tok=tok):
            lane = lax.broadcasted_iota(I32, (tt, LANES), 1)
            blocks = []
            for blk in range(N_EXPERTS // per_block):
                target = None
                for q in range(per_block):
                    ex = blk * per_block + q
                    t = slott_ref[0, tok, ex:ex + 1] - starts[ex] + q * win
                    target = t if target is None else jnp.where(lane >= q * win, t, target)
                blocks.append((target == lane).astype(BF16))
            rows = [ye_ref[0, pl.ds(ex * cap + starts[ex], win), :] for ex in range(N_EXPERTS)]
            acc_s[...] = _dot(jnp.concatenate(blocks, axis=1), jnp.concatenate(rows, axis=0))

        @pl.when(jnp.logical_not(fits))
        def _(j=j, tok=tok):
            lane = lax.broadcasted_iota(I32, (tt, win), 1)
            acc_s[...] = jnp.zeros_like(acc_s)
            for ex in range(N_EXPERTS):
                first, nwin = _slot_windows(offs_ref, (bi * N_EXPERTS + ex) * (nch + 1) + j, win)

                def window(w, c, ex=ex, first=first):
                    a = pl.multiple_of((first + w) * win, win)
                    onehot = ((slott_ref[0, tok, ex:ex + 1] - a) == lane).astype(BF16)
                    acc_s[...] += _dot(onehot, ye_ref[0, pl.ds(ex * cap + a, win), :])
                    return c
                lax.fori_loop(0, nwin, window, 0)

        out = x_ref[0, tok, :] + g2_ref[0] * acc_s[...]
        if final_norm:
            out = out * lax.rsqrt(jnp.mean(out * out, axis=-1, keepdims=True) + EPS) * fg_ref[...]
        o_ref[0, tok, :] = out


def _combine(slott, offs, ye, x, g2, fg, cap, final_norm):
    b, s, d = x.shape
    e = N_EXPERTS
    chunk = min(s, TOKEN_CHUNK)
    tt = min(s, COMBINE_CHUNKS * chunk)
    win = min(cap, SCATTER_WINDOW)
    assert cap % win == 0 and LANES % win == 0 and e % (LANES // win) == 0 and s % tt == 0
    return pl.pallas_call(
        functools.partial(_combine_kernel, cap=cap, win=win, final_norm=final_norm),
        grid_spec=pltpu.PrefetchScalarGridSpec(
            num_scalar_prefetch=1, grid=(b, s // tt),
            in_specs=[pl.BlockSpec((1, tt, e), lambda i, j, o: (i, j, 0)),
                      pl.BlockSpec((1, e * cap, d), lambda i, j, o: (i, 0, 0)),
                      pl.BlockSpec((1, tt, d), lambda i, j, o: (i, j, 0)),
                      pl.BlockSpec((1, 1, d), lambda i, j, o: (i, 0, 0)),
                      pl.BlockSpec((1, d), lambda i, j, o: (0, 0))],
            out_specs=pl.BlockSpec((1, tt, d), lambda i, j, o: (i, j, 0)),
            scratch_shapes=[pltpu.VMEM((chunk, d), F32)]),
        out_shape=jax.ShapeDtypeStruct((b, s, d), F32),
        name="moe_combine",
    )(offs.reshape(-1), slott, ye, x, g2, fg)


def _split_hi_lo(w):
    return jnp.concatenate(_hi_lo(w), axis=0)


def _permute_in_cols(w):
    o_k = Q_W
    o_v = o_k + KV_W
    o_ca = o_v + KV_W
    o_cg = o_ca + CONV_CH
    o_p = o_cg + CONV_CH
    o_g = o_p + POOL_CH
    return jnp.concatenate([w[:, o_g:], w[:, :o_k], w[:, o_ca:o_cg], w[:, o_cg:o_p], w[:, o_p:o_g],
                            w[:, o_k:o_v], w[:, o_v:o_ca]], axis=1)


def kernel(x, c, ctx, c_ctx, norm1_g, norm2_g, w_mod, b_mod, w_in, attn_sink, w_attn_o, conv_dw, conv_dw_b,
           conv_ln_g, conv_ln_b, w_conv_o, w_pool, pool_scale, w_pool_o, w_out, w_router, w_e_gate, w_e_up,
           w_e_down, final_norm_g):
    b, s, d = x.shape
    l = ctx.shape[1]
    depth = w_in.shape[0]
    assert d == _D_MODEL and w_in.shape[2] == IN_W and CONV_CH == POOL_CH

    tabs = _rope_tables(s)
    cc = jnp.zeros((8, d), F32).at[:b].set(c).at[b].set(c_ctx)
    mod = _modulation(cc, w_mod, b_mod)
    fg = final_norm_g.reshape(1, d)

    for layer in range(depth):
        last = layer == depth - 1
        mx = mod[layer, :b].reshape(b, 1, 6, d)
        sh1, sc1, g1, sh2, sc2, g2 = [mx[:, :, i] for i in range(6)]
        mc = jnp.broadcast_to(mod[layer, b].reshape(1, 1, 6, d), (b, 1, 6, d))
        csh1, csc1, cg1, csh2, csc2, cg2 = [mc[:, :, i] for i in range(6)]
        n1g = norm1_g[layer].reshape(1, d)
        n2g = norm2_g[layer].reshape(1, d)
        w_in_l = _permute_in_cols(w_in[layer]).astype(BF16)
        lw = {'conv_dw': conv_dw[layer], 'conv_dw_b': conv_dw_b[layer].reshape(1, -1),
              'conv_ln_g': conv_ln_g[layer].reshape(1, -1), 'conv_ln_b': conv_ln_b[layer].reshape(1, -1),
              'w_pool': w_pool[layer].astype(BF16), 'pool_scale': pool_scale[layer].reshape(1, -1),
              'w_attn_o': w_attn_o[layer].astype(BF16), 'w_conv_o': w_conv_o[layer].astype(BF16),
              'w_pool_o': w_pool_o[layer].astype(BF16), 'w_out': w_out[layer].astype(BF16),
              'w_router_t': _split_hi_lo(w_router[layer].T)}

        p_x = _inproj(x, n1g, sh1, sc1, w_in_l, tm=512)
        if last:
            p_c = _inproj(ctx, n1g, csh1, csc1, w_in_l[:, WCOL_K:], tm=l, mixer_epilogue=False)
            ctx_kv_cols = (0, KV_W)
        else:
            p_c = _inproj(ctx, n1g, csh1, csc1, w_in_l, tm=l)
            ctx_kv_cols = (COL_K, COL_V)
        attn_x = _window_attention(p_x, p_c, ctx_kv_cols, attn_sink[layer], tabs)
        x_mid, h2, afft = _merge(p_x, attn_x, x, g1, sh2, sc2, n2g, lw, tt=256)
        if not last:
            attn_c = _context_attention(p_c, attn_sink[layer])
            c_mid, ch2, cafft = _merge(p_c, attn_c, ctx, cg1, csh2, csc2, n2g, lw, tt=l)
        cap = (CAPACITY_FACTOR * s) // N_EXPERTS
        slot, slott, offs = _topk(afft, cap)
        if last:
            ye = _expert_ffn(layer, slot, offs, afft, h2, w_e_gate, w_e_up, w_e_down, cap)
        else:
            cap_c = (CAPACITY_FACTOR * l) // N_EXPERTS
            cslot, cslott, coffs = _topk(cafft, cap_c)
            sample_base = (jnp.arange(b, dtype=I32) * cap_c)[:, None, None]
            cslot_all = jnp.where(cslot >= 0, cslot + sample_base, -1)
            cslot_all = cslot_all.transpose(1, 0, 2).reshape(N_EXPERTS, 1, b * l)
            cafft_all = cafft.transpose(1, 0, 2).reshape(N_EXPERTS, 1, b * l)
            ye, yec = _expert_ffn(layer, slot, offs, afft, h2, w_e_gate, w_e_up, w_e_down, cap,
                                  ctx_part=(cslot_all, cafft_all, ch2.reshape(b * l, d), b * cap_c))
            yec = yec.reshape(N_EXPERTS, b, cap_c, d).transpose(1, 0, 2, 3).reshape(b, N_EXPERTS * cap_c, d)
            ctx = _combine(cslott, coffs, yec, c_mid, cg2, fg, cap_c, False)
        x = _combine(slott, offs, ye, x_mid, g2, fg, cap, last)
    return x
```

```python
import functools

import jax
import jax.numpy as jnp
import numpy as np
from jax import lax
from jax.experimental import pallas as pl
from jax.experimental.pallas import tpu as pltpu

F32 = jnp.float32
BF16 = jnp.bfloat16
I32 = jnp.int32

EPS = 1e-6
GRID_W = 64
N_HEADS = 8
N_KV_HEADS = 2
HEAD_DIM = 64
GQA_GROUP = N_HEADS // N_KV_HEADS
WINDOW = 128
ATTN_BLOCK = 128
ROPE_BASE = 10000.0
ROPE_PAIRS = HEAD_DIM // 4
CONV_CH = 512
CONV_K = 31
CONV_PAD = CONV_K // 2
POOL_WINDOWS = (2, 4, 8, 16)
POOL_GROUP = 128
POOL_CH = POOL_GROUP * len(POOL_WINDOWS)
N_EXPERTS = 16
CAPACITY_FACTOR = 2
Q_W = N_HEADS * HEAD_DIM
KV_W = N_KV_HEADS * HEAD_DIM

LANES = 128
HALO = 16
HIGHEST = lax.Precision.HIGHEST
F32_TINY = 2.0 ** -126
F32_MANTISSA_BITS = 23
TOKEN_CHUNK = 256
GATHER_WINDOW = 128
SCATTER_WINDOW = 64
F32_SUBLANES = 8
BF16_SUBLANES = 16
ATTN_LOOKAHEAD = 3
ATTN_BLOCKS_PER_STEP = 4
COMBINE_CHUNKS = 2
INPROJ_TILE = 1024
MERGE_TILE = 512

_D_MODEL = 1024
N_GATE_COLS = 3 * _D_MODEL
WCOL_Q = N_GATE_COLS
WCOL_CONV_A = WCOL_Q + Q_W
WCOL_CONV_G = WCOL_CONV_A + CONV_CH
WCOL_POOL = WCOL_CONV_G + CONV_CH
WCOL_K = WCOL_POOL + POOL_CH
IN_W = WCOL_K + 2 * KV_W
COL_Q = N_GATE_COLS
COL_CONV_U = COL_Q + Q_W
COL_POOL = COL_CONV_U + CONV_CH
COL_K = COL_POOL + POOL_CH
COL_V = COL_K + KV_W
OUT_W = COL_V + KV_W
GATE_CHUNK = 768


def _dot(a, b):
    return jnp.dot(a, b, preferred_element_type=F32)


def _dot_nt(a, b, precision=None):
    return lax.dot_general(a, b, (((1,), (1,)), ((), ())), preferred_element_type=F32, precision=precision)


def _sigmoid(v):
    return 0.5 * jnp.tanh(0.5 * v) + 0.5


def _rms_mod(x, g, sh, sc):
    y = x * lax.rsqrt(jnp.mean(x * x, axis=-1, keepdims=True) + EPS) * g
    return y * (1.0 + sc) + sh


def _hi_lo(v):
    hi = v.astype(BF16)
    return hi, (v - hi.astype(F32)).astype(BF16)


def _mod_kernel(c_ref, w_ref, b_ref, o_ref):
    c = c_ref[...]
    a_hi, a_lo = _hi_lo(c * _sigmoid(c))
    w_hi, w_lo = _hi_lo(w_ref[0])
    o_ref[0] = _dot(a_hi, w_hi) + _dot(a_lo, w_hi) + _dot(a_hi, w_lo) + b_ref[0]


def _modulation(cc, w_mod, b_mod, tn=1536):
    depth, d, n = w_mod.shape
    rows = cc.shape[0]
    return pl.pallas_call(
        _mod_kernel,
        grid=(depth, n // tn),
        in_specs=[pl.BlockSpec((rows, d), lambda l, j: (0, 0)),
                  pl.BlockSpec((1, d, tn), lambda l, j: (l, 0, j)),
                  pl.BlockSpec((1, 1, tn), lambda l, j: (l, 0, j))],
        out_specs=pl.BlockSpec((1, rows, tn), lambda l, j: (l, 0, j)),
        out_shape=jax.ShapeDtypeStruct((depth, rows, n), F32),
        name="modulation",
    )(cc, w_mod, b_mod.reshape(depth, 1, n))


def _inproj_kernel(x_ref, g_ref, sh_ref, sc_ref, w_ref, o_ref, *, mixer_epilogue):
    h = _rms_mod(x_ref[0], g_ref[...], sh_ref[0], sc_ref[0]).astype(BF16)
    if not mixer_epilogue:
        o_ref[0] = _dot(h, w_ref[...]).astype(o_ref.dtype)
        return
    for c0 in range(0, N_GATE_COLS, GATE_CHUNK):
        cols = slice(c0, c0 + GATE_CHUNK)
        o_ref[0, :, cols] = _sigmoid(_dot(h, w_ref[:, cols])).astype(o_ref.dtype)
    o_ref[0, :, COL_Q:COL_CONV_U] = _dot(h, w_ref[:, WCOL_Q:WCOL_CONV_A]).astype(o_ref.dtype)
    glu = _dot(h, w_ref[:, WCOL_CONV_A:WCOL_POOL])
    o_ref[0, :, COL_CONV_U:COL_POOL] = (glu[:, :CONV_CH] * _sigmoid(glu[:, CONV_CH:])).astype(o_ref.dtype)
    o_ref[0, :, COL_POOL:] = _dot(h, w_ref[:, WCOL_POOL:]).astype(o_ref.dtype)


def _inproj(x, g, sh, sc, w, tm, mixer_epilogue=True):
    b, s, d = x.shape
    n = w.shape[1]
    n_out = OUT_W if mixer_epilogue else n
    assert not mixer_epilogue or n == IN_W
    return pl.pallas_call(
        functools.partial(_inproj_kernel, mixer_epilogue=mixer_epilogue),
        grid=(b, s // tm),
        in_specs=[pl.BlockSpec((1, tm, d), lambda i, j: (i, j, 0)),
                  pl.BlockSpec((1, d), lambda i, j: (0, 0)),
                  pl.BlockSpec((1, 1, d), lambda i, j: (i, 0, 0)),
                  pl.BlockSpec((1, 1, d), lambda i, j: (i, 0, 0)),
                  pl.BlockSpec((d, n), lambda i, j: (0, 0))],
        out_specs=pl.BlockSpec((1, tm, n_out), lambda i, j: (i, j, 0)),
        out_shape=jax.ShapeDtypeStruct((b, s, n_out), BF16),
        name="inproj",
    )(x, g, sh, sc, w)


def _rope_tables(s):
    t = np.arange(s)
    row = (t // GRID_W).astype(np.float32)
    col = (t % GRID_W).astype(np.float32)
    freqs = jnp.asarray(ROPE_BASE, F32) ** (-jnp.arange(ROPE_PAIRS, dtype=F32) / ROPE_PAIRS)
    ang_r = jnp.asarray(row)[:, None] * freqs
    ang_c = jnp.asarray(col)[:, None] * freqs
    cos_h = jnp.concatenate([jnp.cos(ang_r), jnp.cos(ang_r), jnp.cos(ang_c), jnp.cos(ang_c)], axis=-1)
    sin_h = jnp.concatenate([-jnp.sin(ang_r), jnp.sin(ang_r), -jnp.sin(ang_c), jnp.sin(ang_c)], axis=-1)
    return jnp.tile(cos_h, (1, LANES // HEAD_DIM)), jnp.tile(sin_h, (1, LANES // HEAD_DIM))


def _rope(x, cos, sin_signed):
    lane = lax.broadcasted_iota(I32, x.shape, 1)
    low = (lane & (2 * ROPE_PAIRS - 1)) < ROPE_PAIRS
    partner = jnp.where(low, pltpu.roll(x, LANES - ROPE_PAIRS, 1), pltpu.roll(x, ROPE_PAIRS, 1))
    return x * cos + partner * sin_signed


def _softmax_pv(s_list, v_list, sink):
    m = sink
    for s in s_list:
        m = jnp.maximum(m, jnp.max(s, axis=-1, keepdims=True))
    denom = jnp.exp(sink - m)
    o = None
    for s, v in zip(s_list, v_list):
        e = jnp.exp(s - m)
        denom = denom + jnp.sum(e, axis=-1, keepdims=True)
        pv = _dot(e.astype(BF16), v)
        o = pv if o is None else o + pv
    return o / denom


def _lane_lo(shape):
    return lax.broadcasted_iota(I32, shape, 1) < HEAD_DIM


def _dup_heads(t):
    swapped = pltpu.roll(t, HEAD_DIM, 1)
    lo = _lane_lo(t.shape)
    return jnp.where(lo, t, swapped), jnp.where(lo, swapped, t)


def _heads_attention(qps, keys, vals, masks, sinks):
    lo = _lane_lo(qps[0].shape)
    keeps = (lo, jnp.logical_not(lo))
    tiles_per_kv = len(qps) // len(keys)

    def head_scores(h):
        qp = qps[h // 2]
        qh = jnp.where(keeps[h % 2], qp, jnp.zeros_like(qp))
        g = h // 2 // tiles_per_kv
        return [sc if mask is None else jnp.where(mask, sc, -1e30)
                for sc, mask in zip([_dot_nt(qh, k) for k in keys[g]], masks[g])]

    n_heads = 2 * len(qps)
    outs = []
    pending = [head_scores(h) for h in range(min(ATTN_LOOKAHEAD, n_heads))]
    for h in range(n_heads):
        if h + ATTN_LOOKAHEAD < n_heads:
            pending.append(head_scores(h + ATTN_LOOKAHEAD))
        outs.append(_softmax_pv(pending.pop(0), vals[h // 2 // tiles_per_kv], sinks[h]))
    return [jnp.where(lo, outs[2 * i], outs[2 * i + 1]) for i in range(len(qps))]


def _store_dup(dst_ref, t):
    d0, d1 = _dup_heads(t)
    dst_ref[0] = d0.astype(dst_ref.dtype)
    dst_ref[1] = d1.astype(dst_ref.dtype)


def _win_attn_kernel(sink_ref, q_ref, k_ref, v_ref, kc_ref, vc_ref, cosq_ref, sinq_ref, cosk_ref, sink_tab_ref,
                     o_ref, kd_ref, vd_ref, kcd_ref, vcd_ref, *, seq):
    i = pl.program_id(1)
    blk = ATTN_BLOCK
    win = 3 * blk

    @pl.when(i == 0)
    def _():
        _store_dup(kd_ref, _rope(k_ref[0].astype(F32), cosk_ref[...], sink_tab_ref[...]))
        _store_dup(vd_ref, v_ref[0].astype(F32))
        _store_dup(kcd_ref, kc_ref[0].astype(F32))
        _store_dup(vcd_ref, vc_ref[0].astype(F32))

    scale = HEAD_DIM ** -0.5
    n_sub = q_ref.shape[1] // blk
    qps, keys, vals, masks = [], [], [], []
    for sub in range(n_sub):
        qb = i * n_sub + sub
        rows = slice(sub * blk, (sub + 1) * blk)
        start = pl.multiple_of(jnp.clip((qb - 1) * blk, 0, seq - win), blk)
        qpos = qb * blk + lax.broadcasted_iota(I32, (blk, win), 0)
        kpos = start + lax.broadcasted_iota(I32, (blk, win), 1)
        mask = jnp.abs(kpos - qpos) <= WINDOW
        cos = cosq_ref[rows, :]
        sin = sinq_ref[rows, :]
        qps += [(_rope(q_ref[0, rows, p * LANES:(p + 1) * LANES].astype(F32), cos, sin) * scale).astype(BF16)
                for p in range(N_HEADS // 2)]
        keys += [[kd_ref[kh, pl.ds(start, win), :], kcd_ref[kh]] for kh in range(N_KV_HEADS)]
        vals += [[vd_ref[kh, pl.ds(start, win), :], vcd_ref[kh]] for kh in range(N_KV_HEADS)]
        masks += [[mask, None]] * N_KV_HEADS
    sinks = [sink_ref[h] for h in range(N_HEADS)] * n_sub
    for t, o in enumerate(_heads_attention(qps, keys, vals, masks, sinks)):
        sub, p = divmod(t, N_HEADS // 2)
        o_ref[0, sub * blk:(sub + 1) * blk, p * LANES:(p + 1) * LANES] = o.astype(o_ref.dtype)


def _window_attention(p_x, p_c, ctx_kv_cols, sink, tabs):
    b, s, _ = p_x.shape
    l = p_c.shape[1]
    cos, sin = tabs
    blk = ATTN_BLOCKS_PER_STEP * ATTN_BLOCK
    assert s % blk == 0
    kcol, vcol = COL_K // KV_W, COL_V // KV_W
    kccol, vccol = ctx_kv_cols[0] // KV_W, ctx_kv_cols[1] // KV_W
    return pl.pallas_call(
        functools.partial(_win_attn_kernel, seq=s),
        grid=(b, s // blk),
        in_specs=[pl.BlockSpec(memory_space=pltpu.SMEM),
                  pl.BlockSpec((1, blk, Q_W), lambda i, j: (i, j, COL_Q // Q_W)),
                  pl.BlockSpec((1, s, KV_W), lambda i, j: (i, 0, kcol)),
                  pl.BlockSpec((1, s, KV_W), lambda i, j: (i, 0, vcol)),
                  pl.BlockSpec((1, l, KV_W), lambda i, j: (i, 0, kccol)),
                  pl.BlockSpec((1, l, KV_W), lambda i, j: (i, 0, vccol)),
                  pl.BlockSpec((blk, LANES), lambda i, j: (j, 0)),
                  pl.BlockSpec((blk, LANES), lambda i, j: (j, 0)),
                  pl.BlockSpec((s, LANES), lambda i, j: (0, 0)),
                  pl.BlockSpec((s, LANES), lambda i, j: (0, 0))],
        out_specs=pl.BlockSpec((1, blk, Q_W), lambda i, j: (i, j, 0)),
        out_shape=jax.ShapeDtypeStruct((b, s, Q_W), BF16),
        scratch_shapes=[pltpu.VMEM((N_KV_HEADS, s, LANES), BF16), pltpu.VMEM((N_KV_HEADS, s, LANES), BF16),
                        pltpu.VMEM((N_KV_HEADS, l, LANES), BF16), pltpu.VMEM((N_KV_HEADS, l, LANES), BF16)],
        name="window_attention",
    )(sink, p_x, p_x, p_x, p_c, p_c, cos, sin, cos, sin)


def _ctx_attn_kernel(sink_ref, q_ref, k_ref, v_ref, o_ref):
    kd = [t.astype(BF16) for t in _dup_heads(k_ref[0].astype(F32))]
    vd = [t.astype(BF16) for t in _dup_heads(v_ref[0].astype(F32))]
    scale = HEAD_DIM ** -0.5
    qps = [(q_ref[0, :, p * LANES:(p + 1) * LANES].astype(F32) * scale).astype(BF16) for p in range(N_HEADS // 2)]
    sinks = [sink_ref[h] for h in range(N_HEADS)]
    outs = _heads_attention(qps, [[k] for k in kd], [[v] for v in vd], [[None]] * N_KV_HEADS, sinks)
    for p, o in enumerate(outs):
        o_ref[0, :, p * LANES:(p + 1) * LANES] = o.astype(o_ref.dtype)


def _context_attention(p_c, sink):
    b, l, _ = p_c.shape
    return pl.pallas_call(
        _ctx_attn_kernel,
        grid=(b,),
        in_specs=[pl.BlockSpec(memory_space=pltpu.SMEM),
                  pl.BlockSpec((1, l, Q_W), lambda i: (i, 0, COL_Q // Q_W)),
                  pl.BlockSpec((1, l, KV_W), lambda i: (i, 0, COL_K // KV_W)),
                  pl.BlockSpec((1, l, KV_W), lambda i: (i, 0, COL_V // KV_W))],
        out_specs=pl.BlockSpec((1, l, Q_W), lambda i: (i, 0, 0)),
        out_shape=jax.ShapeDtypeStruct((b, l, Q_W), BF16),
        name="context_attention",
    )(sink, p_c, p_c, p_c)


def _merge_kernel(ga_ref, gb_ref, gc_ref,
                  u_ref, u_p_ref, u_n_ref, pz_ref, pz_p_ref, pz_n_ref,
                  attn_ref, x_ref, g1_ref, sh2_ref, sc2_ref, n2g_ref,
                  dw_ref, dwb_ref, lng_ref, lnb_ref, wpool_ref, pscale_ref,
                  wa_ref, wb_ref, wc_ref, wo_ref, wrt_ref,
                  xo_ref, h2_ref, afft_ref,
                  uwin_ref, zwin_ref, *, seq):
    t = pl.program_id(1)
    tt = x_ref.shape[1]
    has_prev = (t > 0).astype(F32)
    has_next = (t < pl.num_programs(1) - 1).astype(F32)

    uwin_ref[0:HALO, :] = u_p_ref[0].astype(F32) * has_prev
    uwin_ref[HALO:HALO + tt, :] = u_ref[0].astype(F32)
    uwin_ref[HALO + tt:, :] = u_n_ref[0].astype(F32) * has_next
    first = HALO - CONV_PAD
    rows = tt + 2 * HALO
    acc_cols = []
    for cb in range(CONV_CH // LANES):
        cols = slice(cb * LANES, (cb + 1) * LANES)
        window = uwin_ref[:, cols]
        acc_c = jnp.zeros((tt, LANES), F32) + dwb_ref[:, cols]
        for shift in range(F32_SUBLANES):
            taps = [k for k in range(CONV_K) if (first + k) % F32_SUBLANES == shift]
            if not taps:
                continue
            shifted = window if shift == 0 else pltpu.roll(window, rows - shift, 0)
            for k in taps:
                off = first + k - shift
                acc_c = acc_c + shifted[off:off + tt] * dw_ref[k:k + 1, cols]
        acc_cols.append(acc_c)
    acc = jnp.concatenate(acc_cols, axis=-1)
    mu = jnp.mean(acc, axis=-1, keepdims=True)
    cen = acc - mu
    var = jnp.mean(cen * cen, axis=-1, keepdims=True)
    ln = cen * lax.rsqrt(var + EPS) * lng_ref[...] + lnb_ref[...]
    feat_b = (ln * _sigmoid(ln)).astype(BF16)

    zwin_ref[0:HALO, :] = pz_p_ref[0].astype(F32) * has_prev
    zwin_ref[HALO:HALO + tt, :] = pz_ref[0].astype(F32)
    zwin_ref[HALO + tt:, :] = pz_n_ref[0].astype(F32) * has_next
    tpos = t * tt + lax.broadcasted_iota(I32, (tt, 1), 0)
    pooled = []
    for gi, w in enumerate(POOL_WINDOWS):
        cols = slice(gi * POOL_GROUP, (gi + 1) * POOL_GROUP)
        tot = zwin_ref[pl.ds(HALO - w // 2, tt), cols]
        for d in range(1 - w // 2, w - w // 2):
            tot = tot + zwin_ref[pl.ds(HALO + d, tt), cols]
        cnt = (jnp.minimum(tpos + (w - w // 2), seq) - jnp.maximum(tpos - w // 2, 0)).astype(F32)
        diff = tot / cnt - zwin_ref[pl.ds(HALO, tt), cols]
        pooled.append(_dot(diff.astype(BF16), wpool_ref[gi]))
    feat_c = (jnp.concatenate(pooled, axis=-1) * pscale_ref[...]).astype(BF16)

    y_a = _dot(attn_ref[0], wa_ref[...])
    y_b = _dot(feat_b, wb_ref[...])
    y_c = _dot(feat_c, wc_ref[...])
    merged = ga_ref[0].astype(F32) * y_a + gb_ref[0].astype(F32) * y_b + gc_ref[0].astype(F32) * y_c
    xn = x_ref[0] + g1_ref[0] * _dot(merged.astype(BF16), wo_ref[...])
    xo_ref[0] = xn

    h2 = _rms_mod(xn, n2g_ref[...], sh2_ref[0], sc2_ref[0])
    h2_hi, h2_lo = _hi_lo(h2)
    h2_ref[0] = h2_hi
    ne = afft_ref.shape[1]
    by_hi = _dot_nt(wrt_ref[...], h2_hi)
    logits_t = by_hi[:ne] + by_hi[ne:] + _dot_nt(wrt_ref[:ne, :], h2_lo)
    et = jnp.exp(logits_t - jnp.max(logits_t, axis=0, keepdims=True))
    afft_ref[0] = et / jnp.sum(et, axis=0, keepdims=True)


def _merge(p, attn, x, g1, sh2, sc2, n2g, lw, tt):
    b, s, d = x.shape
    nh = tt // HALO
    last_h = s // HALO - 1
    e = N_EXPERTS

    def main(width, col):
        return pl.BlockSpec((1, tt, width), lambda i, j: (i, j, col))

    def prev(col):
        return pl.BlockSpec((1, HALO, CONV_CH), lambda i, j: (i, jnp.maximum(j * nh - 1, 0), col))

    def nxt(col):
        return pl.BlockSpec((1, HALO, CONV_CH), lambda i, j: (i, jnp.minimum((j + 1) * nh, last_h), col))

    def per_batch():
        return pl.BlockSpec((1, 1, d), lambda i, j: (i, 0, 0))

    def const(shape):
        return pl.BlockSpec(shape, lambda i, j: (0,) * len(shape))

    cu, pz = COL_CONV_U // CONV_CH, COL_POOL // CONV_CH
    in_specs = [main(d, 0), main(d, 1), main(d, 2),
                main(CONV_CH, cu), prev(cu), nxt(cu), main(POOL_CH, pz), prev(pz), nxt(pz),
                pl.BlockSpec((1, tt, Q_W), lambda i, j: (i, j, 0)),
                pl.BlockSpec((1, tt, d), lambda i, j: (i, j, 0)),
                per_batch(), per_batch(), per_batch(), const((1, d)),
                const((CONV_K, CONV_CH)), const((1, CONV_CH)), const((1, CONV_CH)), const((1, CONV_CH)),
                const((len(POOL_WINDOWS), POOL_GROUP, POOL_GROUP)), const((1, POOL_CH)),
                const((Q_W, d)), const((CONV_CH, d)), const((POOL_CH, d)), const((d, d)),
                const((2 * e, d))]
    out_specs = [pl.BlockSpec((1, tt, d), lambda i, j: (i, j, 0)),
                 pl.BlockSpec((1, tt, d), lambda i, j: (i, j, 0)),
                 pl.BlockSpec((1, e, tt), lambda i, j: (i, 0, j))]
    out_shape = [jax.ShapeDtypeStruct((b, s, d), F32), jax.ShapeDtypeStruct((b, s, d), BF16),
                 jax.ShapeDtypeStruct((b, e, s), F32)]
    return pl.pallas_call(
        functools.partial(_merge_kernel, seq=s),
        grid=(b, s // tt),
        in_specs=in_specs, out_specs=out_specs, out_shape=out_shape,
        scratch_shapes=[pltpu.VMEM((tt + 2 * HALO, CONV_CH), F32), pltpu.VMEM((tt + 2 * HALO, POOL_CH), F32)],
        name="mix_merge",
    )(p, p, p, p, p, p, p, p, p, attn, x, g1, sh2, sc2, n2g,
      lw['conv_dw'], lw['conv_dw_b'], lw['conv_ln_g'], lw['conv_ln_b'], lw['w_pool'], lw['pool_scale'],
      lw['w_attn_o'], lw['w_conv_o'], lw['w_pool_o'], lw['w_out'], lw['w_router_t'])


def _topk_kernel(afft_ref, slot_ref, slott_ref, offs_ref, *, cap, blk):
    a = afft_ref[0]
    e, s = a.shape

    def keeps_cap(cand):
        return jnp.sum((a >= cand).astype(F32), axis=-1, keepdims=True) >= cap

    tiny = jnp.full((e, 1), F32_TINY, F32)
    thr = jnp.where(keeps_cap(tiny), tiny, 0.0)
    for step in (64, 32, 16, 8, 4, 2, 1):
        cand = thr * float(2 ** step)
        thr = jnp.where(keeps_cap(cand), cand, thr)
    delta = thr
    for _ in range(F32_MANTISSA_BITS):
        delta = delta * 0.5
        cand = thr + delta
        thr = jnp.where(keeps_cap(cand), cand, thr)
    gt = a > thr
    eq = a == thr
    need = cap - jnp.sum(gt.astype(F32), axis=-1, keepdims=True)

    r = lax.broadcasted_iota(I32, (blk, blk), 0)
    c = lax.broadcasted_iota(I32, (blk, blk), 1)
    upper = (r < c).astype(BF16)
    eye = (r == c).astype(F32)

    def prefix(mask_f32):
        carry = jnp.zeros((e, 1), F32)
        parts = []
        for j in range(s // blk):
            m = mask_f32[:, j * blk:(j + 1) * blk]
            parts.append(_dot(m.astype(BF16), upper) + carry)
            carry = carry + jnp.sum(m, axis=-1, keepdims=True)
        return jnp.concatenate(parts, axis=-1)

    sel = gt | (eq & (prefix(eq.astype(F32)) < need))
    pos = prefix(sel.astype(F32))
    slot = jnp.where(sel, pos, -1.0)
    slot_ref[0] = slot.astype(I32)
    for j in range(s // blk):
        slott_ref[0, j * blk:(j + 1) * blk, :] = _dot_nt(eye, slot[:, j * blk:(j + 1) * blk],
                                                        precision=HIGHEST).astype(I32)
    tok = lax.broadcasted_iota(I32, (s, LANES), 0)
    col = lax.broadcasted_iota(I32, (s, LANES), 1)
    before = (tok < col * blk).astype(BF16)
    offs_ref[0] = _dot(sel.astype(BF16), before).astype(I32)


def _topk(afft, cap):
    b, e, s = afft.shape
    blk = min(s, TOKEN_CHUNK)
    slot, slott, offs = pl.pallas_call(
        functools.partial(_topk_kernel, cap=cap, blk=blk),
        grid=(b,),
        in_specs=[pl.BlockSpec((1, e, s), lambda i: (i, 0, 0))],
        out_specs=[pl.BlockSpec((1, e, s), lambda i: (i, 0, 0)),
                   pl.BlockSpec((1, s, e), lambda i: (i, 0, 0)),
                   pl.BlockSpec((1, e, LANES), lambda i: (i, 0, 0))],
        out_shape=[jax.ShapeDtypeStruct((b, e, s), I32), jax.ShapeDtypeStruct((b, s, e), I32),
                   jax.ShapeDtypeStruct((b, e, LANES), I32)],
        name="expert_choice_topk",
    )(afft)
    return slot, slott, offs[:, :, :s // blk + 1]


def _slot_windows(offs_ref, idx, win):
    lo = offs_ref[idx]
    hi = offs_ref[idx + 1]
    first = lo // win
    return first, jnp.where(hi > lo, (hi - 1) // win - first + 1, 0)


def _gather_rows(onehot, h, gate_row):
    picked = _dot(onehot.astype(BF16), h)
    gates = jnp.sum(jnp.where(onehot, gate_row, 0.0), axis=-1, keepdims=True)
    return picked, gates


def _ffn_kernel(offs_ref, slot_ref, afft_ref, h_ref, slotc_ref, cafft_ref, hc_ref, wg_ref, wu_ref, wd_ref,
                ye_ref, yec_ref, wg_s, wu_s, wd_s, xe_s, g_s, *, chunk, group, win, rows):
    ex = pl.program_id(0)
    bi = pl.program_id(1)
    nb = pl.num_programs(1)
    cap = xe_s.shape[0]

    @pl.when(bi == 0)
    def _():
        def cast(i, carry):
            sl = pl.ds(pl.multiple_of(i * rows, rows), rows)
            wg_s[sl, :] = wg_ref[0, 0, sl, :].astype(BF16)
            wu_s[sl, :] = wu_ref[0, 0, sl, :].astype(BF16)
            wd_s[sl, :] = wd_ref[0, 0, sl, :].astype(BF16)
            return carry
        lax.fori_loop(0, wg_s.shape[0] // rows, cast, 0)

    def ffn(xe):
        a = _dot(xe, wg_s[...])
        u = _dot(xe, wu_s[...])
        hid = (a * _sigmoid(a) * u).astype(BF16)
        return _dot(hid, wd_s[...])

    nch = h_ref.shape[1] // chunk
    base = (bi * pl.num_programs(0) + ex) * (nch + 1)
    xe_s[...] = jnp.zeros_like(xe_s)
    g_s[...] = jnp.zeros_like(g_s)
    span = group * chunk
    starts = []
    fits = None
    for p in range(nch // group):
        lo = offs_ref[base + p * group]
        hi = offs_ref[base + (p + 1) * group]
        a = pl.multiple_of(jnp.minimum((lo // F32_SUBLANES) * F32_SUBLANES, cap - win), F32_SUBLANES)
        starts.append(a)
        fits = (hi - a <= win) if fits is None else jnp.logical_and(fits, hi - a <= win)

    def add_window(a, sl, width):
        onehot = (slot_ref[0, pl.ds(ex, 1), sl] - a) == lax.broadcasted_iota(I32, (win, width), 0)
        picked, gates = _gather_rows(onehot, h_ref[0, sl, :], afft_ref[0, pl.ds(ex, 1), sl])
        xe_s[pl.ds(a, win), :] += picked
        g_s[pl.ds(a, win), :] += gates

    @pl.when(fits)
    def _():
        for p, a in enumerate(starts):
            add_window(a, slice(p * span, (p + 1) * span), span)

    @pl.when(jnp.logical_not(fits))
    def _():
        def per_chunk(j, carry):
            first, nwin = _slot_windows(offs_ref, base + j, win)
            sl = pl.ds(pl.multiple_of(j * chunk, chunk), chunk)

            def window(w, c):
                add_window(pl.multiple_of((first + w) * win, win), sl, chunk)
                return c
            lax.fori_loop(0, nwin, window, 0)
            return carry
        lax.fori_loop(0, nch, per_chunk, 0)

    ye_ref[0] = (ffn(xe_s[...].astype(BF16)) * g_s[...]).astype(ye_ref.dtype)

    if yec_ref is not None:
        @pl.when(bi == nb - 1)
        def _():
            rowc = lax.broadcasted_iota(I32, (yec_ref.shape[1], hc_ref.shape[0]), 0)
            picked, gates = _gather_rows(slotc_ref[0] == rowc, hc_ref[...], cafft_ref[0])
            yec_ref[0] = (ffn(picked.astype(BF16)) * gates).astype(yec_ref.dtype)


def _expert_ffn(layer, slot, offs, afft, h2, wg, wu, wd, cap, ctx_part=None):
    b, e, s = slot.shape
    d = h2.shape[2]
    f = wg.shape[3]
    assert f == d
    chunk = min(s, TOKEN_CHUNK)
    nch = s // chunk
    group = 2 if nch % 2 == 0 else 1
    win = min(cap, GATHER_WINDOW)
    assert cap % win == 0
    w_spec = pl.BlockSpec((1, 1, d, f), lambda j, i, o: (layer, j, 0, 0))
    row_spec = pl.BlockSpec((1, e, s), lambda j, i, o: (i, 0, 0))
    in_specs = [row_spec, row_spec, pl.BlockSpec((1, s, d), lambda j, i, o: (i, 0, 0))]
    out_specs = [pl.BlockSpec((1, cap, d), lambda j, i, o: (i * e + j, 0, 0))]
    out_shape = [jax.ShapeDtypeStruct((b * e, cap, d), BF16)]
    args = [slot, afft, h2]
    body = functools.partial(_ffn_kernel, chunk=chunk, group=group, win=win, rows=128)
    if ctx_part is None:
        def kern(offs_ref, slot_ref, afft_ref, h_ref, wg_ref, wu_ref, wd_ref, ye_ref, *scratch):
            body(offs_ref, slot_ref, afft_ref, h_ref, None, None, None, wg_ref, wu_ref, wd_ref, ye_ref, None,
                 *scratch)
    else:
        slot_c, afft_c, h_c, rows_c = ctx_part
        n_c = h_c.shape[0]
        rowc_spec = pl.BlockSpec((1, 1, n_c), lambda j, i, o: (j, 0, 0))
        in_specs += [rowc_spec, rowc_spec, pl.BlockSpec((n_c, d), lambda j, i, o: (0, 0))]
        out_specs.append(pl.BlockSpec((1, rows_c, d), lambda j, i, o: (j, 0, 0)))
        out_shape.append(jax.ShapeDtypeStruct((e, rows_c, d), BF16))
        args += [slot_c, afft_c, h_c]
        kern = body
    outs = pl.pallas_call(
        kern,
        grid_spec=pltpu.PrefetchScalarGridSpec(
            num_scalar_prefetch=1, grid=(e, b),
            in_specs=in_specs + [w_spec, w_spec, w_spec], out_specs=out_specs,
            scratch_shapes=[pltpu.VMEM((d, f), BF16), pltpu.VMEM((d, f), BF16), pltpu.VMEM((f, d), BF16),
                            pltpu.VMEM((cap, d), F32), pltpu.VMEM((cap, 1), F32)]),
        out_shape=out_shape,
        name="expert_ffn",
    )(offs.reshape(-1), *args, wg, wu, wd)
    ye = outs[0].reshape(b, e * cap, d)
    return ye if ctx_part is None else (ye, outs[1])


def _combine_kernel(offs_ref, slott_ref, ye_ref, x_ref, g2_ref, fg_ref, o_ref, acc_s, *, cap, win, final_norm):
    bi = pl.program_id(0)
    tt = acc_s.shape[0]
    n_sub = x_ref.shape[1] // tt
    nch = pl.num_programs(1) * n_sub
    per_block = LANES // win

    for sub in range(n_sub):
        j = pl.program_id(1) * n_sub + sub
        tok = slice(sub * tt, (sub + 1) * tt)

        starts = []
        fits = None
        for ex in range(N_EXPERTS):
            idx = (bi * N_EXPERTS + ex) * (nch + 1) + j
            a = pl.multiple_of(jnp.minimum((offs_ref[idx] // BF16_SUBLANES) * BF16_SUBLANES, cap - win),
                               BF16_SUBLANES)
            starts.append(a)
            ok = offs_ref[idx + 1] - a <= win
            fits = ok if fits is None else jnp.logical_and(fits, ok)

        @pl.when(fits)
        def _(starts=starts, tok=tok):
            lane = lax.broadcasted_iota(I32, (tt, LANES), 1)
            blocks = []
            for blk in range(N_EXPERTS // per_block):
                target = None
                for q in range(per_block):
                    ex = blk * per_block + q
                    t = slott_ref[0, tok, ex:ex + 1] - starts[ex] + q * win
                    target = t if target is None else jnp.where(lane >= q * win, t, target)
                blocks.append((target == lane).astype(BF16))
            rows = [ye_ref[0, pl.ds(ex * cap + starts[ex], win), :] for ex in range(N_EXPERTS)]
            acc_s[...] = _dot(jnp.concatenate(blocks, axis=1), jnp.concatenate(rows, axis=0))

        @pl.when(jnp.logical_not(fits))
        def _(j=j, tok=tok):
            lane = lax.broadcasted_iota(I32, (tt, win), 1)
            acc_s[...] = jnp.zeros_like(acc_s)
            for ex in range(N_EXPERTS):
                first, nwin = _slot_windows(offs_ref, (bi * N_EXPERTS + ex) * (nch + 1) + j, win)

                def window(w, c, ex=ex, first=first):
                    a = pl.multiple_of((first + w) * win, win)
                    onehot = ((slott_ref[0, tok, ex:ex + 1] - a) == lane).astype(BF16)
                    acc_s[...] += _dot(onehot, ye_ref[0, pl.ds(ex * cap + a, win), :])
                    return c
                lax.fori_loop(0, nwin, window, 0)

        out = x_ref[0, tok, :] + g2_ref[0] * acc_s[...]
        if final_norm:
            out = out * lax.rsqrt(jnp.mean(out * out, axis=-1, keepdims=True) + EPS) * fg_ref[...]
        o_ref[0, tok, :] = out


def _combine(slott, offs, ye, x, g2, fg, cap, final_norm):
    b, s, d = x.shape
    e = N_EXPERTS
    chunk = min(s, TOKEN_CHUNK)
    tt = min(s, COMBINE_CHUNKS * chunk)
    win = min(cap, SCATTER_WINDOW)
    assert cap % win == 0 and LANES % win == 0 and e % (LANES // win) == 0 and s % tt == 0
    return pl.pallas_call(
        functools.partial(_combine_kernel, cap=cap, win=win, final_norm=final_norm),
        grid_spec=pltpu.PrefetchScalarGridSpec(
            num_scalar_prefetch=1, grid=(b, s // tt),
            in_specs=[pl.BlockSpec((1, tt, e), lambda i, j, o: (i, j, 0)),
                      pl.BlockSpec((1, e * cap, d), lambda i, j, o: (i, 0, 0)),
                      pl.BlockSpec((1, tt, d), lambda i, j, o: (i, j, 0)),
                      pl.BlockSpec((1, 1, d), lambda i, j, o: (i, 0, 0)),
                      pl.BlockSpec((1, d), lambda i, j, o: (0, 0))],
            out_specs=pl.BlockSpec((1, tt, d), lambda i, j, o: (i, j, 0)),
            scratch_shapes=[pltpu.VMEM((chunk, d), F32)]),
        out_shape=jax.ShapeDtypeStruct((b, s, d), F32),
        name="moe_combine",
    )(offs.reshape(-1), slott, ye, x, g2, fg)


def _split_hi_lo(w):
    return jnp.concatenate(_hi_lo(w), axis=0)


def _permute_in_cols(w):
    o_k = Q_W
    o_v = o_k + KV_W
    o_ca = o_v + KV_W
    o_cg = o_ca + CONV_CH
    o_p = o_cg + CONV_CH
    o_g = o_p + POOL_CH
    return jnp.concatenate([w[:, o_g:], w[:, :o_k], w[:, o_ca:o_cg], w[:, o_cg:o_p], w[:, o_p:o_g],
                            w[:, o_k:o_v], w[:, o_v:o_ca]], axis=1)


def kernel(x, c, ctx, c_ctx, norm1_g, norm2_g, w_mod, b_mod, w_in, attn_sink, w_attn_o, conv_dw, conv_dw_b,
           conv_ln_g, conv_ln_b, w_conv_o, w_pool, pool_scale, w_pool_o, w_out, w_router, w_e_gate, w_e_up,
           w_e_down, final_norm_g):
    b, s, d = x.shape
    l = ctx.shape[1]
    depth = w_in.shape[0]
    assert d == _D_MODEL and w_in.shape[2] == IN_W and CONV_CH == POOL_CH

    tabs = _rope_tables(s)
    cc = jnp.zeros((8, d), F32).at[:b].set(c).at[b].set(c_ctx)
    mod = _modulation(cc, w_mod, b_mod)
    fg = final_norm_g.reshape(1, d)

    for layer in range(depth):
        last = layer == depth - 1
        mx = mod[layer, :b].reshape(b, 1, 6, d)
        sh1, sc1, g1, sh2, sc2, g2 = [mx[:, :, i] for i in range(6)]
        mc = jnp.broadcast_to(mod[layer, b].reshape(1, 1, 6, d), (b, 1, 6, d))
        csh1, csc1, cg1, csh2, csc2, cg2 = [mc[:, :, i] for i in range(6)]
        n1g = norm1_g[layer].reshape(1, d)
        n2g = norm2_g[layer].reshape(1, d)
        w_in_l = _permute_in_cols(w_in[layer]).astype(BF16)
        lw = {'conv_dw': conv_dw[layer], 'conv_dw_b': conv_dw_b[layer].reshape(1, -1),
              'conv_ln_g': conv_ln_g[layer].reshape(1, -1), 'conv_ln_b': conv_ln_b[layer].reshape(1, -1),
              'w_pool': w_pool[layer].astype(BF16), 'pool_scale': pool_scale[layer].reshape(1, -1),
              'w_attn_o': w_attn_o[layer].astype(BF16), 'w_conv_o': w_conv_o[layer].astype(BF16),
              'w_pool_o': w_pool_o[layer].astype(BF16), 'w_out': w_out[layer].astype(BF16),
              'w_router_t': _split_hi_lo(w_router[layer].T)}

        p_x = _inproj(x, n1g, sh1, sc1, w_in_l, tm=min(s, INPROJ_TILE))
        if last:
            p_c = _inproj(ctx, n1g, csh1, csc1, w_in_l[:, WCOL_K:], tm=l, mixer_epilogue=False)
            ctx_kv_cols = (0, KV_W)
        else:
            p_c = _inproj(ctx, n1g, csh1, csc1, w_in_l, tm=l)
            ctx_kv_cols = (COL_K, COL_V)
        attn_x = _window_attention(p_x, p_c, ctx_kv_cols, attn_sink[layer], tabs)
        x_mid, h2, afft = _merge(p_x, attn_x, x, g1, sh2, sc2, n2g, lw, tt=min(s, MERGE_TILE))
        if not last:
            attn_c = _context_attention(p_c, attn_sink[layer])
            c_mid, ch2, cafft = _merge(p_c, attn_c, ctx, cg1, csh2, csc2, n2g, lw, tt=l)
        cap = (CAPACITY_FACTOR * s) // N_EXPERTS
        slot, slott, offs = _topk(afft, cap)
        if last:
            ye = _expert_ffn(layer, slot, offs, afft, h2, w_e_gate, w_e_up, w_e_down, cap)
        else:
            cap_c = (CAPACITY_FACTOR * l) // N_EXPERTS
            cslot, cslott, coffs = _topk(cafft, cap_c)
            sample_base = (jnp.arange(b, dtype=I32) * cap_c)[:, None, None]
            cslot_all = jnp.where(cslot >= 0, cslot + sample_base, -1)
            cslot_all = cslot_all.transpose(1, 0, 2).reshape(N_EXPERTS, 1, b * l)
            cafft_all = cafft.transpose(1, 0, 2).reshape(N_EXPERTS, 1, b * l)
            ye, yec = _expert_ffn(layer, slot, offs, afft, h2, w_e_gate, w_e_up, w_e_down, cap,
                                  ctx_part=(cslot_all, cafft_all, ch2.reshape(b * l, d), b * cap_c))
            yec = yec.reshape(N_EXPERTS, b, cap_c, d).transpose(1, 0, 2, 3).reshape(b, N_EXPERTS * cap_c, d)
            ctx = _combine(cslott, coffs, yec, c_mid, cg2, fg, cap_c, False)
        x = _combine(slott, offs, ye, x_mid, g2, fg, cap, last)
    return x
```

```python
import functools

import jax
import jax.numpy as jnp
import numpy as np
from jax import lax
from jax.experimental import pallas as pl
from jax.experimental.pallas import tpu as pltpu

F32 = jnp.float32
BF16 = jnp.bfloat16
I32 = jnp.int32

EPS = 1e-6
GRID_W = 64
N_HEADS = 8
N_KV_HEADS = 2
HEAD_DIM = 64
GQA_GROUP = N_HEADS // N_KV_HEADS
WINDOW = 128
ATTN_BLOCK = 128
ROPE_BASE = 10000.0
ROPE_PAIRS = HEAD_DIM // 4
CONV_CH = 512
CONV_K = 31
CONV_PAD = CONV_K // 2
POOL_WINDOWS = (2, 4, 8, 16)
POOL_GROUP = 128
POOL_CH = POOL_GROUP * len(POOL_WINDOWS)
N_EXPERTS = 16
CAPACITY_FACTOR = 2
Q_W = N_HEADS * HEAD_DIM
KV_W = N_KV_HEADS * HEAD_DIM

LANES = 128
HALO = 16
HIGHEST = lax.Precision.HIGHEST
LOG2_E = 1.4426950408889634
F32_TINY = 2.0 ** -126
F32_MANTISSA_BITS = 23
TOKEN_CHUNK = 256
GATHER_WINDOW = 128
SCATTER_WINDOW = 64
F32_SUBLANES = 8
BF16_SUBLANES = 16
ATTN_LOOKAHEAD = 3
ATTN_BLOCKS_PER_STEP = 8
COMBINE_CHUNKS = 4
INPROJ_TILE = 1024
MERGE_TILE = 512

_D_MODEL = 1024
N_GATE_COLS = 3 * _D_MODEL
WCOL_Q = N_GATE_COLS
WCOL_CONV_A = WCOL_Q + Q_W
WCOL_CONV_G = WCOL_CONV_A + CONV_CH
WCOL_POOL = WCOL_CONV_G + CONV_CH
WCOL_K = WCOL_POOL + POOL_CH
IN_W = WCOL_K + 2 * KV_W
COL_Q = N_GATE_COLS
COL_CONV_U = COL_Q + Q_W
COL_POOL = COL_CONV_U + CONV_CH
COL_K = COL_POOL + POOL_CH
COL_V = COL_K + KV_W
OUT_W = COL_V + KV_W
GATE_CHUNK = 768


def _dot(a, b):
    return jnp.dot(a, b, preferred_element_type=F32)


def _dot_nt(a, b, precision=None):
    return lax.dot_general(a, b, (((1,), (1,)), ((), ())), preferred_element_type=F32, precision=precision)


def _sigmoid(v):
    return 0.5 * jnp.tanh(0.5 * v) + 0.5


def _rms_mod(x, g, sh, sc):
    y = x * lax.rsqrt(jnp.mean(x * x, axis=-1, keepdims=True) + EPS) * g
    return y * (1.0 + sc) + sh


def _hi_lo(v):
    hi = v.astype(BF16)
    return hi, (v - hi.astype(F32)).astype(BF16)


def _mod_kernel(c_ref, w_ref, b_ref, o_ref):
    c = c_ref[...]
    a_hi, a_lo = _hi_lo(c * _sigmoid(c))
    w_hi, w_lo = _hi_lo(w_ref[0])
    o_ref[0] = _dot(a_hi, w_hi) + _dot(a_lo, w_hi) + _dot(a_hi, w_lo) + b_ref[0]


def _modulation(cc, w_mod, b_mod, tn=1536):
    depth, d, n = w_mod.shape
    rows = cc.shape[0]
    return pl.pallas_call(
        _mod_kernel,
        grid=(depth, n // tn),
        in_specs=[pl.BlockSpec((rows, d), lambda l, j: (0, 0)),
                  pl.BlockSpec((1, d, tn), lambda l, j: (l, 0, j)),
                  pl.BlockSpec((1, 1, tn), lambda l, j: (l, 0, j))],
        out_specs=pl.BlockSpec((1, rows, tn), lambda l, j: (l, 0, j)),
        out_shape=jax.ShapeDtypeStruct((depth, rows, n), F32),
        name="modulation",
    )(cc, w_mod, b_mod.reshape(depth, 1, n))


def _inproj_kernel(x_ref, g_ref, sh_ref, sc_ref, w_ref, o_ref, *, mixer_epilogue):
    h = _rms_mod(x_ref[0], g_ref[...], sh_ref[0], sc_ref[0]).astype(BF16)
    if not mixer_epilogue:
        o_ref[0] = _dot(h, w_ref[...]).astype(o_ref.dtype)
        return
    for c0 in range(0, N_GATE_COLS, GATE_CHUNK):
        cols = slice(c0, c0 + GATE_CHUNK)
        o_ref[0, :, cols] = _sigmoid(_dot(h, w_ref[:, cols])).astype(o_ref.dtype)
    o_ref[0, :, COL_Q:COL_CONV_U] = _dot(h, w_ref[:, WCOL_Q:WCOL_CONV_A]).astype(o_ref.dtype)
    glu = _dot(h, w_ref[:, WCOL_CONV_A:WCOL_POOL])
    o_ref[0, :, COL_CONV_U:COL_POOL] = (glu[:, :CONV_CH] * _sigmoid(glu[:, CONV_CH:])).astype(o_ref.dtype)
    o_ref[0, :, COL_POOL:] = _dot(h, w_ref[:, WCOL_POOL:]).astype(o_ref.dtype)


def _inproj(x, g, sh, sc, w, tm, mixer_epilogue=True):
    b, s, d = x.shape
    n = w.shape[1]
    n_out = OUT_W if mixer_epilogue else n
    assert not mixer_epilogue or n == IN_W
    return pl.pallas_call(
        functools.partial(_inproj_kernel, mixer_epilogue=mixer_epilogue),
        grid=(b, s // tm),
        in_specs=[pl.BlockSpec((1, tm, d), lambda i, j: (i, j, 0)),
                  pl.BlockSpec((1, d), lambda i, j: (0, 0)),
                  pl.BlockSpec((1, 1, d), lambda i, j: (i, 0, 0)),
                  pl.BlockSpec((1, 1, d), lambda i, j: (i, 0, 0)),
                  pl.BlockSpec((d, n), lambda i, j: (0, 0))],
        out_specs=pl.BlockSpec((1, tm, n_out), lambda i, j: (i, j, 0)),
        out_shape=jax.ShapeDtypeStruct((b, s, n_out), BF16),
        name="inproj",
    )(x, g, sh, sc, w)


def _rope_tables(s):
    t = np.arange(s)
    row = (t // GRID_W).astype(np.float32)
    col = (t % GRID_W).astype(np.float32)
    freqs = jnp.asarray(ROPE_BASE, F32) ** (-jnp.arange(ROPE_PAIRS, dtype=F32) / ROPE_PAIRS)
    ang_r = jnp.asarray(row)[:, None] * freqs
    ang_c = jnp.asarray(col)[:, None] * freqs
    cos_h = jnp.concatenate([jnp.cos(ang_r), jnp.cos(ang_r), jnp.cos(ang_c), jnp.cos(ang_c)], axis=-1)
    sin_h = jnp.concatenate([-jnp.sin(ang_r), jnp.sin(ang_r), -jnp.sin(ang_c), jnp.sin(ang_c)], axis=-1)
    return jnp.tile(cos_h, (1, LANES // HEAD_DIM)), jnp.tile(sin_h, (1, LANES // HEAD_DIM))


def _rope(x, cos, sin_signed):
    lane = lax.broadcasted_iota(I32, x.shape, 1)
    low = (lane & (2 * ROPE_PAIRS - 1)) < ROPE_PAIRS
    partner = jnp.where(low, pltpu.roll(x, LANES - ROPE_PAIRS, 1), pltpu.roll(x, ROPE_PAIRS, 1))
    return x * cos + partner * sin_signed


def _softmax_pv(s_list, v_list, sink):
    m = sink
    for s in s_list:
        m = jnp.maximum(m, jnp.max(s, axis=-1, keepdims=True))
    denom = jnp.exp2(sink - m)
    o = None
    for s, v in zip(s_list, v_list):
        e = jnp.exp2(s - m)
        denom = denom + jnp.sum(e, axis=-1, keepdims=True)
        pv = _dot(e.astype(BF16), v)
        o = pv if o is None else o + pv
    return o / denom


def _lane_lo(shape):
    return lax.broadcasted_iota(I32, shape, 1) < HEAD_DIM


def _dup_heads(t):
    swapped = pltpu.roll(t, HEAD_DIM, 1)
    lo = _lane_lo(t.shape)
    return jnp.where(lo, t, swapped), jnp.where(lo, swapped, t)


def _heads_attention(qps, keys, vals, masks, sinks):
    lo = _lane_lo(qps[0].shape)
    keeps = (lo, jnp.logical_not(lo))
    tiles_per_kv = len(qps) // len(keys)

    def head_scores(h):
        qp = qps[h // 2]
        qh = jnp.where(keeps[h % 2], qp, jnp.zeros_like(qp))
        g = h // 2 // tiles_per_kv
        return [sc if mask is None else jnp.where(mask, sc, -1e30)
                for sc, mask in zip([_dot_nt(qh, k) for k in keys[g]], masks[g])]

    n_heads = 2 * len(qps)
    outs = []
    pending = [head_scores(h) for h in range(min(ATTN_LOOKAHEAD, n_heads))]
    for h in range(n_heads):
        if h + ATTN_LOOKAHEAD < n_heads:
            pending.append(head_scores(h + ATTN_LOOKAHEAD))
        outs.append(_softmax_pv(pending.pop(0), vals[h // 2 // tiles_per_kv], sinks[h]))
    return [jnp.where(lo, outs[2 * i], outs[2 * i + 1]) for i in range(len(qps))]


def _store_dup(dst_ref, t):
    d0, d1 = _dup_heads(t)
    dst_ref[0] = d0.astype(dst_ref.dtype)
    dst_ref[1] = d1.astype(dst_ref.dtype)


def _win_attn_kernel(sink_ref, q_ref, k_ref, v_ref, kc_ref, vc_ref, cosq_ref, sinq_ref, cosk_ref, sink_tab_ref,
                     o_ref, kd_ref, vd_ref, kcd_ref, vcd_ref, *, seq):
    i = pl.program_id(1)
    blk = ATTN_BLOCK
    win = 3 * blk

    @pl.when(i == 0)
    def _():
        _store_dup(kd_ref, _rope(k_ref[0].astype(F32), cosk_ref[...], sink_tab_ref[...]))
        _store_dup(vd_ref, v_ref[0].astype(F32))
        _store_dup(kcd_ref, kc_ref[0].astype(F32))
        _store_dup(vcd_ref, vc_ref[0].astype(F32))

    scale = HEAD_DIM ** -0.5 * LOG2_E
    n_sub = q_ref.shape[1] // blk
    qps, keys, vals, masks = [], [], [], []
    for sub in range(n_sub):
        qb = i * n_sub + sub
        rows = slice(sub * blk, (sub + 1) * blk)
        start = pl.multiple_of(jnp.clip((qb - 1) * blk, 0, seq - win), blk)
        qpos = qb * blk + lax.broadcasted_iota(I32, (blk, win), 0)
        kpos = start + lax.broadcasted_iota(I32, (blk, win), 1)
        mask = jnp.abs(kpos - qpos) <= WINDOW
        cos = cosq_ref[rows, :]
        sin = sinq_ref[rows, :]
        qps += [(_rope(q_ref[0, rows, p * LANES:(p + 1) * LANES].astype(F32), cos, sin) * scale).astype(BF16)
                for p in range(N_HEADS // 2)]
        keys += [[kd_ref[kh, pl.ds(start, win), :], kcd_ref[kh]] for kh in range(N_KV_HEADS)]
        vals += [[vd_ref[kh, pl.ds(start, win), :], vcd_ref[kh]] for kh in range(N_KV_HEADS)]
        masks += [[mask, None]] * N_KV_HEADS
    sinks = [sink_ref[h] * LOG2_E for h in range(N_HEADS)] * n_sub
    for t, o in enumerate(_heads_attention(qps, keys, vals, masks, sinks)):
        sub, p = divmod(t, N_HEADS // 2)
        o_ref[0, sub * blk:(sub + 1) * blk, p * LANES:(p + 1) * LANES] = o.astype(o_ref.dtype)


def _window_attention(p_x, p_c, ctx_kv_cols, sink, tabs):
    b, s, _ = p_x.shape
    l = p_c.shape[1]
    cos, sin = tabs
    blk = min(s, ATTN_BLOCKS_PER_STEP * ATTN_BLOCK)
    assert s % blk == 0
    kcol, vcol = COL_K // KV_W, COL_V // KV_W
    kccol, vccol = ctx_kv_cols[0] // KV_W, ctx_kv_cols[1] // KV_W
    return pl.pallas_call(
        functools.partial(_win_attn_kernel, seq=s),
        grid=(b, s // blk),
        in_specs=[pl.BlockSpec(memory_space=pltpu.SMEM),
                  pl.BlockSpec((1, blk, Q_W), lambda i, j: (i, j, COL_Q // Q_W)),
                  pl.BlockSpec((1, s, KV_W), lambda i, j: (i, 0, kcol)),
                  pl.BlockSpec((1, s, KV_W), lambda i, j: (i, 0, vcol)),
                  pl.BlockSpec((1, l, KV_W), lambda i, j: (i, 0, kccol)),
                  pl.BlockSpec((1, l, KV_W), lambda i, j: (i, 0, vccol)),
                  pl.BlockSpec((blk, LANES), lambda i, j: (j, 0)),
                  pl.BlockSpec((blk, LANES), lambda i, j: (j, 0)),
                  pl.BlockSpec((s, LANES), lambda i, j: (0, 0)),
                  pl.BlockSpec((s, LANES), lambda i, j: (0, 0))],
        out_specs=pl.BlockSpec((1, blk, Q_W), lambda i, j: (i, j, 0)),
        out_shape=jax.ShapeDtypeStruct((b, s, Q_W), BF16),
        scratch_shapes=[pltpu.VMEM((N_KV_HEADS, s, LANES), BF16), pltpu.VMEM((N_KV_HEADS, s, LANES), BF16),
                        pltpu.VMEM((N_KV_HEADS, l, LANES), BF16), pltpu.VMEM((N_KV_HEADS, l, LANES), BF16)],
        name="window_attention",
    )(sink, p_x, p_x, p_x, p_c, p_c, cos, sin, cos, sin)


def _ctx_attn_kernel(sink_ref, q_ref, k_ref, v_ref, o_ref):
    kd = [t.astype(BF16) for t in _dup_heads(k_ref[0].astype(F32))]
    vd = [t.astype(BF16) for t in _dup_heads(v_ref[0].astype(F32))]
    scale = HEAD_DIM ** -0.5 * LOG2_E
    qps = [(q_ref[0, :, p * LANES:(p + 1) * LANES].astype(F32) * scale).astype(BF16) for p in range(N_HEADS // 2)]
    sinks = [sink_ref[h] * LOG2_E for h in range(N_HEADS)]
    outs = _heads_attention(qps, [[k] for k in kd], [[v] for v in vd], [[None]] * N_KV_HEADS, sinks)
    for p, o in enumerate(outs):
        o_ref[0, :, p * LANES:(p + 1) * LANES] = o.astype(o_ref.dtype)


def _context_attention(p_c, sink):
    b, l, _ = p_c.shape
    return pl.pallas_call(
        _ctx_attn_kernel,
        grid=(b,),
        in_specs=[pl.BlockSpec(memory_space=pltpu.SMEM),
                  pl.BlockSpec((1, l, Q_W), lambda i: (i, 0, COL_Q // Q_W)),
                  pl.BlockSpec((1, l, KV_W), lambda i: (i, 0, COL_K // KV_W)),
                  pl.BlockSpec((1, l, KV_W), lambda i: (i, 0, COL_V // KV_W))],
        out_specs=pl.BlockSpec((1, l, Q_W), lambda i: (i, 0, 0)),
        out_shape=jax.ShapeDtypeStruct((b, l, Q_W), BF16),
        name="context_attention",
    )(sink, p_c, p_c, p_c)


def _merge_kernel(ga_ref, gb_ref, gc_ref,
                  u_ref, u_p_ref, u_n_ref, pz_ref, pz_p_ref, pz_n_ref,
                  attn_ref, x_ref, g1_ref, sh2_ref, sc2_ref, n2g_ref,
                  dw_ref, dwb_ref, lng_ref, lnb_ref, wpool_ref, pscale_ref,
                  wa_ref, wb_ref, wc_ref, wo_ref, wrt_ref,
                  xo_ref, h2_ref, afft_ref,
                  uwin_ref, zwin_ref, *, seq):
    t = pl.program_id(1)
    tt = x_ref.shape[1]
    has_prev = (t > 0).astype(F32)
    has_next = (t < pl.num_programs(1) - 1).astype(F32)

    uwin_ref[0:HALO, :] = u_p_ref[0].astype(F32) * has_prev
    uwin_ref[HALO:HALO + tt, :] = u_ref[0].astype(F32)
    uwin_ref[HALO + tt:, :] = u_n_ref[0].astype(F32) * has_next
    first = HALO - CONV_PAD
    rows = tt + 2 * HALO
    acc_cols = []
    for cb in range(CONV_CH // LANES):
        cols = slice(cb * LANES, (cb + 1) * LANES)
        window = uwin_ref[:, cols]
        acc_c = jnp.zeros((tt, LANES), F32) + dwb_ref[:, cols]
        for shift in range(F32_SUBLANES):
            taps = [k for k in range(CONV_K) if (first + k) % F32_SUBLANES == shift]
            if not taps:
                continue
            shifted = window if shift == 0 else pltpu.roll(window, rows - shift, 0)
            for k in taps:
                off = first + k - shift
                acc_c = acc_c + shifted[off:off + tt] * dw_ref[k:k + 1, cols]
        acc_cols.append(acc_c)
    acc = jnp.concatenate(acc_cols, axis=-1)
    mu = jnp.mean(acc, axis=-1, keepdims=True)
    cen = acc - mu
    var = jnp.mean(cen * cen, axis=-1, keepdims=True)
    ln = cen * lax.rsqrt(var + EPS) * lng_ref[...] + lnb_ref[...]
    feat_b = (ln * _sigmoid(ln)).astype(BF16)

    zwin_ref[0:HALO, :] = pz_p_ref[0].astype(F32) * has_prev
    zwin_ref[HALO:HALO + tt, :] = pz_ref[0].astype(F32)
    zwin_ref[HALO + tt:, :] = pz_n_ref[0].astype(F32) * has_next
    tpos = t * tt + lax.broadcasted_iota(I32, (tt, 1), 0)
    pooled = []
    for gi, w in enumerate(POOL_WINDOWS):
        cols = slice(gi * POOL_GROUP, (gi + 1) * POOL_GROUP)
        tot = zwin_ref[pl.ds(HALO - w // 2, tt), cols]
        for d in range(1 - w // 2, w - w // 2):
            tot = tot + zwin_ref[pl.ds(HALO + d, tt), cols]
        cnt = (jnp.minimum(tpos + (w - w // 2), seq) - jnp.maximum(tpos - w // 2, 0)).astype(F32)
        diff = tot / cnt - zwin_ref[pl.ds(HALO, tt), cols]
        pooled.append(_dot(diff.astype(BF16), wpool_ref[gi]))
    feat_c = (jnp.concatenate(pooled, axis=-1) * pscale_ref[...]).astype(BF16)

    y_a = _dot(attn_ref[0], wa_ref[...])
    y_b = _dot(feat_b, wb_ref[...])
    y_c = _dot(feat_c, wc_ref[...])
    merged = ga_ref[0].astype(F32) * y_a + gb_ref[0].astype(F32) * y_b + gc_ref[0].astype(F32) * y_c
    xn = x_ref[0] + g1_ref[0] * _dot(merged.astype(BF16), wo_ref[...])
    xo_ref[0] = xn

    h2 = _rms_mod(xn, n2g_ref[...], sh2_ref[0], sc2_ref[0])
    h2_hi, h2_lo = _hi_lo(h2)
    h2_ref[0] = h2_hi
    ne = afft_ref.shape[1]
    by_hi = _dot_nt(wrt_ref[...], h2_hi)
    logits_t = by_hi[:ne] + by_hi[ne:] + _dot_nt(wrt_ref[:ne, :], h2_lo)
    et = jnp.exp(logits_t - jnp.max(logits_t, axis=0, keepdims=True))
    afft_ref[0] = et / jnp.sum(et, axis=0, keepdims=True)


def _merge(p, attn, x, g1, sh2, sc2, n2g, lw, tt):
    b, s, d = x.shape
    nh = tt // HALO
    last_h = s // HALO - 1
    e = N_EXPERTS

    def main(width, col):
        return pl.BlockSpec((1, tt, width), lambda i, j: (i, j, col))

    def prev(col):
        return pl.BlockSpec((1, HALO, CONV_CH), lambda i, j: (i, jnp.maximum(j * nh - 1, 0), col))

    def nxt(col):
        return pl.BlockSpec((1, HALO, CONV_CH), lambda i, j: (i, jnp.minimum((j + 1) * nh, last_h), col))

    def per_batch():
        return pl.BlockSpec((1, 1, d), lambda i, j: (i, 0, 0))

    def const(shape):
        return pl.BlockSpec(shape, lambda i, j: (0,) * len(shape))

    cu, pz = COL_CONV_U // CONV_CH, COL_POOL // CONV_CH
    in_specs = [main(d, 0), main(d, 1), main(d, 2),
                main(CONV_CH, cu), prev(cu), nxt(cu), main(POOL_CH, pz), prev(pz), nxt(pz),
                pl.BlockSpec((1, tt, Q_W), lambda i, j: (i, j, 0)),
                pl.BlockSpec((1, tt, d), lambda i, j: (i, j, 0)),
                per_batch(), per_batch(), per_batch(), const((1, d)),
                const((CONV_K, CONV_CH)), const((1, CONV_CH)), const((1, CONV_CH)), const((1, CONV_CH)),
                const((len(POOL_WINDOWS), POOL_GROUP, POOL_GROUP)), const((1, POOL_CH)),
                const((Q_W, d)), const((CONV_CH, d)), const((POOL_CH, d)), const((d, d)),
                const((2 * e, d))]
    out_specs = [pl.BlockSpec((1, tt, d), lambda i, j: (i, j, 0)),
                 pl.BlockSpec((1, tt, d), lambda i, j: (i, j, 0)),
                 pl.BlockSpec((1, e, tt), lambda i, j: (i, 0, j))]
    out_shape = [jax.ShapeDtypeStruct((b, s, d), F32), jax.ShapeDtypeStruct((b, s, d), BF16),
                 jax.ShapeDtypeStruct((b, e, s), F32)]
    return pl.pallas_call(
        functools.partial(_merge_kernel, seq=s),
        grid=(b, s // tt),
        in_specs=in_specs, out_specs=out_specs, out_shape=out_shape,
        scratch_shapes=[pltpu.VMEM((tt + 2 * HALO, CONV_CH), F32), pltpu.VMEM((tt + 2 * HALO, POOL_CH), F32)],
        name="mix_merge",
    )(p, p, p, p, p, p, p, p, p, attn, x, g1, sh2, sc2, n2g,
      lw['conv_dw'], lw['conv_dw_b'], lw['conv_ln_g'], lw['conv_ln_b'], lw['w_pool'], lw['pool_scale'],
      lw['w_attn_o'], lw['w_conv_o'], lw['w_pool_o'], lw['w_out'], lw['w_router_t'])


def _topk_kernel(afft_ref, slot_ref, slott_ref, offs_ref, *, cap, blk):
    a = afft_ref[0]
    e, s = a.shape

    def keeps_cap(cand):
        return jnp.sum((a >= cand).astype(F32), axis=-1, keepdims=True) >= cap

    tiny = jnp.full((e, 1), F32_TINY, F32)
    thr = jnp.where(keeps_cap(tiny), tiny, 0.0)
    for step in (64, 32, 16, 8, 4, 2, 1):
        cand = thr * float(2 ** step)
        thr = jnp.where(keeps_cap(cand), cand, thr)
    delta = thr
    for _ in range(F32_MANTISSA_BITS):
        delta = delta * 0.5
        cand = thr + delta
        thr = jnp.where(keeps_cap(cand), cand, thr)
    gt = a > thr
    eq = a == thr
    need = cap - jnp.sum(gt.astype(F32), axis=-1, keepdims=True)

    r = lax.broadcasted_iota(I32, (blk, blk), 0)
    c = lax.broadcasted_iota(I32, (blk, blk), 1)
    upper = (r < c).astype(BF16)
    eye = (r == c).astype(F32)

    def prefix(mask_f32):
        carry = jnp.zeros((e, 1), F32)
        parts = []
        for j in range(s // blk):
            m = mask_f32[:, j * blk:(j + 1) * blk]
            parts.append(_dot(m.astype(BF16), upper) + carry)
            carry = carry + jnp.sum(m, axis=-1, keepdims=True)
        return jnp.concatenate(parts, axis=-1)

    sel = gt | (eq & (prefix(eq.astype(F32)) < need))
    pos = prefix(sel.astype(F32))
    slot = jnp.where(sel, pos, -1.0)
    slot_ref[0] = slot.astype(I32)
    for j in range(s // blk):
        slott_ref[0, j * blk:(j + 1) * blk, :] = _dot_nt(eye, slot[:, j * blk:(j + 1) * blk],
                                                        precision=HIGHEST).astype(I32)
    tok = lax.broadcasted_iota(I32, (s, LANES), 0)
    col = lax.broadcasted_iota(I32, (s, LANES), 1)
    before = (tok < col * blk).astype(BF16)
    offs_ref[0] = _dot(sel.astype(BF16), before).astype(I32)


def _topk(afft, cap):
    b, e, s = afft.shape
    blk = min(s, TOKEN_CHUNK)
    slot, slott, offs = pl.pallas_call(
        functools.partial(_topk_kernel, cap=cap, blk=blk),
        grid=(b,),
        in_specs=[pl.BlockSpec((1, e, s), lambda i: (i, 0, 0))],
        out_specs=[pl.BlockSpec((1, e, s), lambda i: (i, 0, 0)),
                   pl.BlockSpec((1, s, e), lambda i: (i, 0, 0)),
                   pl.BlockSpec((1, e, LANES), lambda i: (i, 0, 0))],
        out_shape=[jax.ShapeDtypeStruct((b, e, s), I32), jax.ShapeDtypeStruct((b, s, e), I32),
                   jax.ShapeDtypeStruct((b, e, LANES), I32)],
        name="expert_choice_topk",
    )(afft)
    return slot, slott, offs[:, :, :s // blk + 1]


def _slot_windows(offs_ref, idx, win):
    lo = offs_ref[idx]
    hi = offs_ref[idx + 1]
    first = lo // win
    return first, jnp.where(hi > lo, (hi - 1) // win - first + 1, 0)


def _gather_rows(onehot, h, gate_row):
    picked = _dot(onehot.astype(BF16), h)
    gates = jnp.sum(jnp.where(onehot, gate_row, 0.0), axis=-1, keepdims=True)
    return picked, gates


def _ffn_kernel(offs_ref, slot_ref, afft_ref, h_ref, slotc_ref, cafft_ref, hc_ref, wg_ref, wu_ref, wd_ref,
                ye_ref, yec_ref, wg_s, wu_s, wd_s, xe_s, g_s, *, chunk, group, win, rows):
    ex = pl.program_id(0)
    bi = pl.program_id(1)
    nb = pl.num_programs(1)
    cap = xe_s.shape[0]

    @pl.when(bi == 0)
    def _():
        def cast(i, carry):
            sl = pl.ds(pl.multiple_of(i * rows, rows), rows)
            wg_s[sl, :] = wg_ref[0, 0, sl, :].astype(BF16)
            wu_s[sl, :] = wu_ref[0, 0, sl, :].astype(BF16)
            wd_s[sl, :] = wd_ref[0, 0, sl, :].astype(BF16)
            return carry
        lax.fori_loop(0, wg_s.shape[0] // rows, cast, 0)

    def ffn(xe):
        a = _dot(xe, wg_s[...])
        u = _dot(xe, wu_s[...])
        hid = (a * _sigmoid(a) * u).astype(BF16)
        return _dot(hid, wd_s[...])

    nch = h_ref.shape[1] // chunk
    base = (bi * pl.num_programs(0) + ex) * (nch + 1)
    xe_s[...] = jnp.zeros_like(xe_s)
    g_s[...] = jnp.zeros_like(g_s)
    span = group * chunk
    starts = []
    fits = None
    for p in range(nch // group):
        lo = offs_ref[base + p * group]
        hi = offs_ref[base + (p + 1) * group]
        a = pl.multiple_of(jnp.minimum((lo // F32_SUBLANES) * F32_SUBLANES, cap - win), F32_SUBLANES)
        starts.append(a)
        fits = (hi - a <= win) if fits is None else jnp.logical_and(fits, hi - a <= win)

    def add_window(a, sl, width):
        onehot = (slot_ref[0, pl.ds(ex, 1), sl] - a) == lax.broadcasted_iota(I32, (win, width), 0)
        picked, gates = _gather_rows(onehot, h_ref[0, sl, :], afft_ref[0, pl.ds(ex, 1), sl])
        xe_s[pl.ds(a, win), :] += picked
        g_s[pl.ds(a, win), :] += gates

    @pl.when(fits)
    def _():
        for p, a in enumerate(starts):
            add_window(a, slice(p * span, (p + 1) * span), span)

    @pl.when(jnp.logical_not(fits))
    def _():
        def per_chunk(j, carry):
            first, nwin = _slot_windows(offs_ref, base + j, win)
            sl = pl.ds(pl.multiple_of(j * chunk, chunk), chunk)

            def window(w, c):
                add_window(pl.multiple_of((first + w) * win, win), sl, chunk)
                return c
            lax.fori_loop(0, nwin, window, 0)
            return carry
        lax.fori_loop(0, nch, per_chunk, 0)

    ye_ref[0] = (ffn(xe_s[...].astype(BF16)) * g_s[...]).astype(ye_ref.dtype)

    if yec_ref is not None:
        @pl.when(bi == nb - 1)
        def _():
            rowc = lax.broadcasted_iota(I32, (yec_ref.shape[1], hc_ref.shape[0]), 0)
            picked, gates = _gather_rows(slotc_ref[0] == rowc, hc_ref[...], cafft_ref[0])
            yec_ref[0] = (ffn(picked.astype(BF16)) * gates).astype(yec_ref.dtype)


def _expert_ffn(layer, slot, offs, afft, h2, wg, wu, wd, cap, ctx_part=None):
    b, e, s = slot.shape
    d = h2.shape[2]
    f = wg.shape[3]
    assert f == d
    chunk = min(s, TOKEN_CHUNK)
    nch = s // chunk
    group = 2 if nch % 2 == 0 else 1
    win = min(cap, GATHER_WINDOW)
    assert cap % win == 0
    w_spec = pl.BlockSpec((1, 1, d, f), lambda j, i, o: (layer, j, 0, 0))
    row_spec = pl.BlockSpec((1, e, s), lambda j, i, o: (i, 0, 0))
    in_specs = [row_spec, row_spec, pl.BlockSpec((1, s, d), lambda j, i, o: (i, 0, 0))]
    out_specs = [pl.BlockSpec((1, cap, d), lambda j, i, o: (i * e + j, 0, 0))]
    out_shape = [jax.ShapeDtypeStruct((b * e, cap, d), BF16)]
    args = [slot, afft, h2]
    body = functools.partial(_ffn_kernel, chunk=chunk, group=group, win=win, rows=128)
    if ctx_part is None:
        def kern(offs_ref, slot_ref, afft_ref, h_ref, wg_ref, wu_ref, wd_ref, ye_ref, *scratch):
            body(offs_ref, slot_ref, afft_ref, h_ref, None, None, None, wg_ref, wu_ref, wd_ref, ye_ref, None,
                 *scratch)
    else:
        slot_c, afft_c, h_c, rows_c = ctx_part
        n_c = h_c.shape[0]
        rowc_spec = pl.BlockSpec((1, 1, n_c), lambda j, i, o: (j, 0, 0))
        in_specs += [rowc_spec, rowc_spec, pl.BlockSpec((n_c, d), lambda j, i, o: (0, 0))]
        out_specs.append(pl.BlockSpec((1, rows_c, d), lambda j, i, o: (j, 0, 0)))
        out_shape.append(jax.ShapeDtypeStruct((e, rows_c, d), BF16))
        args += [slot_c, afft_c, h_c]
        kern = body
    outs = pl.pallas_call(
        kern,
        grid_spec=pltpu.PrefetchScalarGridSpec(
            num_scalar_prefetch=1, grid=(e, b),
            in_specs=in_specs + [w_spec, w_spec, w_spec], out_specs=out_specs,
            scratch_shapes=[pltpu.VMEM((d, f), BF16), pltpu.VMEM((d, f), BF16), pltpu.VMEM((f, d), BF16),
                            pltpu.VMEM((cap, d), F32), pltpu.VMEM((cap, 1), F32)]),
        out_shape=out_shape,
        name="expert_ffn",
    )(offs.reshape(-1), *args, wg, wu, wd)
    ye = outs[0].reshape(b, e * cap, d)
    return ye if ctx_part is None else (ye, outs[1])


def _combine_kernel(offs_ref, slott_ref, ye_ref, x_ref, g2_ref, fg_ref, o_ref, acc_s, *, cap, win, final_norm):
    bi = pl.program_id(0)
    tt = acc_s.shape[0]
    n_sub = x_ref.shape[1] // tt
    nch = pl.num_programs(1) * n_sub
    per_block = LANES // win

    for sub in range(n_sub):
        j = pl.program_id(1) * n_sub + sub
        tok = slice(sub * tt, (sub + 1) * tt)

        starts = []
        fits = None
        for ex in range(N_EXPERTS):
            idx = (bi * N_EXPERTS + ex) * (nch + 1) + j
            a = pl.multiple_of(jnp.minimum((offs_ref[idx] // BF16_SUBLANES) * BF16_SUBLANES, cap - win),
                               BF16_SUBLANES)
            starts.append(a)
            ok = offs_ref[idx + 1] - a <= win
            fits = ok if fits is None else jnp.logical_and(fits, ok)

        @pl.when(fits)
        def _(starts=starts, tok=tok):
            lane = lax.broadcasted_iota(I32, (tt, LANES), 1)
            blocks = []
            for blk in range(N_EXPERTS // per_block):
                target = None
                for q in range(per_block):
                    ex = blk * per_block + q
                    t = slott_ref[0, tok, ex:ex + 1] - starts[ex] + q * win
                    target = t if target is None else jnp.where(lane >= q * win, t, target)
                blocks.append((target == lane).astype(BF16))
            rows = [ye_ref[0, pl.ds(ex * cap + starts[ex], win), :] for ex in range(N_EXPERTS)]
            acc_s[...] = _dot(jnp.concatenate(blocks, axis=1), jnp.concatenate(rows, axis=0))

        @pl.when(jnp.logical_not(fits))
        def _(j=j, tok=tok):
            lane = lax.broadcasted_iota(I32, (tt, win), 1)
            acc_s[...] = jnp.zeros_like(acc_s)
            for ex in range(N_EXPERTS):
                first, nwin = _slot_windows(offs_ref, (bi * N_EXPERTS + ex) * (nch + 1) + j, win)

                def window(w, c, ex=ex, first=first):
                    a = pl.multiple_of((first + w) * win, win)
                    onehot = ((slott_ref[0, tok, ex:ex + 1] - a) == lane).astype(BF16)
                    acc_s[...] += _dot(onehot, ye_ref[0, pl.ds(ex * cap + a, win), :])
                    return c
                lax.fori_loop(0, nwin, window, 0)

        out = x_ref[0, tok, :] + g2_ref[0] * acc_s[...]
        if final_norm:
            out = out * lax.rsqrt(jnp.mean(out * out, axis=-1, keepdims=True) + EPS) * fg_ref[...]
        o_ref[0, tok, :] = out


def _combine(slott, offs, ye, x, g2, fg, cap, final_norm):
    b, s, d = x.shape
    e = N_EXPERTS
    chunk = min(s, TOKEN_CHUNK)
    tt = min(s, COMBINE_CHUNKS * chunk)
    win = min(cap, SCATTER_WINDOW)
    assert cap % win == 0 and LANES % win == 0 and e % (LANES // win) == 0 and s % tt == 0
    return pl.pallas_call(
        functools.partial(_combine_kernel, cap=cap, win=win, final_norm=final_norm),
        grid_spec=pltpu.PrefetchScalarGridSpec(
            num_scalar_prefetch=1, grid=(b, s // tt),
            in_specs=[pl.BlockSpec((1, tt, e), lambda i, j, o: (i, j, 0)),
                      pl.BlockSpec((1, e * cap, d), lambda i, j, o: (i, 0, 0)),
                      pl.BlockSpec((1, tt, d), lambda i, j, o: (i, j, 0)),
                      pl.BlockSpec((1, 1, d), lambda i, j, o: (i, 0, 0)),
                      pl.BlockSpec((1, d), lambda i, j, o: (0, 0))],
            out_specs=pl.BlockSpec((1, tt, d), lambda i, j, o: (i, j, 0)),
            scratch_shapes=[pltpu.VMEM((chunk, d), F32)]),
        out_shape=jax.ShapeDtypeStruct((b, s, d), F32),
        name="moe_combine",
    )(offs.reshape(-1), slott, ye, x, g2, fg)


def _split_hi_lo(w):
    return jnp.concatenate(_hi_lo(w), axis=0)


def _permute_in_cols(w):
    o_k = Q_W
    o_v = o_k + KV_W
    o_ca = o_v + KV_W
    o_cg = o_ca + CONV_CH
    o_p = o_cg + CONV_CH
    o_g = o_p + POOL_CH
    return jnp.concatenate([w[:, o_g:], w[:, :o_k], w[:, o_ca:o_cg], w[:, o_cg:o_p], w[:, o_p:o_g],
                            w[:, o_k:o_v], w[:, o_v:o_ca]], axis=1)


def kernel(x, c, ctx, c_ctx, norm1_g, norm2_g, w_mod, b_mod, w_in, attn_sink, w_attn_o, conv_dw, conv_dw_b,
           conv_ln_g, conv_ln_b, w_conv_o, w_pool, pool_scale, w_pool_o, w_out, w_router, w_e_gate, w_e_up,
           w_e_down, final_norm_g):
    b, s, d = x.shape
    l = ctx.shape[1]
    depth = w_in.shape[0]
    assert d == _D_MODEL and w_in.shape[2] == IN_W and CONV_CH == POOL_CH

    tabs = _rope_tables(s)
    cc = jnp.zeros((8, d), F32).at[:b].set(c).at[b].set(c_ctx)
    mod = _modulation(cc, w_mod, b_mod)
    fg = final_norm_g.reshape(1, d)

    for layer in range(depth):
        last = layer == depth - 1
        mx = mod[layer, :b].reshape(b, 1, 6, d)
        sh1, sc1, g1, sh2, sc2, g2 = [mx[:, :, i] for i in range(6)]
        mc = jnp.broadcast_to(mod[layer, b].reshape(1, 1, 6, d), (b, 1, 6, d))
        csh1, csc1, cg1, csh2, csc2, cg2 = [mc[:, :, i] for i in range(6)]
        n1g = norm1_g[layer].reshape(1, d)
        n2g = norm2_g[layer].reshape(1, d)
        w_in_l = _permute_in_cols(w_in[layer]).astype(BF16)
        lw = {'conv_dw': conv_dw[layer], 'conv_dw_b': conv_dw_b[layer].reshape(1, -1),
              'conv_ln_g': conv_ln_g[layer].reshape(1, -1), 'conv_ln_b': conv_ln_b[layer].reshape(1, -1),
              'w_pool': w_pool[layer].astype(BF16), 'pool_scale': pool_scale[layer].reshape(1, -1),
              'w_attn_o': w_attn_o[layer].astype(BF16), 'w_conv_o': w_conv_o[layer].astype(BF16),
              'w_pool_o': w_pool_o[layer].astype(BF16), 'w_out': w_out[layer].astype(BF16),
              'w_router_t': _split_hi_lo(w_router[layer].T)}

        p_x = _inproj(x, n1g, sh1, sc1, w_in_l, tm=min(s, INPROJ_TILE))
        if last:
            p_c = _inproj(ctx, n1g, csh1, csc1, w_in_l[:, WCOL_K:], tm=l, mixer_epilogue=False)
            ctx_kv_cols = (0, KV_W)
        else:
            p_c = _inproj(ctx, n1g, csh1, csc1, w_in_l, tm=l)
            ctx_kv_cols = (COL_K, COL_V)
        attn_x = _window_attention(p_x, p_c, ctx_kv_cols, attn_sink[layer], tabs)
        x_mid, h2, afft = _merge(p_x, attn_x, x, g1, sh2, sc2, n2g, lw, tt=min(s, MERGE_TILE))
        if not last:
            attn_c = _context_attention(p_c, attn_sink[layer])
            c_mid, ch2, cafft = _merge(p_c, attn_c, ctx, cg1, csh2, csc2, n2g, lw, tt=l)
        cap = (CAPACITY_FACTOR * s) // N_EXPERTS
        slot, slott, offs = _topk(afft, cap)
        if last:
            ye = _expert_ffn(layer, slot, offs, afft, h2, w_e_gate, w_e_up, w_e_down, cap)
        else:
            cap_c = (CAPACITY_FACTOR * l) // N_EXPERTS
            cslot, cslott, coffs = _topk(cafft, cap_c)
            sample_base = (jnp.arange(b, dtype=I32) * cap_c)[:, None, None]
            cslot_all = jnp.where(cslot >= 0, cslot + sample_base, -1)
            cslot_all = cslot_all.transpose(1, 0, 2).reshape(N_EXPERTS, 1, b * l)
            cafft_all = cafft.transpose(1, 0, 2).reshape(N_EXPERTS, 1, b * l)
            ye, yec = _expert_ffn(layer, slot, offs, afft, h2, w_e_gate, w_e_up, w_e_down, cap,
                                  ctx_part=(cslot_all, cafft_all, ch2.reshape(b * l, d), b * cap_c))
            yec = yec.reshape(N_EXPERTS, b, cap_c, d).transpose(1, 0, 2, 3).reshape(b, N_EXPERTS * cap_c, d)
            ctx = _combine(cslott, coffs, yec, c_mid, cg2, fg, cap_c, False)
        x = _combine(slott, offs, ye, x_mid, g2, fg, cap, last)
    return x
```

```python
import functools

import jax
import jax.numpy as jnp
import numpy as np
from jax import lax
from jax.experimental import pallas as pl
from jax.experimental.pallas import tpu as pltpu

F32 = jnp.float32
BF16 = jnp.bfloat16
I32 = jnp.int32

EPS = 1e-6
GRID_W = 64
N_HEADS = 8
N_KV_HEADS = 2
HEAD_DIM = 64
GQA_GROUP = N_HEADS // N_KV_HEADS
WINDOW = 128
ATTN_BLOCK = 128
ROPE_BASE = 10000.0
ROPE_PAIRS = HEAD_DIM // 4
CONV_CH = 512
CONV_K = 31
CONV_PAD = CONV_K // 2
POOL_WINDOWS = (2, 4, 8, 16)
POOL_GROUP = 128
POOL_CH = POOL_GROUP * len(POOL_WINDOWS)
N_EXPERTS = 16
CAPACITY_FACTOR = 2
Q_W = N_HEADS * HEAD_DIM
KV_W = N_KV_HEADS * HEAD_DIM

LANES = 128
HALO = 16
HIGHEST = lax.Precision.HIGHEST
LOG2_E = 1.4426950408889634
F32_TINY = 2.0 ** -126
F32_MANTISSA_BITS = 23
TOKEN_CHUNK = 256
GATHER_WINDOW = 128
SCATTER_WINDOW = 64
F32_SUBLANES = 8
BF16_SUBLANES = 16
ATTN_LOOKAHEAD = 3
ATTN_BLOCKS_PER_STEP = 8
COMBINE_CHUNKS = 4
INPROJ_TILE = 1024
MERGE_TILE = 512

_D_MODEL = 1024
N_GATE_COLS = 3 * _D_MODEL
WCOL_Q = N_GATE_COLS
WCOL_CONV_A = WCOL_Q + Q_W
WCOL_CONV_G = WCOL_CONV_A + CONV_CH
WCOL_POOL = WCOL_CONV_G + CONV_CH
WCOL_K = WCOL_POOL + POOL_CH
IN_W = WCOL_K + 2 * KV_W
COL_Q = N_GATE_COLS
COL_CONV_U = COL_Q + Q_W
COL_POOL = COL_CONV_U + CONV_CH
COL_K = COL_POOL + POOL_CH
COL_V = COL_K + KV_W
OUT_W = COL_V + KV_W
GATE_CHUNK = 768


def _dot(a, b):
    return jnp.dot(a, b, preferred_element_type=F32)


def _dot_nt(a, b, precision=None):
    return lax.dot_general(a, b, (((1,), (1,)), ((), ())), preferred_element_type=F32, precision=precision)


def _sigmoid(v):
    return 0.5 * jnp.tanh(0.5 * v) + 0.5


def _rms_mod(x, g, sh, sc):
    y = x * lax.rsqrt(jnp.mean(x * x, axis=-1, keepdims=True) + EPS) * g
    return y * (1.0 + sc) + sh


def _hi_lo(v):
    hi = v.astype(BF16)
    return hi, (v - hi.astype(F32)).astype(BF16)


def _mod_kernel(c_ref, w_ref, b_ref, o_ref):
    c = c_ref[...]
    a_hi, a_lo = _hi_lo(c * _sigmoid(c))
    w_hi, w_lo = _hi_lo(w_ref[0])
    o_ref[0] = _dot(a_hi, w_hi) + _dot(a_lo, w_hi) + _dot(a_hi, w_lo) + b_ref[0]


def _modulation(cc, w_mod, b_mod, tn=1536):
    depth, d, n = w_mod.shape
    rows = cc.shape[0]
    return pl.pallas_call(
        _mod_kernel,
        grid=(depth, n // tn),
        in_specs=[pl.BlockSpec((rows, d), lambda l, j: (0, 0)),
                  pl.BlockSpec((1, d, tn), lambda l, j: (l, 0, j)),
                  pl.BlockSpec((1, 1, tn), lambda l, j: (l, 0, j))],
        out_specs=pl.BlockSpec((1, rows, tn), lambda l, j: (l, 0, j)),
        out_shape=jax.ShapeDtypeStruct((depth, rows, n), F32),
        name="modulation",
    )(cc, w_mod, b_mod.reshape(depth, 1, n))


def _inproj_kernel(x_ref, g_ref, sh_ref, sc_ref, w_ref, o_ref, *, mixer_epilogue):
    h = _rms_mod(x_ref[0], g_ref[...], sh_ref[0], sc_ref[0]).astype(BF16)
    if not mixer_epilogue:
        o_ref[0] = _dot(h, w_ref[...]).astype(o_ref.dtype)
        return
    for c0 in range(0, N_GATE_COLS, GATE_CHUNK):
        cols = slice(c0, c0 + GATE_CHUNK)
        o_ref[0, :, cols] = _sigmoid(_dot(h, w_ref[:, cols])).astype(o_ref.dtype)
    o_ref[0, :, COL_Q:COL_CONV_U] = _dot(h, w_ref[:, WCOL_Q:WCOL_CONV_A]).astype(o_ref.dtype)
    glu = _dot(h, w_ref[:, WCOL_CONV_A:WCOL_POOL])
    o_ref[0, :, COL_CONV_U:COL_POOL] = (glu[:, :CONV_CH] * _sigmoid(glu[:, CONV_CH:])).astype(o_ref.dtype)
    o_ref[0, :, COL_POOL:] = _dot(h, w_ref[:, WCOL_POOL:]).astype(o_ref.dtype)


def _inproj(x, g, sh, sc, w, tm, mixer_epilogue=True):
    b, s, d = x.shape
    n = w.shape[1]
    n_out = OUT_W if mixer_epilogue else n
    assert not mixer_epilogue or n == IN_W
    return pl.pallas_call(
        functools.partial(_inproj_kernel, mixer_epilogue=mixer_epilogue),
        grid=(b, s // tm),
        in_specs=[pl.BlockSpec((1, tm, d), lambda i, j: (i, j, 0)),
                  pl.BlockSpec((1, d), lambda i, j: (0, 0)),
                  pl.BlockSpec((1, 1, d), lambda i, j: (i, 0, 0)),
                  pl.BlockSpec((1, 1, d), lambda i, j: (i, 0, 0)),
                  pl.BlockSpec((d, n), lambda i, j: (0, 0))],
        out_specs=pl.BlockSpec((1, tm, n_out), lambda i, j: (i, j, 0)),
        out_shape=jax.ShapeDtypeStruct((b, s, n_out), BF16),
        name="inproj",
    )(x, g, sh, sc, w)


def _rope_tables(s):
    t = np.arange(s)
    row = (t // GRID_W).astype(np.float32)
    col = (t % GRID_W).astype(np.float32)
    freqs = jnp.asarray(ROPE_BASE, F32) ** (-jnp.arange(ROPE_PAIRS, dtype=F32) / ROPE_PAIRS)
    ang_r = jnp.asarray(row)[:, None] * freqs
    ang_c = jnp.asarray(col)[:, None] * freqs
    cos_h = jnp.concatenate([jnp.cos(ang_r), jnp.cos(ang_r), jnp.cos(ang_c), jnp.cos(ang_c)], axis=-1)
    sin_h = jnp.concatenate([-jnp.sin(ang_r), jnp.sin(ang_r), -jnp.sin(ang_c), jnp.sin(ang_c)], axis=-1)
    return jnp.tile(cos_h, (1, LANES // HEAD_DIM)), jnp.tile(sin_h, (1, LANES // HEAD_DIM))


def _rope(x, cos, sin_signed):
    lane = lax.broadcasted_iota(I32, x.shape, 1)
    low = (lane & (2 * ROPE_PAIRS - 1)) < ROPE_PAIRS
    partner = jnp.where(low, pltpu.roll(x, LANES - ROPE_PAIRS, 1), pltpu.roll(x, ROPE_PAIRS, 1))
    return x * cos + partner * sin_signed


def _softmax_pv(s_list, v_list, sink):
    m = sink
    for s in s_list:
        m = jnp.maximum(m, jnp.max(s, axis=-1, keepdims=True))
    denom = jnp.exp2(sink - m)
    o = None
    for s, v in zip(s_list, v_list):
        e = jnp.exp2(s - m)
        denom = denom + jnp.sum(e, axis=-1, keepdims=True)
        pv = _dot(e.astype(BF16), v)
        o = pv if o is None else o + pv
    return o / denom


def _lane_lo(shape):
    return lax.broadcasted_iota(I32, shape, 1) < HEAD_DIM


def _dup_heads(t):
    swapped = pltpu.roll(t, HEAD_DIM, 1)
    lo = _lane_lo(t.shape)
    return jnp.where(lo, t, swapped), jnp.where(lo, swapped, t)


def _heads_attention(qps, keys, vals, masks, sinks):
    lo = _lane_lo(qps[0].shape)
    keeps = (lo, jnp.logical_not(lo))
    tiles_per_kv = len(qps) // len(keys)

    def head_scores(h):
        qp = qps[h // 2]
        qh = jnp.where(keeps[h % 2], qp, jnp.zeros_like(qp))
        g = h // 2 // tiles_per_kv
        return [sc if mask is None else jnp.where(mask, sc, -1e30)
                for sc, mask in zip([_dot_nt(qh, k) for k in keys[g]], masks[g])]

    n_heads = 2 * len(qps)
    outs = []
    pending = [head_scores(h) for h in range(min(ATTN_LOOKAHEAD, n_heads))]
    for h in range(n_heads):
        if h + ATTN_LOOKAHEAD < n_heads:
            pending.append(head_scores(h + ATTN_LOOKAHEAD))
        outs.append(_softmax_pv(pending.pop(0), vals[h // 2 // tiles_per_kv], sinks[h]))
    return [jnp.where(lo, outs[2 * i], outs[2 * i + 1]) for i in range(len(qps))]


def _store_dup(dst_ref, t):
    d0, d1 = _dup_heads(t)
    dst_ref[0] = d0.astype(dst_ref.dtype)
    dst_ref[1] = d1.astype(dst_ref.dtype)


def _win_attn_kernel(sink_ref, q_ref, k_ref, v_ref, kc_ref, vc_ref, cosq_ref, sinq_ref, cosk_ref, sink_tab_ref,
                     o_ref, kd_ref, vd_ref, kcd_ref, vcd_ref, *, seq):
    i = pl.program_id(1)
    blk = ATTN_BLOCK
    win = 3 * blk

    @pl.when(i == 0)
    def _():
        _store_dup(kd_ref, _rope(k_ref[0].astype(F32), cosk_ref[...], sink_tab_ref[...]))
        _store_dup(vd_ref, v_ref[0].astype(F32))
        _store_dup(kcd_ref, kc_ref[0].astype(F32))
        _store_dup(vcd_ref, vc_ref[0].astype(F32))

    scale = HEAD_DIM ** -0.5 * LOG2_E
    n_sub = q_ref.shape[1] // blk
    qps, keys, vals, masks = [], [], [], []
    for sub in range(n_sub):
        qb = i * n_sub + sub
        rows = slice(sub * blk, (sub + 1) * blk)
        start = pl.multiple_of(jnp.clip((qb - 1) * blk, 0, seq - win), blk)
        qpos = qb * blk + lax.broadcasted_iota(I32, (blk, win), 0)
        kpos = start + lax.broadcasted_iota(I32, (blk, win), 1)
        mask = jnp.abs(kpos - qpos) <= WINDOW
        cos = cosq_ref[rows, :]
        sin = sinq_ref[rows, :]
        qps += [(_rope(q_ref[0, rows, p * LANES:(p + 1) * LANES].astype(F32), cos, sin) * scale).astype(BF16)
                for p in range(N_HEADS // 2)]
        keys += [[kd_ref[kh, pl.ds(start, win), :], kcd_ref[kh]] for kh in range(N_KV_HEADS)]
        vals += [[vd_ref[kh, pl.ds(start, win), :], vcd_ref[kh]] for kh in range(N_KV_HEADS)]
        masks += [[mask, None]] * N_KV_HEADS
    sinks = [sink_ref[h] * LOG2_E for h in range(N_HEADS)] * n_sub
    for t, o in enumerate(_heads_attention(qps, keys, vals, masks, sinks)):
        sub, p = divmod(t, N_HEADS // 2)
        o_ref[0, sub * blk:(sub + 1) * blk, p * LANES:(p + 1) * LANES] = o.astype(o_ref.dtype)


def _window_attention(p_x, p_c, ctx_kv_cols, sink, tabs):
    b, s, _ = p_x.shape
    l = p_c.shape[1]
    cos, sin = tabs
    blk = min(s, ATTN_BLOCKS_PER_STEP * ATTN_BLOCK)
    assert s % blk == 0
    kcol, vcol = COL_K // KV_W, COL_V // KV_W
    kccol, vccol = ctx_kv_cols[0] // KV_W, ctx_kv_cols[1] // KV_W
    return pl.pallas_call(
        functools.partial(_win_attn_kernel, seq=s),
        grid=(b, s // blk),
        in_specs=[pl.BlockSpec(memory_space=pltpu.SMEM),
                  pl.BlockSpec((1, blk, Q_W), lambda i, j: (i, j, COL_Q // Q_W)),
                  pl.BlockSpec((1, s, KV_W), lambda i, j: (i, 0, kcol)),
                  pl.BlockSpec((1, s, KV_W), lambda i, j: (i, 0, vcol)),
                  pl.BlockSpec((1, l, KV_W), lambda i, j: (i, 0, kccol)),
                  pl.BlockSpec((1, l, KV_W), lambda i, j: (i, 0, vccol)),
                  pl.BlockSpec((blk, LANES), lambda i, j: (j, 0)),
                  pl.BlockSpec((blk, LANES), lambda i, j: (j, 0)),
                  pl.BlockSpec((s, LANES), lambda i, j: (0, 0)),
                  pl.BlockSpec((s, LANES), lambda i, j: (0, 0))],
        out_specs=pl.BlockSpec((1, blk, Q_W), lambda i, j: (i, j, 0)),
        out_shape=jax.ShapeDtypeStruct((b, s, Q_W), BF16),
        scratch_shapes=[pltpu.VMEM((N_KV_HEADS, s, LANES), BF16), pltpu.VMEM((N_KV_HEADS, s, LANES), BF16),
                        pltpu.VMEM((N_KV_HEADS, l, LANES), BF16), pltpu.VMEM((N_KV_HEADS, l, LANES), BF16)],
        name="window_attention",
    )(sink, p_x, p_x, p_x, p_c, p_c, cos, sin, cos, sin)


def _ctx_attn_kernel(sink_ref, q_ref, k_ref, v_ref, o_ref):
    kd = [t.astype(BF16) for t in _dup_heads(k_ref[0].astype(F32))]
    vd = [t.astype(BF16) for t in _dup_heads(v_ref[0].astype(F32))]
    scale = HEAD_DIM ** -0.5 * LOG2_E
    qps = [(q_ref[0, :, p * LANES:(p + 1) * LANES].astype(F32) * scale).astype(BF16) for p in range(N_HEADS // 2)]
    sinks = [sink_ref[h] * LOG2_E for h in range(N_HEADS)]
    outs = _heads_attention(qps, [[k] for k in kd], [[v] for v in vd], [[None]] * N_KV_HEADS, sinks)
    for p, o in enumerate(outs):
        o_ref[0, :, p * LANES:(p + 1) * LANES] = o.astype(o_ref.dtype)


def _context_attention(p_c, sink):
    b, l, _ = p_c.shape
    return pl.pallas_call(
        _ctx_attn_kernel,
        grid=(b,),
        in_specs=[pl.BlockSpec(memory_space=pltpu.SMEM),
                  pl.BlockSpec((1, l, Q_W), lambda i: (i, 0, COL_Q // Q_W)),
                  pl.BlockSpec((1, l, KV_W), lambda i: (i, 0, COL_K // KV_W)),
                  pl.BlockSpec((1, l, KV_W), lambda i: (i, 0, COL_V // KV_W))],
        out_specs=pl.BlockSpec((1, l, Q_W), lambda i: (i, 0, 0)),
        out_shape=jax.ShapeDtypeStruct((b, l, Q_W), BF16),
        name="context_attention",
    )(sink, p_c, p_c, p_c)


def _merge_kernel(ga_ref, gb_ref, gc_ref,
                  u_ref, u_p_ref, u_n_ref, pz_ref, pz_p_ref, pz_n_ref,
                  attn_ref, x_ref, g1_ref, sh2_ref, sc2_ref, n2g_ref,
                  dw_ref, dwb_ref, lng_ref, lnb_ref, wpool_ref, pscale_ref,
                  wa_ref, wb_ref, wc_ref, wo_ref, wrt_ref,
                  xo_ref, h2_ref, afft_ref,
                  uwin_ref, zwin_ref, *, seq):
    t = pl.program_id(1)
    tt = x_ref.shape[1]
    has_prev = (t > 0).astype(F32)
    has_next = (t < pl.num_programs(1) - 1).astype(F32)

    uwin_ref[0:HALO, :] = u_p_ref[0].astype(F32) * has_prev
    uwin_ref[HALO:HALO + tt, :] = u_ref[0].astype(F32)
    uwin_ref[HALO + tt:, :] = u_n_ref[0].astype(F32) * has_next
    first = HALO - CONV_PAD
    rows = tt + 2 * HALO
    acc_cols = []
    for cb in range(CONV_CH // LANES):
        cols = slice(cb * LANES, (cb + 1) * LANES)
        window = uwin_ref[:, cols]
        acc_c = jnp.zeros((tt, LANES), F32) + dwb_ref[:, cols]
        for shift in range(F32_SUBLANES):
            taps = [k for k in range(CONV_K) if (first + k) % F32_SUBLANES == shift]
            if not taps:
                continue
            shifted = window if shift == 0 else pltpu.roll(window, rows - shift, 0)
            for k in taps:
                off = first + k - shift
                acc_c = acc_c + shifted[off:off + tt] * dw_ref[k:k + 1, cols]
        acc_cols.append(acc_c)
    acc = jnp.concatenate(acc_cols, axis=-1)
    mu = jnp.mean(acc, axis=-1, keepdims=True)
    cen = acc - mu
    var = jnp.mean(cen * cen, axis=-1, keepdims=True)
    ln = cen * lax.rsqrt(var + EPS) * lng_ref[...] + lnb_ref[...]
    feat_b = (ln * _sigmoid(ln)).astype(BF16)

    zwin_ref[0:HALO, :] = pz_p_ref[0].astype(F32) * has_prev
    zwin_ref[HALO:HALO + tt, :] = pz_ref[0].astype(F32)
    zwin_ref[HALO + tt:, :] = pz_n_ref[0].astype(F32) * has_next
    tpos = t * tt + lax.broadcasted_iota(I32, (tt, 1), 0)
    pooled = []
    for gi, w in enumerate(POOL_WINDOWS):
        cols = slice(gi * POOL_GROUP, (gi + 1) * POOL_GROUP)
        tot = zwin_ref[pl.ds(HALO - w // 2, tt), cols]
        for d in range(1 - w // 2, w - w // 2):
            tot = tot + zwin_ref[pl.ds(HALO + d, tt), cols]
        cnt = (jnp.minimum(tpos + (w - w // 2), seq) - jnp.maximum(tpos - w // 2, 0)).astype(F32)
        diff = tot / cnt - zwin_ref[pl.ds(HALO, tt), cols]
        pooled.append(_dot(diff.astype(BF16), wpool_ref[gi]))
    feat_c = (jnp.concatenate(pooled, axis=-1) * pscale_ref[...]).astype(BF16)

    y_a = _dot(attn_ref[0], wa_ref[...])
    y_b = _dot(feat_b, wb_ref[...])
    y_c = _dot(feat_c, wc_ref[...])
    merged = ga_ref[0].astype(F32) * y_a + gb_ref[0].astype(F32) * y_b + gc_ref[0].astype(F32) * y_c
    xn = x_ref[0] + g1_ref[0] * _dot(merged.astype(BF16), wo_ref[...])
    xo_ref[0] = xn

    h2 = _rms_mod(xn, n2g_ref[...], sh2_ref[0], sc2_ref[0])
    h2_hi, h2_lo = _hi_lo(h2)
    h2_ref[0] = h2_hi
    ne = afft_ref.shape[1]
    by_hi = _dot_nt(wrt_ref[...], h2_hi)
    logits_t = by_hi[:ne] + by_hi[ne:] + _dot_nt(wrt_ref[:ne, :], h2_lo)
    et = jnp.exp(logits_t - jnp.max(logits_t, axis=0, keepdims=True))
    afft_ref[0] = et / jnp.sum(et, axis=0, keepdims=True)


def _merge(p, attn, x, g1, sh2, sc2, n2g, lw, tt):
    b, s, d = x.shape
    nh = tt // HALO
    last_h = s // HALO - 1
    e = N_EXPERTS

    def main(width, col):
        return pl.BlockSpec((1, tt, width), lambda i, j: (i, j, col))

    def prev(col):
        return pl.BlockSpec((1, HALO, CONV_CH), lambda i, j: (i, jnp.maximum(j * nh - 1, 0), col))

    def nxt(col):
        return pl.BlockSpec((1, HALO, CONV_CH), lambda i, j: (i, jnp.minimum((j + 1) * nh, last_h), col))

    def per_batch():
        return pl.BlockSpec((1, 1, d), lambda i, j: (i, 0, 0))

    def const(shape):
        return pl.BlockSpec(shape, lambda i, j: (0,) * len(shape))

    cu, pz = COL_CONV_U // CONV_CH, COL_POOL // CONV_CH
    in_specs = [main(d, 0), main(d, 1), main(d, 2),
                main(CONV_CH, cu), prev(cu), nxt(cu), main(POOL_CH, pz), prev(pz), nxt(pz),
                pl.BlockSpec((1, tt, Q_W), lambda i, j: (i, j, 0)),
                pl.BlockSpec((1, tt, d), lambda i, j: (i, j, 0)),
                per_batch(), per_batch(), per_batch(), const((1, d)),
                const((CONV_K, CONV_CH)), const((1, CONV_CH)), const((1, CONV_CH)), const((1, CONV_CH)),
                const((len(POOL_WINDOWS), POOL_GROUP, POOL_GROUP)), const((1, POOL_CH)),
                const((Q_W, d)), const((CONV_CH, d)), const((POOL_CH, d)), const((d, d)),
                const((2 * e, d))]
    out_specs = [pl.BlockSpec((1, tt, d), lambda i, j: (i, j, 0)),
                 pl.BlockSpec((1, tt, d), lambda i, j: (i, j, 0)),
                 pl.BlockSpec((1, e, tt), lambda i, j: (i, 0, j))]
    out_shape = [jax.ShapeDtypeStruct((b, s, d), F32), jax.ShapeDtypeStruct((b, s, d), BF16),
                 jax.ShapeDtypeStruct((b, e, s), F32)]
    return pl.pallas_call(
        functools.partial(_merge_kernel, seq=s),
        grid=(b, s // tt),
        in_specs=in_specs, out_specs=out_specs, out_shape=out_shape,
        scratch_shapes=[pltpu.VMEM((tt + 2 * HALO, CONV_CH), F32), pltpu.VMEM((tt + 2 * HALO, POOL_CH), F32)],
        name="mix_merge",
    )(p, p, p, p, p, p, p, p, p, attn, x, g1, sh2, sc2, n2g,
      lw['conv_dw'], lw['conv_dw_b'], lw['conv_ln_g'], lw['conv_ln_b'], lw['w_pool'], lw['pool_scale'],
      lw['w_attn_o'], lw['w_conv_o'], lw['w_pool_o'], lw['w_out'], lw['w_router_t'])


def _topk_kernel(afft_ref, slot_ref, slott_ref, offs_ref, *, cap, blk):
    a = afft_ref[0]
    e, s = a.shape

    def keeps_cap(cand):
        return jnp.sum((a >= cand).astype(F32), axis=-1, keepdims=True) >= cap

    tiny = jnp.full((e, 1), F32_TINY, F32)
    thr = jnp.where(keeps_cap(tiny), tiny, 0.0)
    for step in (64, 32, 16, 8, 4, 2, 1):
        cand = thr * float(2 ** step)
        thr = jnp.where(keeps_cap(cand), cand, thr)
    delta = thr
    for _ in range(F32_MANTISSA_BITS):
        delta = delta * 0.5
        cand = thr + delta
        thr = jnp.where(keeps_cap(cand), cand, thr)
    gt = a > thr
    eq = a == thr
    need = cap - jnp.sum(gt.astype(F32), axis=-1, keepdims=True)

    r = lax.broadcasted_iota(I32, (blk, blk), 0)
    c = lax.broadcasted_iota(I32, (blk, blk), 1)
    upper = (r < c).astype(BF16)
    eye = (r == c).astype(F32)

    def prefix(mask_f32):
        carry = jnp.zeros((e, 1), F32)
        parts = []
        for j in range(s // blk):
            m = mask_f32[:, j * blk:(j + 1) * blk]
            parts.append(_dot(m.astype(BF16), upper) + carry)
            carry = carry + jnp.sum(m, axis=-1, keepdims=True)
        return jnp.concatenate(parts, axis=-1)

    sel = gt | (eq & (prefix(eq.astype(F32)) < need))
    pos = prefix(sel.astype(F32))
    slot = jnp.where(sel, pos, -1.0)
    slot_ref[0] = slot.astype(I32)
    for j in range(s // blk):
        slott_ref[0, j * blk:(j + 1) * blk, :] = _dot_nt(eye, slot[:, j * blk:(j + 1) * blk],
                                                        precision=HIGHEST).astype(I32)
    tok = lax.broadcasted_iota(I32, (s, LANES), 0)
    col = lax.broadcasted_iota(I32, (s, LANES), 1)
    before = (tok < col * blk).astype(BF16)
    offs_ref[0] = _dot(sel.astype(BF16), before).astype(I32)


def _topk(afft, cap):
    b, e, s = afft.shape
    blk = min(s, TOKEN_CHUNK)
    slot, slott, offs = pl.pallas_call(
        functools.partial(_topk_kernel, cap=cap, blk=blk),
        grid=(b,),
        in_specs=[pl.BlockSpec((1, e, s), lambda i: (i, 0, 0))],
        out_specs=[pl.BlockSpec((1, e, s), lambda i: (i, 0, 0)),
                   pl.BlockSpec((1, s, e), lambda i: (i, 0, 0)),
                   pl.BlockSpec((1, e, LANES), lambda i: (i, 0, 0))],
        out_shape=[jax.ShapeDtypeStruct((b, e, s), I32), jax.ShapeDtypeStruct((b, s, e), I32),
                   jax.ShapeDtypeStruct((b, e, LANES), I32)],
        name="expert_choice_topk",
    )(afft)
    return slot, slott, offs[:, :, :s // blk + 1]


def _slot_windows(offs_ref, idx, win):
    lo = offs_ref[idx]
    hi = offs_ref[idx + 1]
    first = lo // win
    return first, jnp.where(hi > lo, (hi - 1) // win - first + 1, 0)


def _gather_rows(onehot, h, gate_row):
    picked = _dot(onehot.astype(BF16), h)
    gates = jnp.sum(jnp.where(onehot, gate_row, 0.0), axis=-1, keepdims=True)
    return picked, gates


def _ffn_kernel(offs_ref, slot_ref, afft_ref, h_ref, slotc_ref, cafft_ref, hc_ref, wg_ref, wu_ref, wd_ref,
                ye_ref, yec_ref, wg_s, wu_s, wd_s, xe_s, g_s, *, chunk, group, win, rows):
    ex = pl.program_id(0)
    bi = pl.program_id(1)
    nb = pl.num_programs(1)
    cap = xe_s.shape[0]

    @pl.when(bi == 0)
    def _():
        def cast(i, carry):
            sl = pl.ds(pl.multiple_of(i * rows, rows), rows)
            wg_s[sl, :] = wg_ref[0, 0, sl, :].astype(BF16)
            wu_s[sl, :] = wu_ref[0, 0, sl, :].astype(BF16)
            wd_s[sl, :] = wd_ref[0, 0, sl, :].astype(BF16)
            return carry
        lax.fori_loop(0, wg_s.shape[0] // rows, cast, 0)

    def ffn(xe):
        a = _dot(xe, wg_s[...])
        u = _dot(xe, wu_s[...])
        hid = (a * _sigmoid(a) * u).astype(BF16)
        return _dot(hid, wd_s[...])

    nch = h_ref.shape[1] // chunk
    base = (bi * pl.num_programs(0) + ex) * (nch + 1)
    xe_s[...] = jnp.zeros_like(xe_s)
    g_s[...] = jnp.zeros_like(g_s)
    span = group * chunk
    starts = []
    fits = None
    for p in range(nch // group):
        lo = offs_ref[base + p * group]
        hi = offs_ref[base + (p + 1) * group]
        a = pl.multiple_of(jnp.minimum((lo // F32_SUBLANES) * F32_SUBLANES, cap - win), F32_SUBLANES)
        starts.append(a)
        fits = (hi - a <= win) if fits is None else jnp.logical_and(fits, hi - a <= win)

    def add_window(a, sl, width):
        onehot = (slot_ref[0, pl.ds(ex, 1), sl] - a) == lax.broadcasted_iota(I32, (win, width), 0)
        picked, gates = _gather_rows(onehot, h_ref[0, sl, :], afft_ref[0, pl.ds(ex, 1), sl])
        xe_s[pl.ds(a, win), :] += picked
        g_s[pl.ds(a, win), :] += gates

    @pl.when(fits)
    def _():
        for p, a in enumerate(starts):
            add_window(a, slice(p * span, (p + 1) * span), span)

    @pl.when(jnp.logical_not(fits))
    def _():
        def per_chunk(j, carry):
            first, nwin = _slot_windows(offs_ref, base + j, win)
            sl = pl.ds(pl.multiple_of(j * chunk, chunk), chunk)

            def window(w, c):
                add_window(pl.multiple_of((first + w) * win, win), sl, chunk)
                return c
            lax.fori_loop(0, nwin, window, 0)
            return carry
        lax.fori_loop(0, nch, per_chunk, 0)

    ye_ref[0] = (ffn(xe_s[...].astype(BF16)) * g_s[...]).astype(ye_ref.dtype)

    if yec_ref is not None:
        @pl.when(bi == nb - 1)
        def _():
            rowc = lax.broadcasted_iota(I32, (yec_ref.shape[1], hc_ref.shape[0]), 0)
            picked, gates = _gather_rows(slotc_ref[0] == rowc, hc_ref[...], cafft_ref[0])
            yec_ref[0] = (ffn(picked.astype(BF16)) * gates).astype(yec_ref.dtype)


def _expert_ffn(layer, slot, offs, afft, h2, wg, wu, wd, cap, ctx_part=None):
    b, e, s = slot.shape
    d = h2.shape[2]
    f = wg.shape[3]
    assert f == d
    chunk = min(s, TOKEN_CHUNK)
    nch = s // chunk
    group = 2 if nch % 2 == 0 else 1
    win = min(cap, GATHER_WINDOW)
    assert cap % win == 0

    def w_spec(from_step):
        if from_step < 1:
            return pl.BlockSpec((1, 1, d, f), lambda j, i, o: (layer, j, 0, 0))
        return pl.BlockSpec((1, 1, d, f),
                            lambda j, i, o: (layer, jnp.minimum(j + jnp.where(i >= from_step, 1, 0), e - 1), 0, 0))

    row_spec = pl.BlockSpec((1, e, s), lambda j, i, o: (i, 0, 0))
    in_specs = [row_spec, row_spec, pl.BlockSpec((1, s, d), lambda j, i, o: (i, 0, 0))]
    out_specs = [pl.BlockSpec((1, cap, d), lambda j, i, o: (i * e + j, 0, 0))]
    out_shape = [jax.ShapeDtypeStruct((b * e, cap, d), BF16)]
    args = [slot, afft, h2]
    body = functools.partial(_ffn_kernel, chunk=chunk, group=group, win=win, rows=128)
    if ctx_part is None:
        def kern(offs_ref, slot_ref, afft_ref, h_ref, wg_ref, wu_ref, wd_ref, ye_ref, *scratch):
            body(offs_ref, slot_ref, afft_ref, h_ref, None, None, None, wg_ref, wu_ref, wd_ref, ye_ref, None,
                 *scratch)
    else:
        slot_c, afft_c, h_c, rows_c = ctx_part
        n_c = h_c.shape[0]
        rowc_spec = pl.BlockSpec((1, 1, n_c), lambda j, i, o: (j, 0, 0))
        in_specs += [rowc_spec, rowc_spec, pl.BlockSpec((n_c, d), lambda j, i, o: (0, 0))]
        out_specs.append(pl.BlockSpec((1, rows_c, d), lambda j, i, o: (j, 0, 0)))
        out_shape.append(jax.ShapeDtypeStruct((e, rows_c, d), BF16))
        args += [slot_c, afft_c, h_c]
        kern = body
    outs = pl.pallas_call(
        kern,
        grid_spec=pltpu.PrefetchScalarGridSpec(
            num_scalar_prefetch=1, grid=(e, b),
            in_specs=in_specs + [w_spec(min(k, b - 1)) for k in (1, 2, 3)], out_specs=out_specs,
            scratch_shapes=[pltpu.VMEM((d, f), BF16), pltpu.VMEM((d, f), BF16), pltpu.VMEM((f, d), BF16),
                            pltpu.VMEM((cap, d), F32), pltpu.VMEM((cap, 1), F32)]),
        out_shape=out_shape,
        name="expert_ffn",
    )(offs.reshape(-1), *args, wg, wu, wd)
    ye = outs[0].reshape(b, e * cap, d)
    return ye if ctx_part is None else (ye, outs[1])


def _combine_kernel(offs_ref, slott_ref, ye_ref, x_ref, g2_ref, fg_ref, o_ref, acc_s, *, cap, win, final_norm):
    bi = pl.program_id(0)
    tt = acc_s.shape[0]
    n_sub = x_ref.shape[1] // tt
    nch = pl.num_programs(1) * n_sub
    per_block = LANES // win

    for sub in range(n_sub):
        j = pl.program_id(1) * n_sub + sub
        tok = slice(sub * tt, (sub + 1) * tt)

        starts = []
        fits = None
        for ex in range(N_EXPERTS):
            idx = (bi * N_EXPERTS + ex) * (nch + 1) + j
            a = pl.multiple_of(jnp.minimum((offs_ref[idx] // BF16_SUBLANES) * BF16_SUBLANES, cap - win),
                               BF16_SUBLANES)
            starts.append(a)
            ok = offs_ref[idx + 1] - a <= win
            fits = ok if fits is None else jnp.logical_and(fits, ok)

        @pl.when(fits)
        def _(starts=starts, tok=tok):
            lane = lax.broadcasted_iota(I32, (tt, LANES), 1)
            blocks = []
            for blk in range(N_EXPERTS // per_block):
                target = None
                for q in range(per_block):
                    ex = blk * per_block + q
                    t = slott_ref[0, tok, ex:ex + 1] - starts[ex] + q * win
                    target = t if target is None else jnp.where(lane >= q * win, t, target)
                blocks.append((target == lane).astype(BF16))
            rows = [ye_ref[0, pl.ds(ex * cap + starts[ex], win), :] for ex in range(N_EXPERTS)]
            acc_s[...] = _dot(jnp.concatenate(blocks, axis=1), jnp.concatenate(rows, axis=0))

        @pl.when(jnp.logical_not(fits))
        def _(j=j, tok=tok):
            lane = lax.broadcasted_iota(I32, (tt, win), 1)
            acc_s[...] = jnp.zeros_like(acc_s)
            for ex in range(N_EXPERTS):
                first, nwin = _slot_windows(offs_ref, (bi * N_EXPERTS + ex) * (nch + 1) + j, win)

                def window(w, c, ex=ex, first=first):
                    a = pl.multiple_of((first + w) * win, win)
                    onehot = ((slott_ref[0, tok, ex:ex + 1] - a) == lane).astype(BF16)
                    acc_s[...] += _dot(onehot, ye_ref[0, pl.ds(ex * cap + a, win), :])
                    return c
                lax.fori_loop(0, nwin, window, 0)

        out = x_ref[0, tok, :] + g2_ref[0] * acc_s[...]
        if final_norm:
            out = out * lax.rsqrt(jnp.mean(out * out, axis=-1, keepdims=True) + EPS) * fg_ref[...]
        o_ref[0, tok, :] = out


def _combine(slott, offs, ye, x, g2, fg, cap, final_norm):
    b, s, d = x.shape
    e = N_EXPERTS
    chunk = min(s, TOKEN_CHUNK)
    tt = min(s, COMBINE_CHUNKS * chunk)
    win = min(cap, SCATTER_WINDOW)
    assert cap % win == 0 and LANES % win == 0 and e % (LANES // win) == 0 and s % tt == 0
    return pl.pallas_call(
        functools.partial(_combine_kernel, cap=cap, win=win, final_norm=final_norm),
        grid_spec=pltpu.PrefetchScalarGridSpec(
            num_scalar_prefetch=1, grid=(b, s // tt),
            in_specs=[pl.BlockSpec((1, tt, e), lambda i, j, o: (i, j, 0)),
                      pl.BlockSpec((1, e * cap, d), lambda i, j, o: (i, 0, 0)),
                      pl.BlockSpec((1, tt, d), lambda i, j, o: (i, j, 0)),
                      pl.BlockSpec((1, 1, d), lambda i, j, o: (i, 0, 0)),
                      pl.BlockSpec((1, d), lambda i, j, o: (0, 0))],
            out_specs=pl.BlockSpec((1, tt, d), lambda i, j, o: (i, j, 0)),
            scratch_shapes=[pltpu.VMEM((chunk, d), F32)]),
        out_shape=jax.ShapeDtypeStruct((b, s, d), F32),
        name="moe_combine",
    )(offs.reshape(-1), slott, ye, x, g2, fg)


def _split_hi_lo(w):
    return jnp.concatenate(_hi_lo(w), axis=0)


def _permute_in_cols(w):
    o_k = Q_W
    o_v = o_k + KV_W
    o_ca = o_v + KV_W
    o_cg = o_ca + CONV_CH
    o_p = o_cg + CONV_CH
    o_g = o_p + POOL_CH
    return jnp.concatenate([w[:, o_g:], w[:, :o_k], w[:, o_ca:o_cg], w[:, o_cg:o_p], w[:, o_p:o_g],
                            w[:, o_k:o_v], w[:, o_v:o_ca]], axis=1)


def kernel(x, c, ctx, c_ctx, norm1_g, norm2_g, w_mod, b_mod, w_in, attn_sink, w_attn_o, conv_dw, conv_dw_b,
           conv_ln_g, conv_ln_b, w_conv_o, w_pool, pool_scale, w_pool_o, w_out, w_router, w_e_gate, w_e_up,
           w_e_down, final_norm_g):
    b, s, d = x.shape
    l = ctx.shape[1]
    depth = w_in.shape[0]
    assert d == _D_MODEL and w_in.shape[2] == IN_W and CONV_CH == POOL_CH

    tabs = _rope_tables(s)
    cc = jnp.zeros((8, d), F32).at[:b].set(c).at[b].set(c_ctx)
    mod = _modulation(cc, w_mod, b_mod)
    fg = final_norm_g.reshape(1, d)

    for layer in range(depth):
        last = layer == depth - 1
        mx = mod[layer, :b].reshape(b, 1, 6, d)
        sh1, sc1, g1, sh2, sc2, g2 = [mx[:, :, i] for i in range(6)]
        mc = jnp.broadcast_to(mod[layer, b].reshape(1, 1, 6, d), (b, 1, 6, d))
        csh1, csc1, cg1, csh2, csc2, cg2 = [mc[:, :, i] for i in range(6)]
        n1g = norm1_g[layer].reshape(1, d)
        n2g = norm2_g[layer].reshape(1, d)
        w_in_l = _permute_in_cols(w_in[layer]).astype(BF16)
        lw = {'conv_dw': conv_dw[layer], 'conv_dw_b': conv_dw_b[layer].reshape(1, -1),
              'conv_ln_g': conv_ln_g[layer].reshape(1, -1), 'conv_ln_b': conv_ln_b[layer].reshape(1, -1),
              'w_pool': w_pool[layer].astype(BF16), 'pool_scale': pool_scale[layer].reshape(1, -1),
              'w_attn_o': w_attn_o[layer].astype(BF16), 'w_conv_o': w_conv_o[layer].astype(BF16),
              'w_pool_o': w_pool_o[layer].astype(BF16), 'w_out': w_out[layer].astype(BF16),
              'w_router_t': _split_hi_lo(w_router[layer].T)}

        p_x = _inproj(x, n1g, sh1, sc1, w_in_l, tm=min(s, INPROJ_TILE))
        if last:
            p_c = _inproj(ctx, n1g, csh1, csc1, w_in_l[:, WCOL_K:], tm=l, mixer_epilogue=False)
            ctx_kv_cols = (0, KV_W)
        else:
            p_c = _inproj(ctx, n1g, csh1, csc1, w_in_l, tm=l)
            ctx_kv_cols = (COL_K, COL_V)
        attn_x = _window_attention(p_x, p_c, ctx_kv_cols, attn_sink[layer], tabs)
        x_mid, h2, afft = _merge(p_x, attn_x, x, g1, sh2, sc2, n2g, lw, tt=min(s, MERGE_TILE))
        if not last:
            attn_c = _context_attention(p_c, attn_sink[layer])
            c_mid, ch2, cafft = _merge(p_c, attn_c, ctx, cg1, csh2, csc2, n2g, lw, tt=l)
        cap = (CAPACITY_FACTOR * s) // N_EXPERTS
        slot, slott, offs = _topk(afft, cap)
        if last:
            ye = _expert_ffn(layer, slot, offs, afft, h2, w_e_gate, w_e_up, w_e_down, cap)
        else:
            cap_c = (CAPACITY_FACTOR * l) // N_EXPERTS
            cslot, cslott, coffs = _topk(cafft, cap_c)
            sample_base = (jnp.arange(b, dtype=I32) * cap_c)[:, None, None]
            cslot_all = jnp.where(cslot >= 0, cslot + sample_base, -1)
            cslot_all = cslot_all.transpose(1, 0, 2).reshape(N_EXPERTS, 1, b * l)
            cafft_all = cafft.transpose(1, 0, 2).reshape(N_EXPERTS, 1, b * l)
            ye, yec = _expert_ffn(layer, slot, offs, afft, h2, w_e_gate, w_e_up, w_e_down, cap,
                                  ctx_part=(cslot_all, cafft_all, ch2.reshape(b * l, d), b * cap_c))
            yec = yec.reshape(N_EXPERTS, b, cap_c, d).transpose(1, 0, 2, 3).reshape(b, N_EXPERTS * cap_c, d)
            ctx = _combine(cslott, coffs, yec, c_mid, cg2, fg, cap_c, False)
        x = _combine(slott, offs, ye, x_mid, g2, fg, cap, last)
    return x
```

```python
import functools

import jax
import jax.numpy as jnp
import numpy as np
from jax import lax
from jax.experimental import pallas as pl
from jax.experimental.pallas import tpu as pltpu

F32 = jnp.float32
BF16 = jnp.bfloat16
I32 = jnp.int32

EPS = 1e-6
GRID_W = 64
N_HEADS = 8
N_KV_HEADS = 2
HEAD_DIM = 64
GQA_GROUP = N_HEADS // N_KV_HEADS
WINDOW = 128
ATTN_BLOCK = 128
ROPE_BASE = 10000.0
ROPE_PAIRS = HEAD_DIM // 4
CONV_CH = 512
CONV_K = 31
CONV_PAD = CONV_K // 2
POOL_WINDOWS = (2, 4, 8, 16)
POOL_GROUP = 128
POOL_CH = POOL_GROUP * len(POOL_WINDOWS)
N_EXPERTS = 16
CAPACITY_FACTOR = 2
Q_W = N_HEADS * HEAD_DIM
KV_W = N_KV_HEADS * HEAD_DIM

LANES = 128
HALO = 16
HIGHEST = lax.Precision.HIGHEST
LOG2_E = 1.4426950408889634
F32_TINY = 2.0 ** -126
F32_MANTISSA_BITS = 23
TOKEN_CHUNK = 256
GATHER_WINDOW = 128
SCATTER_WINDOW = 64
F32_SUBLANES = 8
BF16_SUBLANES = 16
ATTN_LOOKAHEAD = 3
ATTN_BLOCKS_PER_STEP = 8
COMBINE_CHUNKS = 4
INPROJ_TILE = 1024
MERGE_TILE = 512

_D_MODEL = 1024
N_GATE_COLS = 3 * _D_MODEL
WCOL_Q = N_GATE_COLS
WCOL_CONV_A = WCOL_Q + Q_W
WCOL_CONV_G = WCOL_CONV_A + CONV_CH
WCOL_POOL = WCOL_CONV_G + CONV_CH
WCOL_K = WCOL_POOL + POOL_CH
IN_W = WCOL_K + 2 * KV_W
COL_Q = N_GATE_COLS
COL_CONV_U = COL_Q + Q_W
COL_POOL = COL_CONV_U + CONV_CH
COL_K = COL_POOL + POOL_CH
COL_V = COL_K + KV_W
OUT_W = COL_V + KV_W
GATE_CHUNK = 768


def _dot(a, b):
    return jnp.dot(a, b, preferred_element_type=F32)


def _dot_nt(a, b, precision=None):
    return lax.dot_general(a, b, (((1,), (1,)), ((), ())), preferred_element_type=F32, precision=precision)


def _sigmoid(v):
    return 0.5 * jnp.tanh(0.5 * v) + 0.5


def _rms_mod(x, g, sh, sc):
    y = x * lax.rsqrt(jnp.mean(x * x, axis=-1, keepdims=True) + EPS) * g
    return y * (1.0 + sc) + sh


def _hi_lo(v):
    hi = v.astype(BF16)
    return hi, (v - hi.astype(F32)).astype(BF16)


def _mod_kernel(c_ref, w_ref, b_ref, o_ref):
    c = c_ref[...]
    a_hi, a_lo = _hi_lo(c * _sigmoid(c))
    w_hi, w_lo = _hi_lo(w_ref[0])
    o_ref[0] = _dot(a_hi, w_hi) + _dot(a_lo, w_hi) + _dot(a_hi, w_lo) + b_ref[0]


def _modulation(cc, w_mod, b_mod, tn=1536):
    depth, d, n = w_mod.shape
    rows = cc.shape[0]
    return pl.pallas_call(
        _mod_kernel,
        grid=(depth, n // tn),
        in_specs=[pl.BlockSpec((rows, d), lambda l, j: (0, 0)),
                  pl.BlockSpec((1, d, tn), lambda l, j: (l, 0, j)),
                  pl.BlockSpec((1, 1, tn), lambda l, j: (l, 0, j))],
        out_specs=pl.BlockSpec((1, rows, tn), lambda l, j: (l, 0, j)),
        out_shape=jax.ShapeDtypeStruct((depth, rows, n), F32),
        name="modulation",
    )(cc, w_mod, b_mod.reshape(depth, 1, n))


def _inproj_kernel(x_ref, g_ref, sh_ref, sc_ref, w_ref, o_ref, *, mixer_epilogue):
    h = _rms_mod(x_ref[0], g_ref[...], sh_ref[0], sc_ref[0]).astype(BF16)
    if not mixer_epilogue:
        o_ref[0] = _dot(h, w_ref[...]).astype(o_ref.dtype)
        return
    for c0 in range(0, N_GATE_COLS, GATE_CHUNK):
        cols = slice(c0, c0 + GATE_CHUNK)
        o_ref[0, :, cols] = _sigmoid(_dot(h, w_ref[:, cols])).astype(o_ref.dtype)
    o_ref[0, :, COL_Q:COL_CONV_U] = _dot(h, w_ref[:, WCOL_Q:WCOL_CONV_A]).astype(o_ref.dtype)
    glu = _dot(h, w_ref[:, WCOL_CONV_A:WCOL_POOL])
    o_ref[0, :, COL_CONV_U:COL_POOL] = (glu[:, :CONV_CH] * _sigmoid(glu[:, CONV_CH:])).astype(o_ref.dtype)
    o_ref[0, :, COL_POOL:] = _dot(h, w_ref[:, WCOL_POOL:]).astype(o_ref.dtype)


def _inproj(x, g, sh, sc, w, tm, mixer_epilogue=True):
    b, s, d = x.shape
    n = w.shape[1]
    n_out = OUT_W if mixer_epilogue else n
    assert not mixer_epilogue or n == IN_W
    return pl.pallas_call(
        functools.partial(_inproj_kernel, mixer_epilogue=mixer_epilogue),
        grid=(b, s // tm),
        in_specs=[pl.BlockSpec((1, tm, d), lambda i, j: (i, j, 0)),
                  pl.BlockSpec((1, d), lambda i, j: (0, 0)),
                  pl.BlockSpec((1, 1, d), lambda i, j: (i, 0, 0)),
                  pl.BlockSpec((1, 1, d), lambda i, j: (i, 0, 0)),
                  pl.BlockSpec((d, n), lambda i, j: (0, 0))],
        out_specs=pl.BlockSpec((1, tm, n_out), lambda i, j: (i, j, 0)),
        out_shape=jax.ShapeDtypeStruct((b, s, n_out), BF16),
        name="inproj",
    )(x, g, sh, sc, w)


def _rope_tables(s):
    t = np.arange(s)
    row = (t // GRID_W).astype(np.float32)
    col = (t % GRID_W).astype(np.float32)
    freqs = jnp.asarray(ROPE_BASE, F32) ** (-jnp.arange(ROPE_PAIRS, dtype=F32) / ROPE_PAIRS)
    ang_r = jnp.asarray(row)[:, None] * freqs
    ang_c = jnp.asarray(col)[:, None] * freqs
    cos_h = jnp.concatenate([jnp.cos(ang_r), jnp.cos(ang_r), jnp.cos(ang_c), jnp.cos(ang_c)], axis=-1)
    sin_h = jnp.concatenate([-jnp.sin(ang_r), jnp.sin(ang_r), -jnp.sin(ang_c), jnp.sin(ang_c)], axis=-1)
    return jnp.tile(cos_h, (1, LANES // HEAD_DIM)), jnp.tile(sin_h, (1, LANES // HEAD_DIM))


def _rope(x, cos, sin_signed):
    lane = lax.broadcasted_iota(I32, x.shape, 1)
    low = (lane & (2 * ROPE_PAIRS - 1)) < ROPE_PAIRS
    partner = jnp.where(low, pltpu.roll(x, LANES - ROPE_PAIRS, 1), pltpu.roll(x, ROPE_PAIRS, 1))
    return x * cos + partner * sin_signed


def _softmax_pv(s_list, v_list, sink):
    m = sink
    for s in s_list:
        m = jnp.maximum(m, jnp.max(s, axis=-1, keepdims=True))
    denom = jnp.exp2(sink - m)
    o = None
    for s, v in zip(s_list, v_list):
        e = jnp.exp2(s - m)
        denom = denom + jnp.sum(e, axis=-1, keepdims=True)
        pv = _dot(e.astype(BF16), v)
        o = pv if o is None else o + pv
    return o / denom


def _lane_lo(shape):
    return lax.broadcasted_iota(I32, shape, 1) < HEAD_DIM


def _dup_heads(t):
    swapped = pltpu.roll(t, HEAD_DIM, 1)
    lo = _lane_lo(t.shape)
    return jnp.where(lo, t, swapped), jnp.where(lo, swapped, t)


def _heads_attention(qps, keys, vals, masks, sinks):
    lo = _lane_lo(qps[0].shape)
    keeps = (lo, jnp.logical_not(lo))
    tiles_per_kv = len(qps) // len(keys)

    def head_scores(h):
        qp = qps[h // 2]
        qh = jnp.where(keeps[h % 2], qp, jnp.zeros_like(qp))
        g = h // 2 // tiles_per_kv
        return [sc if mask is None else jnp.where(mask, sc, -1e30)
                for sc, mask in zip([_dot_nt(qh, k) for k in keys[g]], masks[g])]

    n_heads = 2 * len(qps)
    outs = []
    pending = [head_scores(h) for h in range(min(ATTN_LOOKAHEAD, n_heads))]
    for h in range(n_heads):
        if h + ATTN_LOOKAHEAD < n_heads:
            pending.append(head_scores(h + ATTN_LOOKAHEAD))
        outs.append(_softmax_pv(pending.pop(0), vals[h // 2 // tiles_per_kv], sinks[h]))
    return [jnp.where(lo, outs[2 * i], outs[2 * i + 1]) for i in range(len(qps))]


def _store_dup(dst_ref, t):
    d0, d1 = _dup_heads(t)
    dst_ref[0] = d0.astype(dst_ref.dtype)
    dst_ref[1] = d1.astype(dst_ref.dtype)


def _win_attn_kernel(sink_ref, q_ref, k_ref, v_ref, kc_ref, vc_ref, cosq_ref, sinq_ref, cosk_ref, sink_tab_ref,
                     o_ref, kd_ref, vd_ref, kcd_ref, vcd_ref, *, seq):
    i = pl.program_id(1)
    blk = ATTN_BLOCK
    win = 3 * blk

    @pl.when(i == 0)
    def _():
        _store_dup(kd_ref, _rope(k_ref[0].astype(F32), cosk_ref[...], sink_tab_ref[...]))
        _store_dup(vd_ref, v_ref[0].astype(F32))
        _store_dup(kcd_ref, kc_ref[0].astype(F32))
        _store_dup(vcd_ref, vc_ref[0].astype(F32))

    scale = HEAD_DIM ** -0.5 * LOG2_E
    n_sub = q_ref.shape[1] // blk
    qps, keys, vals, masks = [], [], [], []
    for sub in range(n_sub):
        qb = i * n_sub + sub
        rows = slice(sub * blk, (sub + 1) * blk)
        start = pl.multiple_of(jnp.clip((qb - 1) * blk, 0, seq - win), blk)
        qpos = qb * blk + lax.broadcasted_iota(I32, (blk, win), 0)
        kpos = start + lax.broadcasted_iota(I32, (blk, win), 1)
        mask = jnp.abs(kpos - qpos) <= WINDOW
        cos = cosq_ref[rows, :]
        sin = sinq_ref[rows, :]
        qps += [(_rope(q_ref[0, rows, p * LANES:(p + 1) * LANES].astype(F32), cos, sin) * scale).astype(BF16)
                for p in range(N_HEADS // 2)]
        keys += [[kd_ref[kh, pl.ds(start, win), :], kcd_ref[kh]] for kh in range(N_KV_HEADS)]
        vals += [[vd_ref[kh, pl.ds(start, win), :], vcd_ref[kh]] for kh in range(N_KV_HEADS)]
        masks += [[mask, None]] * N_KV_HEADS
    sinks = [sink_ref[h] * LOG2_E for h in range(N_HEADS)] * n_sub
    for t, o in enumerate(_heads_attention(qps, keys, vals, masks, sinks)):
        sub, p = divmod(t, N_HEADS // 2)
        o_ref[0, sub * blk:(sub + 1) * blk, p * LANES:(p + 1) * LANES] = o.astype(o_ref.dtype)


def _window_attention(p_x, p_c, ctx_kv_cols, sink, tabs):
    b, s, _ = p_x.shape
    l = p_c.shape[1]
    cos, sin = tabs
    blk = min(s, ATTN_BLOCKS_PER_STEP * ATTN_BLOCK)
    assert s % blk == 0
    kcol, vcol = COL_K // KV_W, COL_V // KV_W
    kccol, vccol = ctx_kv_cols[0] // KV_W, ctx_kv_cols[1] // KV_W
    return pl.pallas_call(
        functools.partial(_win_attn_kernel, seq=s),
        grid=(b, s // blk),
        in_specs=[pl.BlockSpec(memory_space=pltpu.SMEM),
                  pl.BlockSpec((1, blk, Q_W), lambda i, j: (i, j, COL_Q // Q_W)),
                  pl.BlockSpec((1, s, KV_W), lambda i, j: (i, 0, kcol)),
                  pl.BlockSpec((1, s, KV_W), lambda i, j: (i, 0, vcol)),
                  pl.BlockSpec((1, l, KV_W), lambda i, j: (i, 0, kccol)),
                  pl.BlockSpec((1, l, KV_W), lambda i, j: (i, 0, vccol)),
                  pl.BlockSpec((blk, LANES), lambda i, j: (j, 0)),
                  pl.BlockSpec((blk, LANES), lambda i, j: (j, 0)),
                  pl.BlockSpec((s, LANES), lambda i, j: (0, 0)),
                  pl.BlockSpec((s, LANES), lambda i, j: (0, 0))],
        out_specs=pl.BlockSpec((1, blk, Q_W), lambda i, j: (i, j, 0)),
        out_shape=jax.ShapeDtypeStruct((b, s, Q_W), BF16),
        scratch_shapes=[pltpu.VMEM((N_KV_HEADS, s, LANES), BF16), pltpu.VMEM((N_KV_HEADS, s, LANES), BF16),
                        pltpu.VMEM((N_KV_HEADS, l, LANES), BF16), pltpu.VMEM((N_KV_HEADS, l, LANES), BF16)],
        name="window_attention",
    )(sink, p_x, p_x, p_x, p_c, p_c, cos, sin, cos, sin)


def _ctx_attn_kernel(sink_ref, q_ref, k_ref, v_ref, o_ref):
    kd = [t.astype(BF16) for t in _dup_heads(k_ref[0].astype(F32))]
    vd = [t.astype(BF16) for t in _dup_heads(v_ref[0].astype(F32))]
    scale = HEAD_DIM ** -0.5 * LOG2_E
    qps = [(q_ref[0, :, p * LANES:(p + 1) * LANES].astype(F32) * scale).astype(BF16) for p in range(N_HEADS // 2)]
    sinks = [sink_ref[h] * LOG2_E for h in range(N_HEADS)]
    outs = _heads_attention(qps, [[k] for k in kd], [[v] for v in vd], [[None]] * N_KV_HEADS, sinks)
    for p, o in enumerate(outs):
        o_ref[0, :, p * LANES:(p + 1) * LANES] = o.astype(o_ref.dtype)


def _context_attention(p_c, sink):
    b, l, _ = p_c.shape
    return pl.pallas_call(
        _ctx_attn_kernel,
        grid=(b,),
        in_specs=[pl.BlockSpec(memory_space=pltpu.SMEM),
                  pl.BlockSpec((1, l, Q_W), lambda i: (i, 0, COL_Q // Q_W)),
                  pl.BlockSpec((1, l, KV_W), lambda i: (i, 0, COL_K // KV_W)),
                  pl.BlockSpec((1, l, KV_W), lambda i: (i, 0, COL_V // KV_W))],
        out_specs=pl.BlockSpec((1, l, Q_W), lambda i: (i, 0, 0)),
        out_shape=jax.ShapeDtypeStruct((b, l, Q_W), BF16),
        name="context_attention",
    )(sink, p_c, p_c, p_c)


def _merge_kernel(ga_ref, gb_ref, gc_ref,
                  u_ref, u_p_ref, u_n_ref, pz_ref, pz_p_ref, pz_n_ref,
                  attn_ref, x_ref, g1_ref, sh2_ref, sc2_ref, n2g_ref,
                  dw_ref, dwb_ref, lng_ref, lnb_ref, wpool_ref, pscale_ref,
                  wa_ref, wb_ref, wc_ref, wo_ref, wrt_ref,
                  xo_ref, h2_ref, afft_ref,
                  uwin_ref, zwin_ref, *, seq):
    t = pl.program_id(1)
    tt = x_ref.shape[1]
    has_prev = (t > 0).astype(F32)
    has_next = (t < pl.num_programs(1) - 1).astype(F32)

    uwin_ref[0:HALO, :] = u_p_ref[0].astype(F32) * has_prev
    uwin_ref[HALO:HALO + tt, :] = u_ref[0].astype(F32)
    uwin_ref[HALO + tt:, :] = u_n_ref[0].astype(F32) * has_next
    first = HALO - CONV_PAD
    rows = tt + 2 * HALO
    acc_cols = []
    for cb in range(CONV_CH // LANES):
        cols = slice(cb * LANES, (cb + 1) * LANES)
        window = uwin_ref[:, cols]
        acc_c = jnp.zeros((tt, LANES), F32) + dwb_ref[:, cols]
        for shift in range(F32_SUBLANES):
            taps = [k for k in range(CONV_K) if (first + k) % F32_SUBLANES == shift]
            if not taps:
                continue
            shifted = window if shift == 0 else pltpu.roll(window, rows - shift, 0)
            for k in taps:
                off = first + k - shift
                acc_c = acc_c + shifted[off:off + tt] * dw_ref[k:k + 1, cols]
        acc_cols.append(acc_c)
    acc = jnp.concatenate(acc_cols, axis=-1)
    mu = jnp.mean(acc, axis=-1, keepdims=True)
    cen = acc - mu
    var = jnp.mean(cen * cen, axis=-1, keepdims=True)
    ln = cen * lax.rsqrt(var + EPS) * lng_ref[...] + lnb_ref[...]
    feat_b = (ln * _sigmoid(ln)).astype(BF16)

    zwin_ref[0:HALO, :] = pz_p_ref[0].astype(F32) * has_prev
    zwin_ref[HALO:HALO + tt, :] = pz_ref[0].astype(F32)
    zwin_ref[HALO + tt:, :] = pz_n_ref[0].astype(F32) * has_next
    tpos = t * tt + lax.broadcasted_iota(I32, (tt, 1), 0)
    pooled = []
    for gi, w in enumerate(POOL_WINDOWS):
        cols = slice(gi * POOL_GROUP, (gi + 1) * POOL_GROUP)
        tot = zwin_ref[pl.ds(HALO - w // 2, tt), cols]
        for d in range(1 - w // 2, w - w // 2):
            tot = tot + zwin_ref[pl.ds(HALO + d, tt), cols]
        cnt = (jnp.minimum(tpos + (w - w // 2), seq) - jnp.maximum(tpos - w // 2, 0)).astype(F32)
        diff = tot / cnt - zwin_ref[pl.ds(HALO, tt), cols]
        pooled.append(_dot(diff.astype(BF16), wpool_ref[gi]))
    feat_c = (jnp.concatenate(pooled, axis=-1) * pscale_ref[...]).astype(BF16)

    y_a = _dot(attn_ref[0], wa_ref[...])
    y_b = _dot(feat_b, wb_ref[...])
    y_c = _dot(feat_c, wc_ref[...])
    merged = ga_ref[0].astype(F32) * y_a + gb_ref[0].astype(F32) * y_b + gc_ref[0].astype(F32) * y_c
    xn = x_ref[0] + g1_ref[0] * _dot(merged.astype(BF16), wo_ref[...])
    xo_ref[0] = xn

    h2 = _rms_mod(xn, n2g_ref[...], sh2_ref[0], sc2_ref[0])
    h2_hi, h2_lo = _hi_lo(h2)
    h2_ref[0] = h2_hi
    ne = afft_ref.shape[1]
    by_hi = _dot_nt(wrt_ref[...], h2_hi)
    logits_t = by_hi[:ne] + by_hi[ne:] + _dot_nt(wrt_ref[:ne, :], h2_lo)
    et = jnp.exp(logits_t - jnp.max(logits_t, axis=0, keepdims=True))
    afft_ref[0] = et / jnp.sum(et, axis=0, keepdims=True)


def _merge(p, attn, x, g1, sh2, sc2, n2g, lw, tt):
    b, s, d = x.shape
    nh = tt // HALO
    last_h = s // HALO - 1
    e = N_EXPERTS

    def main(width, col):
        return pl.BlockSpec((1, tt, width), lambda i, j: (i, j, col))

    def prev(col):
        return pl.BlockSpec((1, HALO, CONV_CH), lambda i, j: (i, jnp.maximum(j * nh - 1, 0), col))

    def nxt(col):
        return pl.BlockSpec((1, HALO, CONV_CH), lambda i, j: (i, jnp.minimum((j + 1) * nh, last_h), col))

    def per_batch():
        return pl.BlockSpec((1, 1, d), lambda i, j: (i, 0, 0))

    def const(shape):
        return pl.BlockSpec(shape, lambda i, j: (0,) * len(shape))

    cu, pz = COL_CONV_U // CONV_CH, COL_POOL // CONV_CH
    in_specs = [main(d, 0), main(d, 1), main(d, 2),
                main(CONV_CH, cu), prev(cu), nxt(cu), main(POOL_CH, pz), prev(pz), nxt(pz),
                pl.BlockSpec((1, tt, Q_W), lambda i, j: (i, j, 0)),
                pl.BlockSpec((1, tt, d), lambda i, j: (i, j, 0)),
                per_batch(), per_batch(), per_batch(), const((1, d)),
                const((CONV_K, CONV_CH)), const((1, CONV_CH)), const((1, CONV_CH)), const((1, CONV_CH)),
                const((len(POOL_WINDOWS), POOL_GROUP, POOL_GROUP)), const((1, POOL_CH)),
                const((Q_W, d)), const((CONV_CH, d)), const((POOL_CH, d)), const((d, d)),
                const((2 * e, d))]
    out_specs = [pl.BlockSpec((1, tt, d), lambda i, j: (i, j, 0)),
                 pl.BlockSpec((1, tt, d), lambda i, j: (i, j, 0)),
                 pl.BlockSpec((1, e, tt), lambda i, j: (i, 0, j))]
    out_shape = [jax.ShapeDtypeStruct((b, s, d), F32), jax.ShapeDtypeStruct((b, s, d), BF16),
                 jax.ShapeDtypeStruct((b, e, s), F32)]
    return pl.pallas_call(
        functools.partial(_merge_kernel, seq=s),
        grid=(b, s // tt),
        in_specs=in_specs, out_specs=out_specs, out_shape=out_shape,
        scratch_shapes=[pltpu.VMEM((tt + 2 * HALO, CONV_CH), F32), pltpu.VMEM((tt + 2 * HALO, POOL_CH), F32)],
        name="mix_merge",
    )(p, p, p, p, p, p, p, p, p, attn, x, g1, sh2, sc2, n2g,
      lw['conv_dw'], lw['conv_dw_b'], lw['conv_ln_g'], lw['conv_ln_b'], lw['w_pool'], lw['pool_scale'],
      lw['w_attn_o'], lw['w_conv_o'], lw['w_pool_o'], lw['w_out'], lw['w_router_t'])


def _topk_kernel(afft_ref, slot_ref, slott_ref, offs_ref, *, cap, blk):
    a = afft_ref[0]
    e, s = a.shape

    def keeps_cap(cand):
        return jnp.sum((a >= cand).astype(F32), axis=-1, keepdims=True) >= cap

    def largest_kept(thr, cands):
        for cand in cands:
            thr = jnp.where(keeps_cap(cand), cand, thr)
        return thr

    tiny = jnp.full((e, 1), F32_TINY, F32)
    thr = jnp.where(keeps_cap(tiny), tiny, 0.0)
    for hi, lo in ((64, 32), (16, 8), (4, 2)):
        thr = largest_kept(thr, [thr * float(2 ** lo), thr * float(2 ** hi), thr * float(2 ** (hi + lo))])
    thr = largest_kept(thr, [thr * 2.0])
    delta = thr
    for _ in range(F32_MANTISSA_BITS // 2):
        d_hi, d_lo = delta * 0.5, delta * 0.25
        thr = largest_kept(thr, [thr + d_lo, thr + d_hi, thr + (d_hi + d_lo)])
        delta = d_lo
    if F32_MANTISSA_BITS % 2:
        thr = largest_kept(thr, [thr + delta * 0.5])
    gt = a > thr
    eq = a == thr
    need = cap - jnp.sum(gt.astype(F32), axis=-1, keepdims=True)

    r = lax.broadcasted_iota(I32, (blk, blk), 0)
    c = lax.broadcasted_iota(I32, (blk, blk), 1)
    upper = (r < c).astype(BF16)
    eye = (r == c).astype(F32)

    def prefix(mask_f32):
        carry = jnp.zeros((e, 1), F32)
        parts = []
        for j in range(s // blk):
            m = mask_f32[:, j * blk:(j + 1) * blk]
            parts.append(_dot(m.astype(BF16), upper) + carry)
            carry = carry + jnp.sum(m, axis=-1, keepdims=True)
        return jnp.concatenate(parts, axis=-1)

    sel = gt | (eq & (prefix(eq.astype(F32)) < need))
    pos = prefix(sel.astype(F32))
    slot = jnp.where(sel, pos, -1.0)
    slot_ref[0] = slot.astype(I32)
    for j in range(s // blk):
        slott_ref[0, j * blk:(j + 1) * blk, :] = _dot_nt(eye, slot[:, j * blk:(j + 1) * blk],
                                                        precision=HIGHEST).astype(I32)
    tok = lax.broadcasted_iota(I32, (s, LANES), 0)
    col = lax.broadcasted_iota(I32, (s, LANES), 1)
    before = (tok < col * blk).astype(BF16)
    offs_ref[0] = _dot(sel.astype(BF16), before).astype(I32)


def _topk(afft, cap):
    b, e, s = afft.shape
    blk = min(s, TOKEN_CHUNK)
    slot, slott, offs = pl.pallas_call(
        functools.partial(_topk_kernel, cap=cap, blk=blk),
        grid=(b,),
        in_specs=[pl.BlockSpec((1, e, s), lambda i: (i, 0, 0))],
        out_specs=[pl.BlockSpec((1, e, s), lambda i: (i, 0, 0)),
                   pl.BlockSpec((1, s, e), lambda i: (i, 0, 0)),
                   pl.BlockSpec((1, e, LANES), lambda i: (i, 0, 0))],
        out_shape=[jax.ShapeDtypeStruct((b, e, s), I32), jax.ShapeDtypeStruct((b, s, e), I32),
                   jax.ShapeDtypeStruct((b, e, LANES), I32)],
        name="expert_choice_topk",
    )(afft)
    return slot, slott, offs[:, :, :s // blk + 1]


def _slot_windows(offs_ref, idx, win):
    lo = offs_ref[idx]
    hi = offs_ref[idx + 1]
    first = lo // win
    return first, jnp.where(hi > lo, (hi - 1) // win - first + 1, 0)


def _gather_rows(onehot, h, gate_row):
    picked = _dot(onehot.astype(BF16), h)
    gates = jnp.sum(jnp.where(onehot, gate_row, 0.0), axis=-1, keepdims=True)
    return picked, gates


def _ffn_kernel(offs_ref, slot_ref, afft_ref, h_ref, slotc_ref, cafft_ref, hc_ref, wg_ref, wu_ref, wd_ref,
                ye_ref, yec_ref, wg_s, wu_s, wd_s, xe_s, g_s, *, chunk, group, win, rows):
    ex = pl.program_id(0)
    bi = pl.program_id(1)
    nb = pl.num_programs(1)
    cap = xe_s.shape[0]

    @pl.when(bi == 0)
    def _():
        def cast(i, carry):
            sl = pl.ds(pl.multiple_of(i * rows, rows), rows)
            wg_s[sl, :] = wg_ref[0, 0, sl, :].astype(BF16)
            wu_s[sl, :] = wu_ref[0, 0, sl, :].astype(BF16)
            wd_s[sl, :] = wd_ref[0, 0, sl, :].astype(BF16)
            return carry
        lax.fori_loop(0, wg_s.shape[0] // rows, cast, 0)

    def ffn(xe):
        a = _dot(xe, wg_s[...])
        u = _dot(xe, wu_s[...])
        hid = (a * _sigmoid(a) * u).astype(BF16)
        return _dot(hid, wd_s[...])

    nch = h_ref.shape[1] // chunk
    base = (bi * pl.num_programs(0) + ex) * (nch + 1)
    xe_s[...] = jnp.zeros_like(xe_s)
    g_s[...] = jnp.zeros_like(g_s)
    span = group * chunk
    starts = []
    fits = None
    for p in range(nch // group):
        lo = offs_ref[base + p * group]
        hi = offs_ref[base + (p + 1) * group]
        a = pl.multiple_of(jnp.minimum((lo // F32_SUBLANES) * F32_SUBLANES, cap - win), F32_SUBLANES)
        starts.append(a)
        fits = (hi - a <= win) if fits is None else jnp.logical_and(fits, hi - a <= win)

    def add_window(a, sl, width):
        onehot = (slot_ref[0, pl.ds(ex, 1), sl] - a) == lax.broadcasted_iota(I32, (win, width), 0)
        picked, gates = _gather_rows(onehot, h_ref[0, sl, :], afft_ref[0, pl.ds(ex, 1), sl])
        xe_s[pl.ds(a, win), :] += picked
        g_s[pl.ds(a, win), :] += gates

    @pl.when(fits)
    def _():
        for p, a in enumerate(starts):
            add_window(a, slice(p * span, (p + 1) * span), span)

    @pl.when(jnp.logical_not(fits))
    def _():
        def per_chunk(j, carry):
            first, nwin = _slot_windows(offs_ref, base + j, win)
            sl = pl.ds(pl.multiple_of(j * chunk, chunk), chunk)

            def window(w, c):
                add_window(pl.multiple_of((first + w) * win, win), sl, chunk)
                return c
            lax.fori_loop(0, nwin, window, 0)
            return carry
        lax.fori_loop(0, nch, per_chunk, 0)

    if yec_ref is None:
        ye_ref[0] = (ffn(xe_s[...].astype(BF16)) * g_s[...]).astype(ye_ref.dtype)
        return

    @pl.when(bi < nb - 1)
    def _():
        ye_ref[0] = (ffn(xe_s[...].astype(BF16)) * g_s[...]).astype(ye_ref.dtype)

    @pl.when(bi == nb - 1)
    def _():
        rowc = lax.broadcasted_iota(I32, (yec_ref.shape[1], hc_ref.shape[0]), 0)
        picked, gates = _gather_rows(slotc_ref[0] == rowc, hc_ref[...], cafft_ref[0])
        xe_all = jnp.concatenate([xe_s[...].astype(BF16), picked.astype(BF16)], axis=0)
        y_all = ffn(xe_all) * jnp.concatenate([g_s[...], gates], axis=0)
        ye_ref[0] = y_all[:cap].astype(ye_ref.dtype)
        yec_ref[0] = y_all[cap:].astype(yec_ref.dtype)


def _expert_ffn(layer, slot, offs, afft, h2, wg, wu, wd, cap, ctx_part=None):
    b, e, s = slot.shape
    d = h2.shape[2]
    f = wg.shape[3]
    assert f == d
    chunk = min(s, TOKEN_CHUNK)
    nch = s // chunk
    group = 2 if nch % 2 == 0 else 1
    win = min(cap, GATHER_WINDOW)
    assert cap % win == 0

    def w_spec(from_step):
        if from_step < 1:
            return pl.BlockSpec((1, 1, d, f), lambda j, i, o: (layer, j, 0, 0))
        return pl.BlockSpec((1, 1, d, f),
                            lambda j, i, o: (layer, jnp.minimum(j + jnp.where(i >= from_step, 1, 0), e - 1), 0, 0))

    row_spec = pl.BlockSpec((1, e, s), lambda j, i, o: (i, 0, 0))
    in_specs = [row_spec, row_spec, pl.BlockSpec((1, s, d), lambda j, i, o: (i, 0, 0))]
    out_specs = [pl.BlockSpec((1, cap, d), lambda j, i, o: (i * e + j, 0, 0))]
    out_shape = [jax.ShapeDtypeStruct((b * e, cap, d), BF16)]
    args = [slot, afft, h2]
    body = functools.partial(_ffn_kernel, chunk=chunk, group=group, win=win, rows=128)
    if ctx_part is None:
        def kern(offs_ref, slot_ref, afft_ref, h_ref, wg_ref, wu_ref, wd_ref, ye_ref, *scratch):
            body(offs_ref, slot_ref, afft_ref, h_ref, None, None, None, wg_ref, wu_ref, wd_ref, ye_ref, None,
                 *scratch)
    else:
        slot_c, afft_c, h_c, rows_c = ctx_part
        n_c = h_c.shape[0]
        rowc_spec = pl.BlockSpec((1, 1, n_c), lambda j, i, o: (j, 0, 0))
        in_specs += [rowc_spec, rowc_spec, pl.BlockSpec((n_c, d), lambda j, i, o: (0, 0))]
        out_specs.append(pl.BlockSpec((1, rows_c, d), lambda j, i, o: (j, 0, 0)))
        out_shape.append(jax.ShapeDtypeStruct((e, rows_c, d), BF16))
        args += [slot_c, afft_c, h_c]
        kern = body
    outs = pl.pallas_call(
        kern,
        grid_spec=pltpu.PrefetchScalarGridSpec(
            num_scalar_prefetch=1, grid=(e, b),
            in_specs=in_specs + [w_spec(min(k, b - 1)) for k in (1, 2, 3)], out_specs=out_specs,
            scratch_shapes=[pltpu.VMEM((d, f), BF16), pltpu.VMEM((d, f), BF16), pltpu.VMEM((f, d), BF16),
                            pltpu.VMEM((cap, d), F32), pltpu.VMEM((cap, 1), F32)]),
        out_shape=out_shape,
        name="expert_ffn",
    )(offs.reshape(-1), *args, wg, wu, wd)
    ye = outs[0].reshape(b, e * cap, d)
    return ye if ctx_part is None else (ye, outs[1])


def _combine_kernel(offs_ref, slott_ref, ye_ref, x_ref, g2_ref, fg_ref, o_ref, acc_s, *, cap, win, final_norm):
    bi = pl.program_id(0)
    tt = acc_s.shape[0]
    n_sub = x_ref.shape[1] // tt
    nch = pl.num_programs(1) * n_sub
    per_block = LANES // win

    for sub in range(n_sub):
        j = pl.program_id(1) * n_sub + sub
        tok = slice(sub * tt, (sub + 1) * tt)

        starts = []
        fits = None
        for ex in range(N_EXPERTS):
            idx = (bi * N_EXPERTS + ex) * (nch + 1) + j
            a = pl.multiple_of(jnp.minimum((offs_ref[idx] // BF16_SUBLANES) * BF16_SUBLANES, cap - win),
                               BF16_SUBLANES)
            starts.append(a)
            ok = offs_ref[idx + 1] - a <= win
            fits = ok if fits is None else jnp.logical_and(fits, ok)

        @pl.when(fits)
        def _(starts=starts, tok=tok):
            lane = lax.broadcasted_iota(I32, (tt, LANES), 1)
            blocks = []
            for blk in range(N_EXPERTS // per_block):
                target = None
                for q in range(per_block):
                    ex = blk * per_block + q
                    t = slott_ref[0, tok, ex:ex + 1] - starts[ex] + q * win
                    target = t if target is None else jnp.where(lane >= q * win, t, target)
                blocks.append((target == lane).astype(BF16))
            rows = [ye_ref[0, pl.ds(ex * cap + starts[ex], win), :] for ex in range(N_EXPERTS)]
            acc_s[...] = _dot(jnp.concatenate(blocks, axis=1), jnp.concatenate(rows, axis=0))

        @pl.when(jnp.logical_not(fits))
        def _(j=j, tok=tok):
            lane = lax.broadcasted_iota(I32, (tt, win), 1)
            acc_s[...] = jnp.zeros_like(acc_s)
            for ex in range(N_EXPERTS):
                first, nwin = _slot_windows(offs_ref, (bi * N_EXPERTS + ex) * (nch + 1) + j, win)

                def window(w, c, ex=ex, first=first):
                    a = pl.multiple_of((first + w) * win, win)
                    onehot = ((slott_ref[0, tok, ex:ex + 1] - a) == lane).astype(BF16)
                    acc_s[...] += _dot(onehot, ye_ref[0, pl.ds(ex * cap + a, win), :])
                    return c
                lax.fori_loop(0, nwin, window, 0)

        out = x_ref[0, tok, :] + g2_ref[0] * acc_s[...]
        if final_norm:
            out = out * lax.rsqrt(jnp.mean(out * out, axis=-1, keepdims=True) + EPS) * fg_ref[...]
        o_ref[0, tok, :] = out


def _combine(slott, offs, ye, x, g2, fg, cap, final_norm):
    b, s, d = x.shape
    e = N_EXPERTS
    chunk = min(s, TOKEN_CHUNK)
    tt = min(s, COMBINE_CHUNKS * chunk)
    win = min(cap, SCATTER_WINDOW)
    assert cap % win == 0 and LANES % win == 0 and e % (LANES // win) == 0 and s % tt == 0
    return pl.pallas_call(
        functools.partial(_combine_kernel, cap=cap, win=win, final_norm=final_norm),
        grid_spec=pltpu.PrefetchScalarGridSpec(
            num_scalar_prefetch=1, grid=(b, s // tt),
            in_specs=[pl.BlockSpec((1, tt, e), lambda i, j, o: (i, j, 0)),
                      pl.BlockSpec((1, e * cap, d), lambda i, j, o: (i, 0, 0)),
                      pl.BlockSpec((1, tt, d), lambda i, j, o: (i, j, 0)),
                      pl.BlockSpec((1, 1, d), lambda i, j, o: (i, 0, 0)),
                      pl.BlockSpec((1, d), lambda i, j, o: (0, 0))],
            out_specs=pl.BlockSpec((1, tt, d), lambda i, j, o: (i, j, 0)),
            scratch_shapes=[pltpu.VMEM((chunk, d), F32)]),
        out_shape=jax.ShapeDtypeStruct((b, s, d), F32),
        name="moe_combine",
    )(offs.reshape(-1), slott, ye, x, g2, fg)


def _split_hi_lo(w):
    return jnp.concatenate(_hi_lo(w), axis=0)


def _permute_in_cols(w):
    o_k = Q_W
    o_v = o_k + KV_W
    o_ca = o_v + KV_W
    o_cg = o_ca + CONV_CH
    o_p = o_cg + CONV_CH
    o_g = o_p + POOL_CH
    return jnp.concatenate([w[:, o_g:], w[:, :o_k], w[:, o_ca:o_cg], w[:, o_cg:o_p], w[:, o_p:o_g],
                            w[:, o_k:o_v], w[:, o_v:o_ca]], axis=1)


def kernel(x, c, ctx, c_ctx, norm1_g, norm2_g, w_mod, b_mod, w_in, attn_sink, w_attn_o, conv_dw, conv_dw_b,
           conv_ln_g, conv_ln_b, w_conv_o, w_pool, pool_scale, w_pool_o, w_out, w_router, w_e_gate, w_e_up,
           w_e_down, final_norm_g):
    b, s, d = x.shape
    l = ctx.shape[1]
    depth = w_in.shape[0]
    assert d == _D_MODEL and w_in.shape[2] == IN_W and CONV_CH == POOL_CH

    tabs = _rope_tables(s)
    cc = jnp.zeros((8, d), F32).at[:b].set(c).at[b].set(c_ctx)
    mod = _modulation(cc, w_mod, b_mod)
    fg = final_norm_g.reshape(1, d)

    for layer in range(depth):
        last = layer == depth - 1
        mx = mod[layer, :b].reshape(b, 1, 6, d)
        sh1, sc1, g1, sh2, sc2, g2 = [mx[:, :, i] for i in range(6)]
        mc = jnp.broadcast_to(mod[layer, b].reshape(1, 1, 6, d), (b, 1, 6, d))
        csh1, csc1, cg1, csh2, csc2, cg2 = [mc[:, :, i] for i in range(6)]
        n1g = norm1_g[layer].reshape(1, d)
        n2g = norm2_g[layer].reshape(1, d)
        w_in_l = _permute_in_cols(w_in[layer]).astype(BF16)
        lw = {'conv_dw': conv_dw[layer], 'conv_dw_b': conv_dw_b[layer].reshape(1, -1),
              'conv_ln_g': conv_ln_g[layer].reshape(1, -1), 'conv_ln_b': conv_ln_b[layer].reshape(1, -1),
              'w_pool': w_pool[layer].astype(BF16), 'pool_scale': pool_scale[layer].reshape(1, -1),
              'w_attn_o': w_attn_o[layer].astype(BF16), 'w_conv_o': w_conv_o[layer].astype(BF16),
              'w_pool_o': w_pool_o[layer].astype(BF16), 'w_out': w_out[layer].astype(BF16),
              'w_router_t': _split_hi_lo(w_router[layer].T)}

        p_x = _inproj(x, n1g, sh1, sc1, w_in_l, tm=min(s, INPROJ_TILE))
        if last:
            p_c = _inproj(ctx, n1g, csh1, csc1, w_in_l[:, WCOL_K:], tm=l, mixer_epilogue=False)
            ctx_kv_cols = (0, KV_W)
        else:
            p_c = _inproj(ctx, n1g, csh1, csc1, w_in_l, tm=l)
            ctx_kv_cols = (COL_K, COL_V)
        attn_x = _window_attention(p_x, p_c, ctx_kv_cols, attn_sink[layer], tabs)
        x_mid, h2, afft = _merge(p_x, attn_x, x, g1, sh2, sc2, n2g, lw, tt=min(s, MERGE_TILE))
        if not last:
            attn_c = _context_attention(p_c, attn_sink[layer])
            c_mid, ch2, cafft = _merge(p_c, attn_c, ctx, cg1, csh2, csc2, n2g, lw, tt=l)
        cap = (CAPACITY_FACTOR * s) // N_EXPERTS
        slot, slott, offs = _topk(afft, cap)
        if last:
            ye = _expert_ffn(layer, slot, offs, afft, h2, w_e_gate, w_e_up, w_e_down, cap)
        else:
            cap_c = (CAPACITY_FACTOR * l) // N_EXPERTS
            cslot, cslott, coffs = _topk(cafft, cap_c)
            sample_base = (jnp.arange(b, dtype=I32) * cap_c)[:, None, None]
            cslot_all = jnp.where(cslot >= 0, cslot + sample_base, -1)
            cslot_all = cslot_all.transpose(1, 0, 2).reshape(N_EXPERTS, 1, b * l)
            cafft_all = cafft.transpose(1, 0, 2).reshape(N_EXPERTS, 1, b * l)
            ye, yec = _expert_ffn(layer, slot, offs, afft, h2, w_e_gate, w_e_up, w_e_down, cap,
                                  ctx_part=(cslot_all, cafft_all, ch2.reshape(b * l, d), b * cap_c))
            yec = yec.reshape(N_EXPERTS, b, cap_c, d).transpose(1, 0, 2, 3).reshape(b, N_EXPERTS * cap_c, d)
            ctx = _combine(cslott, coffs, yec, c_mid, cg2, fg, cap_c, False)
        x = _combine(slott, offs, ye, x_mid, g2, fg, cap, last)
    return x
```

```python
import functools

import jax
import jax.numpy as jnp
import numpy as np
from jax import lax
from jax.experimental import pallas as pl
from jax.experimental.pallas import tpu as pltpu

F32 = jnp.float32
BF16 = jnp.bfloat16
I32 = jnp.int32

EPS = 1e-6
GRID_W = 64
N_HEADS = 8
N_KV_HEADS = 2
HEAD_DIM = 64
GQA_GROUP = N_HEADS // N_KV_HEADS
WINDOW = 128
ATTN_BLOCK = 128
ROPE_BASE = 10000.0
ROPE_PAIRS = HEAD_DIM // 4
CONV_CH = 512
CONV_K = 31
CONV_PAD = CONV_K // 2
POOL_WINDOWS = (2, 4, 8, 16)
POOL_GROUP = 128
POOL_CH = POOL_GROUP * len(POOL_WINDOWS)
N_EXPERTS = 16
CAPACITY_FACTOR = 2
Q_W = N_HEADS * HEAD_DIM
KV_W = N_KV_HEADS * HEAD_DIM

LANES = 128
HALO = 16
HIGHEST = lax.Precision.HIGHEST
LOG2_E = 1.4426950408889634
F32_TINY = 2.0 ** -126
F32_MANTISSA_BITS = 23
TOKEN_CHUNK = 256
GATHER_WINDOW = 128
SCATTER_WINDOW = 64
F32_SUBLANES = 8
BF16_SUBLANES = 16
ATTN_LOOKAHEAD = 3
ATTN_BLOCKS_PER_STEP = 8
COMBINE_CHUNKS = 4
INPROJ_TILE = 1024
MERGE_TILE = 512

_D_MODEL = 1024
N_GATE_COLS = 3 * _D_MODEL
WCOL_Q = N_GATE_COLS
WCOL_CONV_A = WCOL_Q + Q_W
WCOL_CONV_G = WCOL_CONV_A + CONV_CH
WCOL_POOL = WCOL_CONV_G + CONV_CH
WCOL_K = WCOL_POOL + POOL_CH
IN_W = WCOL_K + 2 * KV_W
COL_Q = N_GATE_COLS
COL_CONV_U = COL_Q + Q_W
COL_POOL = COL_CONV_U + CONV_CH
COL_K = COL_POOL + POOL_CH
COL_V = COL_K + KV_W
OUT_W = COL_V + KV_W
GATE_CHUNK = 768


def _dot(a, b):
    return jnp.dot(a, b, preferred_element_type=F32)


def _dot_nt(a, b, precision=None):
    return lax.dot_general(a, b, (((1,), (1,)), ((), ())), preferred_element_type=F32, precision=precision)


def _sigmoid(v):
    return 0.5 * jnp.tanh(0.5 * v) + 0.5


def _rms_mod(x, g, sh, sc):
    y = x * lax.rsqrt(jnp.mean(x * x, axis=-1, keepdims=True) + EPS) * g
    return y * (1.0 + sc) + sh


def _hi_lo(v):
    hi = v.astype(BF16)
    return hi, (v - hi.astype(F32)).astype(BF16)


def _mod_kernel(c_ref, w_ref, b_ref, o_ref):
    c = c_ref[...]
    a_hi, a_lo = _hi_lo(c * _sigmoid(c))
    w_hi, w_lo = _hi_lo(w_ref[0])
    o_ref[0] = _dot(a_hi, w_hi) + _dot(a_lo, w_hi) + _dot(a_hi, w_lo) + b_ref[0]


def _modulation(cc, w_mod, b_mod, tn=1536):
    depth, d, n = w_mod.shape
    rows = cc.shape[0]
    return pl.pallas_call(
        _mod_kernel,
        grid=(depth, n // tn),
        in_specs=[pl.BlockSpec((rows, d), lambda l, j: (0, 0)),
                  pl.BlockSpec((1, d, tn), lambda l, j: (l, 0, j)),
                  pl.BlockSpec((1, 1, tn), lambda l, j: (l, 0, j))],
        out_specs=pl.BlockSpec((1, rows, tn), lambda l, j: (l, 0, j)),
        out_shape=jax.ShapeDtypeStruct((depth, rows, n), F32),
        name="modulation",
    )(cc, w_mod, b_mod.reshape(depth, 1, n))


def _inproj_kernel(x_ref, g_ref, sh_ref, sc_ref, w_ref, o_ref, *, mixer_epilogue):
    h = _rms_mod(x_ref[0], g_ref[...], sh_ref[0], sc_ref[0]).astype(BF16)
    if not mixer_epilogue:
        o_ref[0] = _dot(h, w_ref[...]).astype(o_ref.dtype)
        return
    for c0 in range(0, N_GATE_COLS, GATE_CHUNK):
        cols = slice(c0, c0 + GATE_CHUNK)
        o_ref[0, :, cols] = _sigmoid(_dot(h, w_ref[:, cols])).astype(o_ref.dtype)
    o_ref[0, :, COL_Q:COL_CONV_U] = _dot(h, w_ref[:, WCOL_Q:WCOL_CONV_A]).astype(o_ref.dtype)
    glu = _dot(h, w_ref[:, WCOL_CONV_A:WCOL_POOL])
    o_ref[0, :, COL_CONV_U:COL_POOL] = (glu[:, :CONV_CH] * _sigmoid(glu[:, CONV_CH:])).astype(o_ref.dtype)
    o_ref[0, :, COL_POOL:] = _dot(h, w_ref[:, WCOL_POOL:]).astype(o_ref.dtype)


def _inproj(x, g, sh, sc, w, tm, mixer_epilogue=True):
    b, s, d = x.shape
    n = w.shape[1]
    n_out = OUT_W if mixer_epilogue else n
    assert not mixer_epilogue or n == IN_W
    return pl.pallas_call(
        functools.partial(_inproj_kernel, mixer_epilogue=mixer_epilogue),
        grid=(b, s // tm),
        in_specs=[pl.BlockSpec((1, tm, d), lambda i, j: (i, j, 0)),
                  pl.BlockSpec((1, d), lambda i, j: (0, 0)),
                  pl.BlockSpec((1, 1, d), lambda i, j: (i, 0, 0)),
                  pl.BlockSpec((1, 1, d), lambda i, j: (i, 0, 0)),
                  pl.BlockSpec((d, n), lambda i, j: (0, 0))],
        out_specs=pl.BlockSpec((1, tm, n_out), lambda i, j: (i, j, 0)),
        out_shape=jax.ShapeDtypeStruct((b, s, n_out), BF16),
        name="inproj",
    )(x, g, sh, sc, w)


def _rope_tables(s):
    t = np.arange(s)
    row = (t // GRID_W).astype(np.float32)
    col = (t % GRID_W).astype(np.float32)
    freqs = jnp.asarray(ROPE_BASE, F32) ** (-jnp.arange(ROPE_PAIRS, dtype=F32) / ROPE_PAIRS)
    ang_r = jnp.asarray(row)[:, None] * freqs
    ang_c = jnp.asarray(col)[:, None] * freqs
    cos_h = jnp.concatenate([jnp.cos(ang_r), jnp.cos(ang_r), jnp.cos(ang_c), jnp.cos(ang_c)], axis=-1)
    sin_h = jnp.concatenate([-jnp.sin(ang_r), jnp.sin(ang_r), -jnp.sin(ang_c), jnp.sin(ang_c)], axis=-1)
    return jnp.tile(cos_h, (1, LANES // HEAD_DIM)), jnp.tile(sin_h, (1, LANES // HEAD_DIM))


def _rope(x, cos, sin_signed):
    lane = lax.broadcasted_iota(I32, x.shape, 1)
    low = (lane & (2 * ROPE_PAIRS - 1)) < ROPE_PAIRS
    partner = jnp.where(low, pltpu.roll(x, LANES - ROPE_PAIRS, 1), pltpu.roll(x, ROPE_PAIRS, 1))
    return x * cos + partner * sin_signed


def _softmax_pv(s_list, v_list, sink):
    m = sink
    for s in s_list:
        m = jnp.maximum(m, jnp.max(s, axis=-1, keepdims=True))
    denom = jnp.exp2(sink - m)
    o = None
    for s, v in zip(s_list, v_list):
        e = jnp.exp2(s - m)
        denom = denom + jnp.sum(e, axis=-1, keepdims=True)
        pv = _dot(e.astype(BF16), v)
        o = pv if o is None else o + pv
    return o / denom


def _lane_lo(shape):
    return lax.broadcasted_iota(I32, shape, 1) < HEAD_DIM


def _dup_heads(t):
    swapped = pltpu.roll(t, HEAD_DIM, 1)
    lo = _lane_lo(t.shape)
    return jnp.where(lo, t, swapped), jnp.where(lo, swapped, t)


def _heads_attention(qps, keys, vals, masks, sinks):
    lo = _lane_lo(qps[0].shape)
    keeps = (lo, jnp.logical_not(lo))
    tiles_per_kv = len(qps) // len(keys)

    def head_scores(h):
        qp = qps[h // 2]
        qh = jnp.where(keeps[h % 2], qp, jnp.zeros_like(qp))
        g = h // 2 // tiles_per_kv
        return [sc if mask is None else jnp.where(mask, sc, -1e30)
                for sc, mask in zip([_dot_nt(qh, k) for k in keys[g]], masks[g])]

    n_heads = 2 * len(qps)
    outs = []
    pending = [head_scores(h) for h in range(min(ATTN_LOOKAHEAD, n_heads))]
    for h in range(n_heads):
        if h + ATTN_LOOKAHEAD < n_heads:
            pending.append(head_scores(h + ATTN_LOOKAHEAD))
        outs.append(_softmax_pv(pending.pop(0), vals[h // 2 // tiles_per_kv], sinks[h]))
    return [jnp.where(lo, outs[2 * i], outs[2 * i + 1]) for i in range(len(qps))]


def _store_dup(dst_ref, t):
    d0, d1 = _dup_heads(t)
    dst_ref[0] = d0.astype(dst_ref.dtype)
    dst_ref[1] = d1.astype(dst_ref.dtype)


def _win_attn_kernel(sink_ref, q_ref, k_ref, v_ref, kc_ref, vc_ref, cosq_ref, sinq_ref, cosk_ref, sink_tab_ref,
                     o_ref, kd_ref, vd_ref, kcd_ref, vcd_ref, *, seq):
    i = pl.program_id(1)
    blk = ATTN_BLOCK
    win = 3 * blk

    @pl.when(i == 0)
    def _():
        _store_dup(kd_ref, _rope(k_ref[0].astype(F32), cosk_ref[...], sink_tab_ref[...]))
        _store_dup(vd_ref, v_ref[0].astype(F32))
        _store_dup(kcd_ref, kc_ref[0].astype(F32))
        _store_dup(vcd_ref, vc_ref[0].astype(F32))

    scale = HEAD_DIM ** -0.5 * LOG2_E
    n_sub = q_ref.shape[1] // blk
    qps, keys, vals, masks = [], [], [], []
    for sub in range(n_sub):
        qb = i * n_sub + sub
        rows = slice(sub * blk, (sub + 1) * blk)
        start = pl.multiple_of(jnp.clip((qb - 1) * blk, 0, seq - win), blk)
        qpos = qb * blk + lax.broadcasted_iota(I32, (blk, win), 0)
        kpos = start + lax.broadcasted_iota(I32, (blk, win), 1)
        mask = jnp.abs(kpos - qpos) <= WINDOW
        cos = cosq_ref[rows, :]
        sin = sinq_ref[rows, :]
        qps += [(_rope(q_ref[0, rows, p * LANES:(p + 1) * LANES].astype(F32), cos, sin) * scale).astype(BF16)
                for p in range(N_HEADS // 2)]
        keys += [[kd_ref[kh, pl.ds(start, win), :], kcd_ref[kh]] for kh in range(N_KV_HEADS)]
        vals += [[vd_ref[kh, pl.ds(start, win), :], vcd_ref[kh]] for kh in range(N_KV_HEADS)]
        masks += [[mask, None]] * N_KV_HEADS
    sinks = [sink_ref[h] * LOG2_E for h in range(N_HEADS)] * n_sub
    for t, o in enumerate(_heads_attention(qps, keys, vals, masks, sinks)):
        sub, p = divmod(t, N_HEADS // 2)
        o_ref[0, sub * blk:(sub + 1) * blk, p * LANES:(p + 1) * LANES] = o.astype(o_ref.dtype)


def _window_attention(p_x, p_c, ctx_kv_cols, sink, tabs):
    b, s, _ = p_x.shape
    l = p_c.shape[1]
    cos, sin = tabs
    blk = min(s, ATTN_BLOCKS_PER_STEP * ATTN_BLOCK)
    assert s % blk == 0
    kcol, vcol = COL_K // KV_W, COL_V // KV_W
    kccol, vccol = ctx_kv_cols[0] // KV_W, ctx_kv_cols[1] // KV_W
    return pl.pallas_call(
        functools.partial(_win_attn_kernel, seq=s),
        grid=(b, s // blk),
        in_specs=[pl.BlockSpec(memory_space=pltpu.SMEM),
                  pl.BlockSpec((1, blk, Q_W), lambda i, j: (i, j, COL_Q // Q_W)),
                  pl.BlockSpec((1, s, KV_W), lambda i, j: (i, 0, kcol)),
                  pl.BlockSpec((1, s, KV_W), lambda i, j: (i, 0, vcol)),
                  pl.BlockSpec((1, l, KV_W), lambda i, j: (i, 0, kccol)),
                  pl.BlockSpec((1, l, KV_W), lambda i, j: (i, 0, vccol)),
                  pl.BlockSpec((blk, LANES), lambda i, j: (j, 0)),
                  pl.BlockSpec((blk, LANES), lambda i, j: (j, 0)),
                  pl.BlockSpec((s, LANES), lambda i, j: (0, 0)),
                  pl.BlockSpec((s, LANES), lambda i, j: (0, 0))],
        out_specs=pl.BlockSpec((1, blk, Q_W), lambda i, j: (i, j, 0)),
        out_shape=jax.ShapeDtypeStruct((b, s, Q_W), BF16),
        scratch_shapes=[pltpu.VMEM((N_KV_HEADS, s, LANES), BF16), pltpu.VMEM((N_KV_HEADS, s, LANES), BF16),
                        pltpu.VMEM((N_KV_HEADS, l, LANES), BF16), pltpu.VMEM((N_KV_HEADS, l, LANES), BF16)],
        name="window_attention",
    )(sink, p_x, p_x, p_x, p_c, p_c, cos, sin, cos, sin)


def _ctx_attn_kernel(sink_ref, q_ref, k_ref, v_ref, o_ref):
    kd = [t.astype(BF16) for t in _dup_heads(k_ref[0].astype(F32))]
    vd = [t.astype(BF16) for t in _dup_heads(v_ref[0].astype(F32))]
    scale = HEAD_DIM ** -0.5 * LOG2_E
    qps = [(q_ref[0, :, p * LANES:(p + 1) * LANES].astype(F32) * scale).astype(BF16) for p in range(N_HEADS // 2)]
    sinks = [sink_ref[h] * LOG2_E for h in range(N_HEADS)]
    outs = _heads_attention(qps, [[k] for k in kd], [[v] for v in vd], [[None]] * N_KV_HEADS, sinks)
    for p, o in enumerate(outs):
        o_ref[0, :, p * LANES:(p + 1) * LANES] = o.astype(o_ref.dtype)


def _context_attention(p_c, sink):
    b, l, _ = p_c.shape
    return pl.pallas_call(
        _ctx_attn_kernel,
        grid=(b,),
        in_specs=[pl.BlockSpec(memory_space=pltpu.SMEM),
                  pl.BlockSpec((1, l, Q_W), lambda i: (i, 0, COL_Q // Q_W)),
                  pl.BlockSpec((1, l, KV_W), lambda i: (i, 0, COL_K // KV_W)),
                  pl.BlockSpec((1, l, KV_W), lambda i: (i, 0, COL_V // KV_W))],
        out_specs=pl.BlockSpec((1, l, Q_W), lambda i: (i, 0, 0)),
        out_shape=jax.ShapeDtypeStruct((b, l, Q_W), BF16),
        name="context_attention",
    )(sink, p_c, p_c, p_c)


def _merge_kernel(ga_ref, gb_ref, gc_ref,
                  u_ref, u_p_ref, u_n_ref, pz_ref, pz_p_ref, pz_n_ref,
                  attn_ref, x_ref, g1_ref, sh2_ref, sc2_ref, n2g_ref,
                  dw_ref, dwb_ref, lng_ref, lnb_ref, wpool_ref, pscale_ref,
                  wa_ref, wb_ref, wc_ref, wo_ref, wrt_ref,
                  xo_ref, h2_ref, afft_ref,
                  uwin_ref, zwin_ref, *, seq):
    t = pl.program_id(1)
    tt = x_ref.shape[1]
    has_prev = (t > 0).astype(F32)
    has_next = (t < pl.num_programs(1) - 1).astype(F32)

    uwin_ref[0:HALO, :] = u_p_ref[0].astype(F32) * has_prev
    uwin_ref[HALO:HALO + tt, :] = u_ref[0].astype(F32)
    uwin_ref[HALO + tt:, :] = u_n_ref[0].astype(F32) * has_next
    first = HALO - CONV_PAD
    rows = tt + 2 * HALO
    acc_cols = []
    for cb in range(CONV_CH // LANES):
        cols = slice(cb * LANES, (cb + 1) * LANES)
        window = uwin_ref[:, cols]
        acc_c = jnp.zeros((tt, LANES), F32) + dwb_ref[:, cols]
        for shift in range(F32_SUBLANES):
            taps = [k for k in range(CONV_K) if (first + k) % F32_SUBLANES == shift]
            if not taps:
                continue
            shifted = window if shift == 0 else pltpu.roll(window, rows - shift, 0)
            for k in taps:
                off = first + k - shift
                acc_c = acc_c + shifted[off:off + tt] * dw_ref[k:k + 1, cols]
        acc_cols.append(acc_c)
    acc = jnp.concatenate(acc_cols, axis=-1)
    mu = jnp.mean(acc, axis=-1, keepdims=True)
    cen = acc - mu
    var = jnp.mean(cen * cen, axis=-1, keepdims=True)
    ln = cen * lax.rsqrt(var + EPS) * lng_ref[...] + lnb_ref[...]
    feat_b = (ln * _sigmoid(ln)).astype(BF16)

    zwin_ref[0:HALO, :] = pz_p_ref[0].astype(F32) * has_prev
    zwin_ref[HALO:HALO + tt, :] = pz_ref[0].astype(F32)
    zwin_ref[HALO + tt:, :] = pz_n_ref[0].astype(F32) * has_next
    tpos = t * tt + lax.broadcasted_iota(I32, (tt, 1), 0)
    pooled = []
    for gi, w in enumerate(POOL_WINDOWS):
        cols = slice(gi * POOL_GROUP, (gi + 1) * POOL_GROUP)
        tot = zwin_ref[pl.ds(HALO - w // 2, tt), cols]
        for d in range(1 - w // 2, w - w // 2):
            tot = tot + zwin_ref[pl.ds(HALO + d, tt), cols]
        cnt = (jnp.minimum(tpos + (w - w // 2), seq) - jnp.maximum(tpos - w // 2, 0)).astype(F32)
        diff = tot / cnt - zwin_ref[pl.ds(HALO, tt), cols]
        pooled.append(_dot(diff.astype(BF16), wpool_ref[gi]))
    feat_c = (jnp.concatenate(pooled, axis=-1) * pscale_ref[...]).astype(BF16)

    y_a = _dot(attn_ref[0], wa_ref[...])
    y_b = _dot(feat_b, wb_ref[...])
    y_c = _dot(feat_c, wc_ref[...])
    merged = ga_ref[0].astype(F32) * y_a + gb_ref[0].astype(F32) * y_b + gc_ref[0].astype(F32) * y_c
    xn = x_ref[0] + g1_ref[0] * _dot(merged.astype(BF16), wo_ref[...])
    xo_ref[0] = xn

    h2 = _rms_mod(xn, n2g_ref[...], sh2_ref[0], sc2_ref[0])
    h2_hi, h2_lo = _hi_lo(h2)
    h2_ref[0] = h2_hi
    ne = afft_ref.shape[1]
    by_hi = _dot_nt(wrt_ref[...], h2_hi)
    logits_t = by_hi[:ne] + by_hi[ne:] + _dot_nt(wrt_ref[:ne, :], h2_lo)
    et = jnp.exp(logits_t - jnp.max(logits_t, axis=0, keepdims=True))
    afft_ref[0] = et / jnp.sum(et, axis=0, keepdims=True)


def _merge(p, attn, x, g1, sh2, sc2, n2g, lw, tt):
    b, s, d = x.shape
    nh = tt // HALO
    last_h = s // HALO - 1
    e = N_EXPERTS

    def main(width, col):
        return pl.BlockSpec((1, tt, width), lambda i, j: (i, j, col))

    def prev(col):
        return pl.BlockSpec((1, HALO, CONV_CH), lambda i, j: (i, jnp.maximum(j * nh - 1, 0), col))

    def nxt(col):
        return pl.BlockSpec((1, HALO, CONV_CH), lambda i, j: (i, jnp.minimum((j + 1) * nh, last_h), col))

    def per_batch():
        return pl.BlockSpec((1, 1, d), lambda i, j: (i, 0, 0))

    def const(shape):
        return pl.BlockSpec(shape, lambda i, j: (0,) * len(shape))

    cu, pz = COL_CONV_U // CONV_CH, COL_POOL // CONV_CH
    in_specs = [main(d, 0), main(d, 1), main(d, 2),
                main(CONV_CH, cu), prev(cu), nxt(cu), main(POOL_CH, pz), prev(pz), nxt(pz),
                pl.BlockSpec((1, tt, Q_W), lambda i, j: (i, j, 0)),
                pl.BlockSpec((1, tt, d), lambda i, j: (i, j, 0)),
                per_batch(), per_batch(), per_batch(), const((1, d)),
                const((CONV_K, CONV_CH)), const((1, CONV_CH)), const((1, CONV_CH)), const((1, CONV_CH)),
                const((len(POOL_WINDOWS), POOL_GROUP, POOL_GROUP)), const((1, POOL_CH)),
                const((Q_W, d)), const((CONV_CH, d)), const((POOL_CH, d)), const((d, d)),
                const((2 * e, d))]
    out_specs = [pl.BlockSpec((1, tt, d), lambda i, j: (i, j, 0)),
                 pl.BlockSpec((1, tt, d), lambda i, j: (i, j, 0)),
                 pl.BlockSpec((1, e, tt), lambda i, j: (i, 0, j))]
    out_shape = [jax.ShapeDtypeStruct((b, s, d), F32), jax.ShapeDtypeStruct((b, s, d), BF16),
                 jax.ShapeDtypeStruct((b, e, s), F32)]
    return pl.pallas_call(
        functools.partial(_merge_kernel, seq=s),
        grid=(b, s // tt),
        in_specs=in_specs, out_specs=out_specs, out_shape=out_shape,
        scratch_shapes=[pltpu.VMEM((tt + 2 * HALO, CONV_CH), F32), pltpu.VMEM((tt + 2 * HALO, POOL_CH), F32)],
        name="mix_merge",
    )(p, p, p, p, p, p, p, p, p, attn, x, g1, sh2, sc2, n2g,
      lw['conv_dw'], lw['conv_dw_b'], lw['conv_ln_g'], lw['conv_ln_b'], lw['w_pool'], lw['pool_scale'],
      lw['w_attn_o'], lw['w_conv_o'], lw['w_pool_o'], lw['w_out'], lw['w_router_t'])


def _topk_kernel(afft_ref, slot_ref, slott_ref, offs_ref, *, cap, blk):
    a = afft_ref[0]
    e, s = a.shape

    def keeps_cap(cand):
        return jnp.sum((a >= cand).astype(F32), axis=-1, keepdims=True) >= cap

    def largest_kept(thr, cands):
        for cand in cands:
            thr = jnp.where(keeps_cap(cand), cand, thr)
        return thr

    tiny = jnp.full((e, 1), F32_TINY, F32)
    thr = jnp.where(keeps_cap(tiny), tiny, 0.0)
    for hi, lo in ((64, 32), (16, 8), (4, 2)):
        thr = largest_kept(thr, [thr * float(2 ** lo), thr * float(2 ** hi), thr * float(2 ** (hi + lo))])
    thr = largest_kept(thr, [thr * 2.0])
    delta = thr
    for _ in range(F32_MANTISSA_BITS // 2):
        d_hi, d_lo = delta * 0.5, delta * 0.25
        thr = largest_kept(thr, [thr + d_lo, thr + d_hi, thr + (d_hi + d_lo)])
        delta = d_lo
    if F32_MANTISSA_BITS % 2:
        thr = largest_kept(thr, [thr + delta * 0.5])
    gt = a > thr
    eq = a == thr
    need = cap - jnp.sum(gt.astype(F32), axis=-1, keepdims=True)

    r = lax.broadcasted_iota(I32, (blk, blk), 0)
    c = lax.broadcasted_iota(I32, (blk, blk), 1)
    upper = (r < c).astype(BF16)
    eye = (r == c).astype(F32)

    def prefix(mask_f32):
        carry = jnp.zeros((e, 1), F32)
        parts = []
        for j in range(s // blk):
            m = mask_f32[:, j * blk:(j + 1) * blk]
            parts.append(_dot(m.astype(BF16), upper) + carry)
            carry = carry + jnp.sum(m, axis=-1, keepdims=True)
        return jnp.concatenate(parts, axis=-1)

    sel = gt | (eq & (prefix(eq.astype(F32)) < need))
    pos = prefix(sel.astype(F32))
    slot = jnp.where(sel, pos, -1.0)
    slot_ref[0] = slot.astype(I32)
    for j in range(s // blk):
        slott_ref[0, j * blk:(j + 1) * blk, :] = _dot_nt(eye, slot[:, j * blk:(j + 1) * blk],
                                                        precision=HIGHEST).astype(I32)
    tok = lax.broadcasted_iota(I32, (s, LANES), 0)
    col = lax.broadcasted_iota(I32, (s, LANES), 1)
    before = (tok < col * blk).astype(BF16)
    offs_ref[0] = _dot(sel.astype(BF16), before).astype(I32)


def _topk(afft, cap):
    b, e, s = afft.shape
    blk = min(s, TOKEN_CHUNK)
    slot, slott, offs = pl.pallas_call(
        functools.partial(_topk_kernel, cap=cap, blk=blk),
        grid=(b,),
        in_specs=[pl.BlockSpec((1, e, s), lambda i: (i, 0, 0))],
        out_specs=[pl.BlockSpec((1, e, s), lambda i: (i, 0, 0)),
                   pl.BlockSpec((1, s, e), lambda i: (i, 0, 0)),
                   pl.BlockSpec((1, e, LANES), lambda i: (i, 0, 0))],
        out_shape=[jax.ShapeDtypeStruct((b, e, s), I32), jax.ShapeDtypeStruct((b, s, e), I32),
                   jax.ShapeDtypeStruct((b, e, LANES), I32)],
        name="expert_choice_topk",
    )(afft)
    return slot, slott, offs[:, :, :s // blk + 1]


def _sample_of_step(expert, step, n_samples):
    return jnp.where(expert % 2 == 0, step, n_samples - 1 - step)


def _slot_windows(offs_ref, idx, win):
    lo = offs_ref[idx]
    hi = offs_ref[idx + 1]
    first = lo // win
    return first, jnp.where(hi > lo, (hi - 1) // win - first + 1, 0)


def _gather_rows(onehot, h, gate_row):
    picked = _dot(onehot.astype(BF16), h)
    gates = jnp.sum(jnp.where(onehot, gate_row, 0.0), axis=-1, keepdims=True)
    return picked, gates


def _ffn_kernel(offs_ref, slot_ref, afft_ref, h_ref, slotc_ref, cafft_ref, hc_ref, wg_ref, wu_ref, wd_ref,
                ye_ref, yec_ref, wg_s, wu_s, wd_s, xe_s, g_s, *, chunk, group, win, rows):
    ex = pl.program_id(0)
    bi = pl.program_id(1)
    nb = pl.num_programs(1)
    cap = xe_s.shape[0]

    @pl.when(bi == 0)
    def _():
        def cast(i, carry):
            sl = pl.ds(pl.multiple_of(i * rows, rows), rows)
            wg_s[sl, :] = wg_ref[0, 0, sl, :].astype(BF16)
            wu_s[sl, :] = wu_ref[0, 0, sl, :].astype(BF16)
            wd_s[sl, :] = wd_ref[0, 0, sl, :].astype(BF16)
            return carry
        lax.fori_loop(0, wg_s.shape[0] // rows, cast, 0)

    def ffn(xe):
        a = _dot(xe, wg_s[...])
        u = _dot(xe, wu_s[...])
        hid = (a * _sigmoid(a) * u).astype(BF16)
        return _dot(hid, wd_s[...])

    nch = h_ref.shape[1] // chunk
    base = (_sample_of_step(ex, bi, nb) * pl.num_programs(0) + ex) * (nch + 1)
    xe_s[...] = jnp.zeros_like(xe_s)
    g_s[...] = jnp.zeros_like(g_s)
    span = group * chunk
    starts = []
    fits = None
    for p in range(nch // group):
        lo = offs_ref[base + p * group]
        hi = offs_ref[base + (p + 1) * group]
        a = pl.multiple_of(jnp.minimum((lo // F32_SUBLANES) * F32_SUBLANES, cap - win), F32_SUBLANES)
        starts.append(a)
        fits = (hi - a <= win) if fits is None else jnp.logical_and(fits, hi - a <= win)

    def add_window(a, sl, width):
        onehot = (slot_ref[0, pl.ds(ex, 1), sl] - a) == lax.broadcasted_iota(I32, (win, width), 0)
        picked, gates = _gather_rows(onehot, h_ref[0, sl, :], afft_ref[0, pl.ds(ex, 1), sl])
        xe_s[pl.ds(a, win), :] += picked
        g_s[pl.ds(a, win), :] += gates

    @pl.when(fits)
    def _():
        for p, a in enumerate(starts):
            add_window(a, slice(p * span, (p + 1) * span), span)

    @pl.when(jnp.logical_not(fits))
    def _():
        def per_chunk(j, carry):
            first, nwin = _slot_windows(offs_ref, base + j, win)
            sl = pl.ds(pl.multiple_of(j * chunk, chunk), chunk)

            def window(w, c):
                add_window(pl.multiple_of((first + w) * win, win), sl, chunk)
                return c
            lax.fori_loop(0, nwin, window, 0)
            return carry
        lax.fori_loop(0, nch, per_chunk, 0)

    if yec_ref is None:
        ye_ref[0] = (ffn(xe_s[...].astype(BF16)) * g_s[...]).astype(ye_ref.dtype)
        return

    @pl.when(bi < nb - 1)
    def _():
        ye_ref[0] = (ffn(xe_s[...].astype(BF16)) * g_s[...]).astype(ye_ref.dtype)

    @pl.when(bi == nb - 1)
    def _():
        rowc = lax.broadcasted_iota(I32, (yec_ref.shape[1], hc_ref.shape[0]), 0)
        picked, gates = _gather_rows(slotc_ref[0] == rowc, hc_ref[...], cafft_ref[0])
        xe_all = jnp.concatenate([xe_s[...].astype(BF16), picked.astype(BF16)], axis=0)
        y_all = ffn(xe_all) * jnp.concatenate([g_s[...], gates], axis=0)
        ye_ref[0] = y_all[:cap].astype(ye_ref.dtype)
        yec_ref[0] = y_all[cap:].astype(yec_ref.dtype)


def _expert_ffn(layer, slot, offs, afft, h2, wg, wu, wd, cap, ctx_part=None):
    b, e, s = slot.shape
    d = h2.shape[2]
    f = wg.shape[3]
    assert f == d
    chunk = min(s, TOKEN_CHUNK)
    nch = s // chunk
    group = 2 if nch % 2 == 0 else 1
    win = min(cap, GATHER_WINDOW)
    assert cap % win == 0

    def w_spec(from_step):
        if from_step < 1:
            return pl.BlockSpec((1, 1, d, f), lambda j, i, o: (layer, j, 0, 0))
        return pl.BlockSpec((1, 1, d, f),
                            lambda j, i, o: (layer, jnp.minimum(j + jnp.where(i >= from_step, 1, 0), e - 1), 0, 0))

    def sample(j, i):
        return _sample_of_step(j, i, b)

    row_spec = pl.BlockSpec((1, e, s), lambda j, i, o: (sample(j, i), 0, 0))
    in_specs = [row_spec, row_spec, pl.BlockSpec((1, s, d), lambda j, i, o: (sample(j, i), 0, 0))]
    out_specs = [pl.BlockSpec((1, cap, d), lambda j, i, o: (sample(j, i) * e + j, 0, 0))]
    out_shape = [jax.ShapeDtypeStruct((b * e, cap, d), BF16)]
    args = [slot, afft, h2]
    body = functools.partial(_ffn_kernel, chunk=chunk, group=group, win=win, rows=128)
    if ctx_part is None:
        def kern(offs_ref, slot_ref, afft_ref, h_ref, wg_ref, wu_ref, wd_ref, ye_ref, *scratch):
            body(offs_ref, slot_ref, afft_ref, h_ref, None, None, None, wg_ref, wu_ref, wd_ref, ye_ref, None,
                 *scratch)
    else:
        slot_c, afft_c, h_c, rows_c = ctx_part
        n_c = h_c.shape[0]
        rowc_spec = pl.BlockSpec((1, 1, n_c), lambda j, i, o: (j, 0, 0))
        in_specs += [rowc_spec, rowc_spec, pl.BlockSpec((n_c, d), lambda j, i, o: (0, 0))]
        out_specs.append(pl.BlockSpec((1, rows_c, d), lambda j, i, o: (j, 0, 0)))
        out_shape.append(jax.ShapeDtypeStruct((e, rows_c, d), BF16))
        args += [slot_c, afft_c, h_c]
        kern = body
    outs = pl.pallas_call(
        kern,
        grid_spec=pltpu.PrefetchScalarGridSpec(
            num_scalar_prefetch=1, grid=(e, b),
            in_specs=in_specs + [w_spec(min(k, b - 1)) for k in (1, 2, 3)], out_specs=out_specs,
            scratch_shapes=[pltpu.VMEM((d, f), BF16), pltpu.VMEM((d, f), BF16), pltpu.VMEM((f, d), BF16),
                            pltpu.VMEM((cap, d), F32), pltpu.VMEM((cap, 1), F32)]),
        out_shape=out_shape,
        name="expert_ffn",
    )(offs.reshape(-1), *args, wg, wu, wd)
    ye = outs[0].reshape(b, e * cap, d)
    return ye if ctx_part is None else (ye, outs[1])


def _combine_kernel(offs_ref, slott_ref, ye_ref, x_ref, g2_ref, fg_ref, o_ref, acc_s, *, cap, win, final_norm):
    bi = pl.program_id(0)
    tt = acc_s.shape[0]
    n_sub = x_ref.shape[1] // tt
    nch = pl.num_programs(1) * n_sub
    per_block = LANES // win

    for sub in range(n_sub):
        j = pl.program_id(1) * n_sub + sub
        tok = slice(sub * tt, (sub + 1) * tt)

        starts = []
        fits = None
        for ex in range(N_EXPERTS):
            idx = (bi * N_EXPERTS + ex) * (nch + 1) + j
            a = pl.multiple_of(jnp.minimum((offs_ref[idx] // BF16_SUBLANES) * BF16_SUBLANES, cap - win),
                               BF16_SUBLANES)
            starts.append(a)
            ok = offs_ref[idx + 1] - a <= win
            fits = ok if fits is None else jnp.logical_and(fits, ok)

        @pl.when(fits)
        def _(starts=starts, tok=tok):
            lane = lax.broadcasted_iota(I32, (tt, LANES), 1)
            blocks = []
            for blk in range(N_EXPERTS // per_block):
                target = None
                for q in range(per_block):
                    ex = blk * per_block + q
                    t = slott_ref[0, tok, ex:ex + 1] - starts[ex] + q * win
                    target = t if target is None else jnp.where(lane >= q * win, t, target)
                blocks.append((target == lane).astype(BF16))
            rows = [ye_ref[0, pl.ds(ex * cap + starts[ex], win), :] for ex in range(N_EXPERTS)]
            acc_s[...] = _dot(jnp.concatenate(blocks, axis=1), jnp.concatenate(rows, axis=0))

        @pl.when(jnp.logical_not(fits))
        def _(j=j, tok=tok):
            lane = lax.broadcasted_iota(I32, (tt, win), 1)
            acc_s[...] = jnp.zeros_like(acc_s)
            for ex in range(N_EXPERTS):
                first, nwin = _slot_windows(offs_ref, (bi * N_EXPERTS + ex) * (nch + 1) + j, win)

                def window(w, c, ex=ex, first=first):
                    a = pl.multiple_of((first + w) * win, win)
                    onehot = ((slott_ref[0, tok, ex:ex + 1] - a) == lane).astype(BF16)
                    acc_s[...] += _dot(onehot, ye_ref[0, pl.ds(ex * cap + a, win), :])
                    return c
                lax.fori_loop(0, nwin, window, 0)

        out = x_ref[0, tok, :] + g2_ref[0] * acc_s[...]
        if final_norm:
            out = out * lax.rsqrt(jnp.mean(out * out, axis=-1, keepdims=True) + EPS) * fg_ref[...]
        o_ref[0, tok, :] = out


def _combine(slott, offs, ye, x, g2, fg, cap, final_norm):
    b, s, d = x.shape
    e = N_EXPERTS
    chunk = min(s, TOKEN_CHUNK)
    tt = min(s, COMBINE_CHUNKS * chunk)
    win = min(cap, SCATTER_WINDOW)
    assert cap % win == 0 and LANES % win == 0 and e % (LANES // win) == 0 and s % tt == 0
    return pl.pallas_call(
        functools.partial(_combine_kernel, cap=cap, win=win, final_norm=final_norm),
        grid_spec=pltpu.PrefetchScalarGridSpec(
            num_scalar_prefetch=1, grid=(b, s // tt),
            in_specs=[pl.BlockSpec((1, tt, e), lambda i, j, o: (i, j, 0)),
                      pl.BlockSpec((1, e * cap, d), lambda i, j, o: (i, 0, 0)),
                      pl.BlockSpec((1, tt, d), lambda i, j, o: (i, j, 0)),
                      pl.BlockSpec((1, 1, d), lambda i, j, o: (i, 0, 0)),
                      pl.BlockSpec((1, d), lambda i, j, o: (0, 0))],
            out_specs=pl.BlockSpec((1, tt, d), lambda i, j, o: (i, j, 0)),
            scratch_shapes=[pltpu.VMEM((chunk, d), F32)]),
        out_shape=jax.ShapeDtypeStruct((b, s, d), F32),
        name="moe_combine",
    )(offs.reshape(-1), slott, ye, x, g2, fg)


def _split_hi_lo(w):
    return jnp.concatenate(_hi_lo(w), axis=0)


def _permute_in_cols(w):
    o_k = Q_W
    o_v = o_k + KV_W
    o_ca = o_v + KV_W
    o_cg = o_ca + CONV_CH
    o_p = o_cg + CONV_CH
    o_g = o_p + POOL_CH
    return jnp.concatenate([w[:, o_g:], w[:, :o_k], w[:, o_ca:o_cg], w[:, o_cg:o_p], w[:, o_p:o_g],
                            w[:, o_k:o_v], w[:, o_v:o_ca]], axis=1)


def kernel(x, c, ctx, c_ctx, norm1_g, norm2_g, w_mod, b_mod, w_in, attn_sink, w_attn_o, conv_dw, conv_dw_b,
           conv_ln_g, conv_ln_b, w_conv_o, w_pool, pool_scale, w_pool_o, w_out, w_router, w_e_gate, w_e_up,
           w_e_down, final_norm_g):
    b, s, d = x.shape
    l = ctx.shape[1]
    depth = w_in.shape[0]
    assert d == _D_MODEL and w_in.shape[2] == IN_W and CONV_CH == POOL_CH

    tabs = _rope_tables(s)
    cc = jnp.zeros((8, d), F32).at[:b].set(c).at[b].set(c_ctx)
    mod = _modulation(cc, w_mod, b_mod)
    fg = final_norm_g.reshape(1, d)

    for layer in range(depth):
        last = layer == depth - 1
        mx = mod[layer, :b].reshape(b, 1, 6, d)
        sh1, sc1, g1, sh2, sc2, g2 = [mx[:, :, i] for i in range(6)]
        mc = jnp.broadcast_to(mod[layer, b].reshape(1, 1, 6, d), (b, 1, 6, d))
        csh1, csc1, cg1, csh2, csc2, cg2 = [mc[:, :, i] for i in range(6)]
        n1g = norm1_g[layer].reshape(1, d)
        n2g = norm2_g[layer].reshape(1, d)
        w_in_l = _permute_in_cols(w_in[layer]).astype(BF16)
        lw = {'conv_dw': conv_dw[layer], 'conv_dw_b': conv_dw_b[layer].reshape(1, -1),
              'conv_ln_g': conv_ln_g[layer].reshape(1, -1), 'conv_ln_b': conv_ln_b[layer].reshape(1, -1),
              'w_pool': w_pool[layer].astype(BF16), 'pool_scale': pool_scale[layer].reshape(1, -1),
              'w_attn_o': w_attn_o[layer].astype(BF16), 'w_conv_o': w_conv_o[layer].astype(BF16),
              'w_pool_o': w_pool_o[layer].astype(BF16), 'w_out': w_out[layer].astype(BF16),
              'w_router_t': _split_hi_lo(w_router[layer].T)}

        p_x = _inproj(x, n1g, sh1, sc1, w_in_l, tm=min(s, INPROJ_TILE))
        if last:
            p_c = _inproj(ctx, n1g, csh1, csc1, w_in_l[:, WCOL_K:], tm=l, mixer_epilogue=False)
            ctx_kv_cols = (0, KV_W)
        else:
            p_c = _inproj(ctx, n1g, csh1, csc1, w_in_l, tm=l)
            ctx_kv_cols = (COL_K, COL_V)
        attn_x = _window_attention(p_x, p_c, ctx_kv_cols, attn_sink[layer], tabs)
        x_mid, h2, afft = _merge(p_x, attn_x, x, g1, sh2, sc2, n2g, lw, tt=min(s, MERGE_TILE))
        if not last:
            attn_c = _context_attention(p_c, attn_sink[layer])
            c_mid, ch2, cafft = _merge(p_c, attn_c, ctx, cg1, csh2, csc2, n2g, lw, tt=l)
        cap = (CAPACITY_FACTOR * s) // N_EXPERTS
        slot, slott, offs = _topk(afft, cap)
        if last:
            ye = _expert_ffn(layer, slot, offs, afft, h2, w_e_gate, w_e_up, w_e_down, cap)
        else:
            cap_c = (CAPACITY_FACTOR * l) // N_EXPERTS
            cslot, cslott, coffs = _topk(cafft, cap_c)
            sample_base = (jnp.arange(b, dtype=I32) * cap_c)[:, None, None]
            cslot_all = jnp.where(cslot >= 0, cslot + sample_base, -1)
            cslot_all = cslot_all.transpose(1, 0, 2).reshape(N_EXPERTS, 1, b * l)
            cafft_all = cafft.transpose(1, 0, 2).reshape(N_EXPERTS, 1, b * l)
            ye, yec = _expert_ffn(layer, slot, offs, afft, h2, w_e_gate, w_e_up, w_e_down, cap,
                                  ctx_part=(cslot_all, cafft_all, ch2.reshape(b * l, d), b * cap_c))
            yec = yec.reshape(N_EXPERTS, b, cap_c, d).transpose(1, 0, 2, 3).reshape(b, N_EXPERTS * cap_c, d)
            ctx = _combine(cslott, coffs, yec, c_mid, cg2, fg, cap_c, False)
        x = _combine(slott, offs, ye, x_mid, g2, fg, cap, last)
    return x
```
